```python
import jax
import jax.numpy as jnp
from jax import lax
import numpy as np

D_MODEL = 1024
BATCH = 8
SEQ = 8192
DEPTH = 2
DEC_BATCH = 16
DEC_SEQ = 2048
PAST_LEN = 128

HEAD_DIM = 64
A_Q_HEADS = 4
A_KV_HEADS = 2
A_WINDOW = 128
B_HEADS = 6
B_NOPE = 64
B_ROPE = 32
B_V = 64
B_Q_LORA = 384
B_KV_LORA = 256
ROPE_THETA = 10000.0
Q_BLOCK = 128
C_HEADS = 6
DILATED_PAIRS = ((128, 1), (512, 4), (2048, 16))
A_WIDTH = A_Q_HEADS * HEAD_DIM
B_WIDTH = B_HEADS * B_V
C_WIDTH = C_HEADS * HEAD_DIM
MIX_WIDTH = A_WIDTH + B_WIDTH + C_WIDTH
IN_SIZES = (A_WIDTH, A_KV_HEADS * HEAD_DIM, A_KV_HEADS * HEAD_DIM, B_Q_LORA, B_KV_LORA, B_ROPE, C_WIDTH, C_WIDTH, C_WIDTH)
IN_COLS = sum(IN_SIZES)
MEM_LEN = 256
X_HEADS = 4
X_HEAD_DIM = 128
D_FF = 2816
N_EXPERTS = 8
TOP_K = 2
D_FF_EXPERT = 3584
N_DENSE = (DEPTH + 1) // 2
N_MOE = DEPTH // 2
EPS = 1e-6
NEG = -1e30

kernel_name = 'hybrid_bidir_encoder_two_groups'


def rmsnorm(x, g):
    xf = x.astype(jnp.float32)
    y = xf * lax.rsqrt(jnp.mean(xf * xf, axis=-1, keepdims=True) + EPS)
    return (y * g.astype(jnp.float32)).astype(x.dtype)


def alibi_slopes(n):
    return 2.0 ** (-8.0 * jnp.arange(1, n + 1, dtype=jnp.float32) / n)


def rope(x, pos):
    half = x.shape[-1] // 2
    inv = ROPE_THETA ** (-jnp.arange(half, dtype=jnp.float32) / half)
    ang = pos[:, None] * inv[None, :]
    cos = jnp.cos(ang)[None, :, None, :]
    sin = jnp.sin(ang)[None, :, None, :]
    xf = x.astype(jnp.float32)
    x1, x2 = xf[..., :half], xf[..., half:]
    return jnp.concatenate([x1 * cos - x2 * sin, x2 * cos + x1 * sin], axis=-1).astype(x.dtype)


def split_cols(z, sizes):
    offs = [int(o) for o in np.cumsum(sizes)[:-1]]
    return jnp.split(z, offs, axis=-1)


def banded_attention(q, k, v, slopes, step, half_window, sink=None):
    bsz, L, hk, g, d = q.shape
    W = half_window
    nb = -(-L // W)
    pad = nb * W - L
    qb = jnp.pad(q, ((0, 0), (0, pad), (0, 0), (0, 0), (0, 0))).reshape(bsz, nb, W, hk, g, d)

    def key_blocks(t):
        tp = jnp.pad(t, ((0, 0), (W, W + pad), (0, 0), (0, 0))).reshape(bsz, nb + 2, W, hk, d)
        return jnp.concatenate([tp[:, :-2], tp[:, 1:-1], tp[:, 2:]], axis=2)

    kb, vb = key_blocks(k), key_blocks(v)
    s = jnp.einsum('bnqhgd,bnkhd->bnhgqk', qb, kb, preferred_element_type=jnp.float32) * (d ** -0.5)
    qpos = jnp.arange(nb * W).reshape(nb, W)
    kpos = jnp.arange(nb)[:, None] * W - W + jnp.arange(3 * W)[None, :]
    dist = jnp.abs(qpos[:, :, None] - kpos[:, None, :])
    valid = (dist <= W) & (kpos[:, None, :] >= 0) & (kpos[:, None, :] < L)
    bias = -slopes[None, None, :, :, None, None] * (step * dist).astype(jnp.float32)[None, :, None, None, :, :]
    s = jnp.where(valid[None, :, None, None], s + bias, NEG)
    m = jnp.max(s, axis=-1)
    if sink is not None:
        sk = sink.astype(jnp.float32)[None, None, :, :, None]
        m = jnp.maximum(m, sk)
    p = jnp.exp(s - m[..., None])
    denom = jnp.sum(p, axis=-1)
    if sink is not None:
        denom = denom + jnp.exp(sk - m)
    o = jnp.einsum('bnhgqk,bnkhd->bnqhgd', p, vb.astype(jnp.float32))
    o = o / jnp.transpose(denom, (0, 1, 4, 2, 3))[..., None]
    lse = jnp.transpose(m + jnp.log(denom), (0, 1, 4, 2, 3))
    o = o.reshape(bsz, nb * W, hk, g, d)[:, :L].astype(q.dtype)
    lse = lse.reshape(bsz, nb * W, hk, g)[:, :L]
    return o, lse


def windowed_gqa_sink(qa, ka, va, sink):
    bsz, L, _ = qa.shape
    g = A_Q_HEADS // A_KV_HEADS
    q = qa.reshape(bsz, L, A_KV_HEADS, g, HEAD_DIM)
    k = ka.reshape(bsz, L, A_KV_HEADS, HEAD_DIM)
    v = va.reshape(bsz, L, A_KV_HEADS, HEAD_DIM)
    slopes = alibi_slopes(A_Q_HEADS).reshape(A_KV_HEADS, g)
    o, _ = banded_attention(q, k, v, slopes, 1, A_WINDOW, sink.reshape(A_KV_HEADS, g))
    return o.reshape(bsz, L, A_WIDTH)


def dense_blocked_attention(q, k, v):
    bsz, L, h, dqk = q.shape
    nq = L // Q_BLOCK
    qb = jnp.transpose(q.reshape(bsz, nq, Q_BLOCK, h, dqk), (1, 0, 2, 3, 4))
    vf = v.astype(jnp.float32)
    scale = dqk ** -0.5

    def one_block(qi):
        s = jnp.einsum('bqhd,bkhd->bhqk', qi, k, preferred_element_type=jnp.float32) * scale
        p = jax.nn.softmax(s, axis=-1)
        return jnp.einsum('bhqk,bkhd->bqhd', p, vf).astype(q.dtype)

    ob = lax.map(one_block, qb)
    return jnp.transpose(ob, (1, 0, 2, 3, 4)).reshape(bsz, L, h, v.shape[-1])


def mla(c_q, c_kv, k_rope, g_q, g_kv, w_uq, w_ukv):
    bsz, L, _ = c_q.shape
    pos = jnp.arange(L, dtype=jnp.float32)
    q = (rmsnorm(c_q, g_q) @ w_uq).reshape(bsz, L, B_HEADS, B_NOPE + B_ROPE)
    kv = (rmsnorm(c_kv, g_kv) @ w_ukv).reshape(bsz, L, B_HEADS, B_NOPE + B_V)
    q = jnp.concatenate([q[..., :B_NOPE], rope(q[..., B_NOPE:], pos)], axis=-1)
    kr = rope(k_rope[:, :, None, :], pos)
    k = jnp.concatenate([kv[..., :B_NOPE], jnp.broadcast_to(kr, (bsz, L, B_HEADS, B_ROPE))], axis=-1)
    v = kv[..., B_NOPE:]
    return dense_blocked_attention(q, k, v).reshape(bsz, L, B_WIDTH)


def dilated_mixture(qc, kc, vc):
    bsz, L, _ = qc.shape
    q = qc.reshape(bsz, L, C_HEADS, HEAD_DIM)
    k = kc.reshape(bsz, L, C_HEADS, HEAD_DIM)
    v = vc.reshape(bsz, L, C_HEADS, HEAD_DIM)
    slopes = alibi_slopes(C_HEADS)[:, None]
    outs, lses = [], []
    for (w, r) in DILATED_PAIRS:
        half = w // (2 * r)
        ls = L // r

        def to_sub(t):
            return jnp.transpose(t.reshape(bsz, ls, r, C_HEADS, HEAD_DIM), (0, 2, 1, 3, 4)).reshape(bsz * r, ls, C_HEADS, HEAD_DIM)

        o, lse = banded_attention(to_sub(q)[:, :, :, None, :], to_sub(k), to_sub(v), slopes, r, half)
        o = jnp.transpose(o[:, :, :, 0].reshape(bsz, r, ls, C_HEADS, HEAD_DIM), (0, 2, 1, 3, 4)).reshape(bsz, L, C_HEADS, HEAD_DIM)
        lse = jnp.transpose(lse[..., 0].reshape(bsz, r, ls, C_HEADS), (0, 2, 1, 3)).reshape(bsz, L, C_HEADS)
        outs.append(o)
        lses.append(lse)
    wts = jax.nn.softmax(jnp.stack(lses, axis=-1), axis=-1)
    out = jnp.einsum('blhdp,blhp->blhd', jnp.stack(outs, axis=-1).astype(jnp.float32), wts)
    return out.astype(qc.dtype).reshape(bsz, L, C_WIDTH)


def parallel_mixer(h, w_in, sink, g_q, g_kv, w_uq, w_ukv, g_out, w_out):
    qa, ka, va, cq, ckv, kr, qc, kc, vc = split_cols(h @ w_in, IN_SIZES)
    ya = windowed_gqa_sink(qa, ka, va, sink)
    yb = mla(cq, ckv, kr, g_q, g_kv, w_uq, w_ukv)
    yc = dilated_mixture(qc, kc, vc)
    y = jnp.concatenate([
        rmsnorm(ya, g_out[:A_WIDTH]),
        rmsnorm(yb, g_out[A_WIDTH:A_WIDTH + B_WIDTH]),
        rmsnorm(yc, g_out[A_WIDTH + B_WIDTH:]),
    ], axis=-1)
    return y @ w_out


def memory_cross_attention(h, mem, g_mem, w_q, w_kv, w_o):
    bsz, L, _ = h.shape
    m = rmsnorm(mem, g_mem)
    q = (h @ w_q).reshape(bsz, L, X_HEADS, X_HEAD_DIM)
    kv = (m @ w_kv).reshape(bsz, mem.shape[1], 2, X_HEADS, X_HEAD_DIM)
    s = jnp.einsum('blhd,bmhd->bhlm', q, kv[:, :, 0], preferred_element_type=jnp.float32) * (X_HEAD_DIM ** -0.5)
    p = jax.nn.softmax(s, axis=-1)
    o = jnp.einsum('bhlm,bmhd->blhd', p, kv[:, :, 1].astype(jnp.float32)).astype(h.dtype)
    return o.reshape(bsz, L, X_HEADS * X_HEAD_DIM) @ w_o


def swiglu(h, w_gu, w_down):
    g, u = jnp.split(h @ w_gu, 2, axis=-1)
    return (jax.nn.silu(g) * u) @ w_down


def moe_swiglu(h, w_router, w_gu, w_down):
    logits = jnp.einsum('bld,de->ble', h, w_router, preferred_element_type=jnp.float32)
    top_val, top_idx = lax.top_k(logits, TOP_K)
    top_w = jax.nn.softmax(top_val, axis=-1)
    gates = jnp.einsum('blk,blke->ble', top_w, jax.nn.one_hot(top_idx, N_EXPERTS, dtype=jnp.float32))
    y = jnp.zeros(h.shape, jnp.float32)
    for e in range(N_EXPERTS):
        y = y + gates[..., e:e + 1] * swiglu(h, w_gu[e], w_down[e]).astype(jnp.float32)
    return y.astype(h.dtype)


def encoder_trunk(x, mem, norm_mix_g, w_in, sink_a, mla_q_norm_g, mla_kv_norm_g, mla_w_uq, mla_w_ukv,
                  mix_out_norm_g, w_out, norm_x_g, norm_mem_g, w_xq, w_xkv, w_xo, norm_ffn_g,
                  ffn_w_gu, ffn_w_down, moe_router, moe_w_gu, moe_w_down, final_norm_g):
    for l in range(DEPTH):
        x = x + parallel_mixer(rmsnorm(x, norm_mix_g[l]), w_in[l], sink_a[l], mla_q_norm_g[l], mla_kv_norm_g[l],
                               mla_w_uq[l], mla_w_ukv[l], mix_out_norm_g[l], w_out[l])
        x = x + memory_cross_attention(rmsnorm(x, norm_x_g[l]), mem, norm_mem_g[l], w_xq[l], w_xkv[l], w_xo[l])
        hn = rmsnorm(x, norm_ffn_g[l])
        if l % 2 == 0:
            x = x + swiglu(hn, ffn_w_gu[l // 2], ffn_w_down[l // 2])
        else:
            x = x + moe_swiglu(hn, moe_router[l // 2], moe_w_gu[l // 2], moe_w_down[l // 2])
    return rmsnorm(x, final_norm_g)


def setup_inputs(seed: int = 0) -> dict:
    key = jax.random.key(seed)
    ks = iter(jax.random.split(key, 40))
    f32 = jnp.float32

    def act(shape):
        return jax.random.normal(next(ks), shape, f32)

    def w(shape, fan_in):
        return jax.random.normal(next(ks), shape, f32) * (fan_in ** -0.5)

    def gain(shape):
        return 1.0 + 0.02 * jax.random.normal(next(ks), shape, f32)

    xd = X_HEADS * X_HEAD_DIM
    return {
        'x_prompt': act((BATCH, SEQ, D_MODEL)),
        'x_sample': act((DEC_BATCH, DEC_SEQ, D_MODEL)),
        'mem_prompt': act((BATCH, MEM_LEN, D_MODEL)),
        'mem_sample': act((DEC_BATCH, MEM_LEN, D_MODEL)),
        'norm_mix_g': gain((DEPTH, D_MODEL)),
        'w_in': w((DEPTH, D_MODEL, IN_COLS), D_MODEL),
        'sink_a': act((DEPTH, A_Q_HEADS)),
        'mla_q_norm_g': gain((DEPTH, B_Q_LORA)),
        'mla_kv_norm_g': gain((DEPTH, B_KV_LORA)),
        'mla_w_uq': w((DEPTH, B_Q_LORA, B_HEADS * (B_NOPE + B_ROPE)), B_Q_LORA),
        'mla_w_ukv': w((DEPTH, B_KV_LORA, B_HEADS * (B_NOPE + B_V)), B_KV_LORA),
        'mix_out_norm_g': gain((DEPTH, MIX_WIDTH)),
        'w_out': w((DEPTH, MIX_WIDTH, D_MODEL), MIX_WIDTH),
        'norm_x_g': gain((DEPTH, D_MODEL)),
        'norm_mem_g': gain((DEPTH, D_MODEL)),
        'w_xq': w((DEPTH, D_MODEL, xd), D_MODEL),
        'w_xkv': w((DEPTH, D_MODEL, 2 * xd), D_MODEL),
        'w_xo': w((DEPTH, xd, D_MODEL), xd),
        'norm_ffn_g': gain((DEPTH, D_MODEL)),
        'ffn_w_gu': w((N_DENSE, D_MODEL, 2 * D_FF), D_MODEL),
        'ffn_w_down': w((N_DENSE, D_FF, D_MODEL), D_FF),
        'moe_router': w((N_MOE, D_MODEL, N_EXPERTS), D_MODEL),
        'moe_w_gu': w((N_MOE, N_EXPERTS, D_MODEL, 2 * D_FF_EXPERT), D_MODEL),
        'moe_w_down': w((N_MOE, N_EXPERTS, D_FF_EXPERT, D_MODEL), D_FF_EXPERT),
        'final_norm_g': gain((D_MODEL,)),
    }


def reference(x_prompt, x_sample, mem_prompt, mem_sample, norm_mix_g, w_in, sink_a, mla_q_norm_g, mla_kv_norm_g,
              mla_w_uq, mla_w_ukv, mix_out_norm_g, w_out, norm_x_g, norm_mem_g, w_xq, w_xkv, w_xo, norm_ffn_g,
              ffn_w_gu, ffn_w_down, moe_router, moe_w_gu, moe_w_down, final_norm_g):
    weights = (norm_mix_g, w_in, sink_a, mla_q_norm_g, mla_kv_norm_g, mla_w_uq, mla_w_ukv, mix_out_norm_g, w_out,
               norm_x_g, norm_mem_g, w_xq, w_xkv, w_xo, norm_ffn_g, ffn_w_gu, ffn_w_down, moe_router, moe_w_gu,
               moe_w_down, final_norm_g)
    y_prompt = encoder_trunk(x_prompt, mem_prompt, *weights)
    y_sample = encoder_trunk(x_sample, mem_sample, *weights)
    return (y_prompt, y_sample)
```

```python
import functools
import math

import numpy as np
import jax
import jax.numpy as jnp
from jax import lax
from jax.experimental import pallas as pl
from jax.experimental.pallas import tpu as pltpu

D_MODEL = 1024
HEAD_DIM = 64
A_Q_HEADS = 4
A_KV_HEADS = 2
A_WINDOW = 128
B_HEADS = 6
B_NOPE = 64
B_ROPE = 32
B_V = 64
B_Q_LORA = 384
B_KV_LORA = 256
ROPE_THETA = 10000.0
C_HEADS = 6
DILATED_PAIRS = ((128, 1), (512, 4), (2048, 16))
A_WIDTH = A_Q_HEADS * HEAD_DIM
B_WIDTH = B_HEADS * B_V
C_WIDTH = C_HEADS * HEAD_DIM
IN_SIZES = (A_WIDTH, A_KV_HEADS * HEAD_DIM, A_KV_HEADS * HEAD_DIM, B_Q_LORA, B_KV_LORA, B_ROPE,
            C_WIDTH, C_WIDTH, C_WIDTH)
X_HEADS = 4
X_HEAD_DIM = 128
N_EXPERTS = 8
TOP_K = 2
EPS = 1e-6
NEG = -1e30

LANES = 128
HALF = LANES // 2
LSE_LANES = 8
VMEM_LIMIT = 56 * 1024 * 1024
LOG2E = math.log2(math.e)

BF16 = jnp.bfloat16
F32 = jnp.float32

A_HEAD_ORDER = (0, 2, 1, 3)


def _params(*sem):
    return pltpu.CompilerParams(dimension_semantics=sem, vmem_limit_bytes=VMEM_LIMIT)


def _rms(x, g):
    return x * lax.rsqrt(jnp.mean(x * x, axis=-1, keepdims=True) + EPS) * g


def _dot(a, b):
    return jnp.dot(a, b, preferred_element_type=F32)


def _dot_nt(a, b):
    return lax.dot_general(a, b, (((1,), (1,)), ((), ())), preferred_element_type=F32)


def _const_spec(shape):
    n = len(shape)
    return pl.BlockSpec(shape, lambda *_: (0,) * n)


def _in_kernel(x_ref, g_ref, wa_ref, wc_ref, wb_ref, wkr_ref, gq_ref, gkv_ref, wuq_ref, wukv_ref,
               cq_ref, sq_ref, ck_ref, sk_ref,
               qa_ref, ka_ref, va_ref, qc_ref, kc_ref, vc_ref, qb_ref, kb_ref, vb_ref):
    h = _rms(x_ref[...], g_ref[...]).astype(BF16)
    za = _dot(h, wa_ref[...])
    qa_ref[...] = za[:, :A_WIDTH].astype(BF16)
    ka_ref[...] = za[:, A_WIDTH:A_WIDTH + LANES].astype(BF16)
    va_ref[...] = za[:, A_WIDTH + LANES:].astype(BF16)
    zc = _dot(h, wc_ref[...])
    qc_ref[...] = zc[:, :C_WIDTH].astype(BF16)
    kc_ref[...] = zc[:, C_WIDTH:2 * C_WIDTH].astype(BF16)
    vc_ref[...] = zc[:, 2 * C_WIDTH:].astype(BF16)

    zb = _dot(h, wb_ref[...])
    hq = _rms(zb[:, :B_Q_LORA], gq_ref[...]).astype(BF16)
    hkv = _rms(zb[:, B_Q_LORA:], gkv_ref[...]).astype(BF16)
    zq = _dot(hq, wuq_ref[...])
    zkv = _dot(hkv, wukv_ref[...])
    zkr = _dot(h, wkr_ref[...])
    kr = zkr[:, :LANES] * ck_ref[...] + zkr[:, LANES:] * sk_ref[...]
    cq = cq_ref[...]
    sq = sq_ref[...]
    lane = lax.broadcasted_iota(jnp.int32, (1, LANES), 1)
    nb = B_HEADS * LANES
    for hd in range(B_HEADS):
        lo, hi = hd * LANES, (hd + 1) * LANES
        qb_ref[hd] = (zq[:, lo:hi] * cq + zq[:, nb + lo:nb + hi] * sq).astype(BF16)
        kb_ref[hd] = (zkv[:, lo:hi] + kr).astype(BF16)
        ones = (lane == (HALF if hd % 2 == 0 else 0)).astype(F32)
        vb_ref[hd] = (zkv[:, nb + lo:nb + hi] + ones).astype(BF16)


def _in_proj(x2, lw, tabs, L, tm):
    T = x2.shape[0]
    nl = L // tm
    row = lambda w: pl.BlockSpec((tm, w), lambda i: (i, 0))
    tab = pl.BlockSpec((tm, LANES), lambda i: (i % nl, 0))
    hm = pl.BlockSpec((B_HEADS, tm, LANES), lambda i: (0, i, 0))
    weights = (lw['g_mix'], lw['wa'], lw['wc'], lw['wb'], lw['wkr'], lw['g_q'], lw['g_kv'], lw['wuq'], lw['wukv'])
    out_shape = ([jax.ShapeDtypeStruct((T, w), BF16) for w in (A_WIDTH, LANES, LANES, C_WIDTH, C_WIDTH, C_WIDTH)]
                 + [jax.ShapeDtypeStruct((B_HEADS, T, LANES), BF16)] * 3)
    return pl.pallas_call(
        _in_kernel,
        grid=(T // tm,),
        in_specs=[row(D_MODEL)] + [_const_spec(w.shape) for w in weights] + [tab] * 4,
        out_specs=[row(A_WIDTH), row(LANES), row(LANES), row(C_WIDTH), row(C_WIDTH), row(C_WIDTH), hm, hm, hm],
        out_shape=out_shape,
        compiler_params=_params("parallel"),
        name="in_proj",
    )(x2, *weights, *tabs)


def _banded_kernel(*refs, W, R, n_kg, G, has_sink, want_lse, nchunks):
    it = iter(refs)
    q_ref = next(it)
    kp_ref, km_ref, kn_ref = next(it), next(it), next(it)
    vp_ref, vm_ref, vn_ref = next(it), next(it), next(it)
    bias_ref = next(it)
    sink_ref = next(it) if has_sink else None
    o_ref = next(it)
    lse_ref = next(it) if want_lse else None
    kfull, vfull = next(it), next(it)

    c = pl.program_id(2)
    kfull[0:W] = kp_ref[...]
    kfull[W:W + R] = km_ref[...]
    kfull[W + R:] = kn_ref[...]
    vfull[0:W] = vp_ref[...]
    vfull[W:W + R] = vm_ref[...]
    vfull[W + R:] = vn_ref[...]

    nsub = R // W
    lo = lax.broadcasted_iota(jnp.int32, (W, LANES), 1) < HALF
    col = lax.broadcasted_iota(jnp.int32, (W, 3 * W), 1)
    lane8 = lax.broadcasted_iota(jnp.int32, (W, LSE_LANES), 1)
    before_start = jnp.logical_and(c == 0, col < W)
    after_end = jnp.logical_and(c == nchunks - 1, col >= 2 * W)
    for i in range(nsub):
        lse_tile = jnp.zeros((W, LSE_LANES), F32)
        for kg in range(n_kg):
            kwin = kfull[i * W:(i + 3) * W, kg * LANES:(kg + 1) * LANES]
            vwin = vfull[i * W:(i + 3) * W, kg * LANES:(kg + 1) * LANES]
            for g in range(G):
                qg = kg * G + g
                qblk = q_ref[i * W:(i + 1) * W, qg * LANES:(qg + 1) * LANES]
                outs = []
                for half in range(2):
                    hidx = 2 * qg + half
                    qm = jnp.where(lo if half == 0 else jnp.logical_not(lo), qblk, jnp.zeros_like(qblk))
                    s = _dot_nt(qm, kwin) + bias_ref[hidx]
                    if i == 0:
                        s = jnp.where(before_start, NEG, s)
                    if i == nsub - 1:
                        s = jnp.where(after_end, NEG, s)
                    m = jnp.max(s, axis=-1, keepdims=True)
                    if has_sink:
                        sk = sink_ref[hidx]
                        m = jnp.maximum(m, sk)
                    p = jnp.exp(s - m)
                    den = jnp.sum(p, axis=-1, keepdims=True)
                    if has_sink:
                        den = den + jnp.exp(sk - m)
                    o = _dot(p.astype(BF16), vwin)
                    outs.append(o / den)
                    if want_lse:
                        lse_tile = jnp.where(lane8 == hidx, m + jnp.log(den), lse_tile)
                o_ref[i * W:(i + 1) * W, qg * LANES:(qg + 1) * LANES] = jnp.where(lo, outs[0], outs[1]).astype(BF16)
        if want_lse:
            lse_ref[i * W:(i + 1) * W, :] = lse_tile


def _band_bias(slopes, step, W):
    row = np.arange(W)[:, None]
    col = np.arange(3 * W)[None, :]
    dist = np.abs(row + W - col)
    bias = -np.asarray(slopes, np.float32)[:, None, None] * (step * dist).astype(np.float32)[None]
    return jnp.asarray(np.where(dist[None] <= W, bias, np.float32(NEG)).astype(np.float32))


def _alibi_slopes(n):
    return (2.0 ** (-8.0 * np.arange(1, n + 1, dtype=np.float32) / n)).astype(np.float32)


def _band_rows(Ls, W):
    R = min(Ls, 512)
    assert Ls % R == 0 and R % W == 0, (Ls, R, W)
    return R


def _banded_attention(q, k, v, slopes, *, bsz, L, r, W, n_kg, G, sink=None, want_lse=False):
    Ls = L // r
    assert Ls * r == L and Ls % W == 0
    R = _band_rows(Ls, W)
    nsub = R // W
    nchunks = Ls // R
    nblk = Ls // W
    Cq = n_kg * G * LANES
    Ck = n_kg * LANES
    qv = q.reshape(bsz, Ls, r * Cq)
    kv = k.reshape(bsz, Ls, r * Ck)
    vv = v.reshape(bsz, Ls, r * Ck)
    bias = _band_bias(slopes, r, W)
    q_spec = pl.BlockSpec((None, R, Cq), lambda b, j, c: (b, c, j))
    main = pl.BlockSpec((None, R, Ck), lambda b, j, c: (b, c, j))
    prev = pl.BlockSpec((None, W, Ck), lambda b, j, c: (b, jnp.maximum(c * nsub - 1, 0), j))
    nxt = pl.BlockSpec((None, W, Ck), lambda b, j, c: (b, jnp.minimum((c + 1) * nsub, nblk - 1), j))
    in_specs = [q_spec, prev, main, nxt, prev, main, nxt, _const_spec(bias.shape)]
    args = [qv, kv, kv, kv, vv, vv, vv, bias]
    if sink is not None:
        in_specs.append(pl.BlockSpec(memory_space=pltpu.SMEM))
        args.append(sink)
    out_specs = [q_spec]
    out_shape = [jax.ShapeDtypeStruct((bsz, Ls, r * Cq), BF16)]
    if want_lse:
        out_specs.append(pl.BlockSpec((None, None, R, LSE_LANES), lambda b, j, c: (b, j, c, 0)))
        out_shape.append(jax.ShapeDtypeStruct((bsz, r, Ls, LSE_LANES), F32))
    kern = functools.partial(_banded_kernel, W=W, R=R, n_kg=n_kg, G=G, has_sink=sink is not None,
                             want_lse=want_lse, nchunks=nchunks)
    outs = pl.pallas_call(
        kern,
        grid=(bsz, r, nchunks),
        in_specs=in_specs,
        out_specs=out_specs,
        out_shape=out_shape,
        scratch_shapes=[pltpu.VMEM((R + 2 * W, Ck), BF16), pltpu.VMEM((R + 2 * W, Ck), BF16)],
        compiler_params=_params("parallel", "parallel", "parallel"),
        name=f"banded_w{W}_r{r}",
    )(*args)
    o = outs[0].reshape(bsz * L, Cq)
    if want_lse:
        lse = jnp.transpose(outs[1], (0, 2, 1, 3)).reshape(bsz * L, LSE_LANES)
        return o, lse
    return o


def _flash_kernel(q_ref, k_ref, v_ref, o_ref, *, tk, nk):
    tq = q_ref.shape[1]
    outs = []
    for h in range(2):
        q = q_ref[h]

        def body(t, carry, h=h, q=q):
            m, acc = carry
            off = pl.multiple_of(t * tk, tk)
            kt = k_ref[h, pl.ds(off, tk), :]
            vt = v_ref[h, pl.ds(off, tk), :]
            s = _dot_nt(q, kt)
            m_new = jnp.maximum(m, jnp.max(s, axis=-1, keepdims=True))
            alpha = jnp.exp2(m - m_new)
            p = jnp.exp2(s - m_new)
            return m_new, acc * alpha + _dot(p.astype(BF16), vt)

        m0 = jnp.full((tq, 1), NEG, F32)
        acc0 = jnp.zeros((tq, LANES), F32)
        _, acc = lax.fori_loop(0, nk, body, (m0, acc0))
        den = acc[:, HALF:HALF + 1] if h == 0 else acc[:, 0:1]
        outs.append(acc / den)
    lo = lax.broadcasted_iota(jnp.int32, (tq, LANES), 1) < HALF
    o_ref[...] = jnp.where(lo, outs[0], outs[1]).astype(BF16)


def _flash_tiles(L):
    tq = min(L, 512)
    tk = min(L, 512)
    assert L % tq == 0 and L % tk == 0
    return tq, tk


def _latent_attention(qb, kb, vb, bsz, L):
    tq, tk = _flash_tiles(L)
    q4 = qb.reshape(B_HEADS, bsz, L, LANES)
    k4 = kb.reshape(B_HEADS, bsz, L, LANES)
    v4 = vb.reshape(B_HEADS, bsz, L, LANES)
    kv_spec = pl.BlockSpec((2, None, L, LANES), lambda b, hp, i: (hp, b, 0, 0))
    out = pl.pallas_call(
        functools.partial(_flash_kernel, tk=tk, nk=L // tk),
        grid=(bsz, B_HEADS // 2, L // tq),
        in_specs=[pl.BlockSpec((2, None, tq, LANES), lambda b, hp, i: (hp, b, i, 0)), kv_spec, kv_spec],
        out_specs=pl.BlockSpec((None, tq, LANES), lambda b, hp, i: (b, i, hp)),
        out_shape=jax.ShapeDtypeStruct((bsz, L, B_WIDTH), BF16),
        compiler_params=_params("parallel", "parallel", "parallel"),
        name="latent_flash",
    )(q4, k4, v4)
    return out.reshape(bsz * L, B_WIDTH)


def _mix_out_kernel(x_ref, ya_ref, yb_ref, c0_ref, c1_ref, c2_ref, l0_ref, l1_ref, l2_ref,
                    ga_ref, gb_ref, gc_ref, w_ref, o_ref):
    tm = x_ref.shape[0]
    l0, l1, l2 = l0_ref[...], l1_ref[...], l2_ref[...]
    mx = jnp.maximum(jnp.maximum(l0, l1), l2)
    e0, e1, e2 = jnp.exp(l0 - mx), jnp.exp(l1 - mx), jnp.exp(l2 - mx)
    tot = e0 + e1 + e2
    wts = (e0 / tot, e1 / tot, e2 / tot)
    lo = lax.broadcasted_iota(jnp.int32, (tm, LANES), 1) < HALF
    groups = []
    for kg in range(C_HEADS // 2):
        sl = slice(kg * LANES, (kg + 1) * LANES)
        acc = jnp.zeros((tm, LANES), F32)
        for cref, w in zip((c0_ref, c1_ref, c2_ref), wts):
            wfull = jnp.where(lo, w[:, 2 * kg:2 * kg + 1], w[:, 2 * kg + 1:2 * kg + 2])
            acc = acc + cref[:, sl].astype(F32) * wfull
        groups.append(acc)
    yc = jnp.concatenate(groups, axis=-1)
    y = jnp.concatenate([
        _rms(ya_ref[...].astype(F32), ga_ref[...]).astype(BF16),
        _rms(yb_ref[...].astype(F32), gb_ref[...]).astype(BF16),
        _rms(yc, gc_ref[...]).astype(BF16)], axis=-1)
    o_ref[...] = x_ref[...] + _dot(y, w_ref[...])


def _mix_out(x2, ya, yb, ycs, lses, lw, tm):
    T = x2.shape[0]
    row = lambda w: pl.BlockSpec((tm, w), lambda i: (i, 0))
    weights = (lw['g_out_a'], lw['g_out_b'], lw['g_out_c'], lw['w_out'])
    return pl.pallas_call(
        _mix_out_kernel,
        grid=(T // tm,),
        in_specs=([row(D_MODEL), row(A_WIDTH), row(B_WIDTH)] + [row(C_WIDTH)] * 3 + [row(LSE_LANES)] * 3
                  + [_const_spec(w.shape) for w in weights]),
        out_specs=row(D_MODEL),
        out_shape=jax.ShapeDtypeStruct((T, D_MODEL), F32),
        compiler_params=_params("parallel"),
        name="mix_out",
    )(x2, ya, yb, *ycs, *lses, *weights)


def _norm_matmul_kernel(x_ref, g_ref, w_ref, o_ref):
    o_ref[...] = _dot(_rms(x_ref[...], g_ref[...]).astype(BF16), w_ref[...]).astype(o_ref.dtype)


def _norm_matmul(x2, g, w, tm):
    T, K = x2.shape
    N = w.shape[1]
    return pl.pallas_call(
        _norm_matmul_kernel,
        grid=(T // tm,),
        in_specs=[pl.BlockSpec((tm, K), lambda i: (i, 0)), _const_spec(g.shape), _const_spec(w.shape)],
        out_specs=pl.BlockSpec((tm, N), lambda i: (i, 0)),
        out_shape=jax.ShapeDtypeStruct((T, N), BF16),
        compiler_params=_params("parallel"),
        name="norm_matmul",
    )(x2, g, w)


def _cross_kernel(x_ref, kv_ref, g_ref, wq_ref, wo_ref, o_ref):
    x = x_ref[...]
    h = _rms(x, g_ref[...]).astype(BF16)
    q = (_dot(h, wq_ref[...]) * (X_HEAD_DIM ** -0.5)).astype(BF16)
    xd = X_HEADS * X_HEAD_DIM
    outs = []
    for hd in range(X_HEADS):
        sl = slice(hd * X_HEAD_DIM, (hd + 1) * X_HEAD_DIM)
        k = kv_ref[:, sl]
        v = kv_ref[:, xd + hd * X_HEAD_DIM:xd + (hd + 1) * X_HEAD_DIM]
        s = _dot_nt(q[:, sl], k)
        p = jnp.exp(s - jnp.max(s, axis=-1, keepdims=True))
        den = jnp.sum(p, axis=-1, keepdims=True)
        outs.append((_dot(p.astype(BF16), v) / den).astype(BF16))
    o_ref[...] = x + _dot(jnp.concatenate(outs, axis=-1), wo_ref[...])


def _cross_attention(x2, kvm, lw, L, tm):
    T = x2.shape[0]
    nl = L // tm
    M = kvm.shape[1]
    xd = X_HEADS * X_HEAD_DIM
    weights = (lw['g_x'], lw['w_xq'], lw['w_xo'])
    return pl.pallas_call(
        _cross_kernel,
        grid=(T // tm,),
        in_specs=[pl.BlockSpec((tm, D_MODEL), lambda i: (i, 0)),
                  pl.BlockSpec((None, M, 2 * xd), lambda i: (i // nl, 0, 0))]
                 + [_const_spec(w.shape) for w in weights],
        out_specs=pl.BlockSpec((tm, D_MODEL), lambda i: (i, 0)),
        out_shape=jax.ShapeDtypeStruct((T, D_MODEL), F32),
        compiler_params=_params("parallel"),
        name="cross_attention",
    )(x2, kvm, *weights)


def _swiglu_kernel(te_ref, nv_ref, x_ref, g_ref, wg_ref, wu_ref, wd_ref, o_ref, h_scr, acc_scr, *, dense):
    i = pl.program_id(0)
    j = pl.program_id(1)
    nj = pl.num_programs(1)

    @pl.when(i < nv_ref[0])
    def _():
        @pl.when(j == 0)
        def _():
            if dense:
                h_scr[...] = _rms(x_ref[...], g_ref[...]).astype(BF16)
            else:
                h_scr[...] = x_ref[...]
            acc_scr[...] = jnp.zeros_like(acc_scr)

        h = h_scr[...]
        gate = _dot(h, wg_ref[...])
        up = _dot(h, wu_ref[...])
        act = (gate * jax.nn.sigmoid(gate) * up).astype(BF16)
        acc_scr[...] += _dot(act, wd_ref[...])

        @pl.when(j == nj - 1)
        def _():
            if dense:
                o_ref[...] = x_ref[...] + acc_scr[...]
            else:
                o_ref[...] = (g_ref[...] * acc_scr[...]).astype(o_ref.dtype)

    @pl.when(jnp.logical_and(i >= nv_ref[0], j == nj - 1))
    def _():
        o_ref[...] = jnp.zeros_like(o_ref)


def _ff_chunk(dff):
    for c in (1408, 1024, 512, 256, 128):
        if dff % c == 0:
            return c
    raise ValueError(dff)


def _swiglu(x2, g, wg, wu, wd, tile_expert, n_valid, tm, dense):
    T = x2.shape[0]
    dff = wg.shape[2]
    ck = _ff_chunk(dff)
    nj = dff // ck
    nt = T // tm

    def chunk(i, j, nv):
        return jnp.where(i < nv[0], j, nj - 1)

    grid_spec = pltpu.PrefetchScalarGridSpec(
        num_scalar_prefetch=2,
        grid=(nt, nj),
        in_specs=[pl.BlockSpec((tm, D_MODEL), lambda i, j, te, nv: (i, 0)),
                  (pl.BlockSpec(g.shape, lambda i, j, te, nv: (0, 0)) if dense
                   else pl.BlockSpec((tm, 1), lambda i, j, te, nv: (i, 0))),
                  pl.BlockSpec((None, D_MODEL, ck), lambda i, j, te, nv: (te[i], 0, chunk(i, j, nv))),
                  pl.BlockSpec((None, D_MODEL, ck), lambda i, j, te, nv: (te[i], 0, chunk(i, j, nv))),
                  pl.BlockSpec((None, ck, D_MODEL), lambda i, j, te, nv: (te[i], chunk(i, j, nv), 0))],
        out_specs=pl.BlockSpec((tm, D_MODEL), lambda i, j, te, nv: (i, 0)),
        scratch_shapes=[pltpu.VMEM((tm, D_MODEL), BF16), pltpu.VMEM((tm, D_MODEL), F32)],
    )
    return pl.pallas_call(
        functools.partial(_swiglu_kernel, dense=dense),
        grid_spec=grid_spec,
        out_shape=jax.ShapeDtypeStruct((T, D_MODEL), F32 if dense else BF16),
        compiler_params=_params("parallel", "arbitrary"),
        name="swiglu_dense" if dense else "swiglu_experts",
    )(tile_expert, n_valid, x2, g, wg, wu, wd)


def _router_kernel(x_ref, g_ref, wr_ref, h_ref, route_ref):
    h = _rms(x_ref[...], g_ref[...])
    h_ref[...] = h.astype(BF16)
    logits = jnp.dot(h, wr_ref[...], preferred_element_type=F32, precision=lax.Precision.HIGHEST)
    tm = logits.shape[0]
    lane = lax.broadcasted_iota(jnp.int32, (tm, LANES), 1)
    logits = jnp.where(lane < N_EXPERTS, logits, NEG)
    v1 = jnp.max(logits, axis=-1, keepdims=True)
    i1 = jnp.min(jnp.where(logits == v1, lane, LANES), axis=-1, keepdims=True)
    rest = jnp.where(lane == i1, NEG, logits)
    v2 = jnp.max(rest, axis=-1, keepdims=True)
    i2 = jnp.min(jnp.where(rest == v2, lane, LANES), axis=-1, keepdims=True)
    e2 = jnp.exp(v2 - v1)
    g1 = 1.0 / (1.0 + e2)
    g2 = e2 / (1.0 + e2)
    lane8 = lax.broadcasted_iota(jnp.int32, (tm, LSE_LANES), 1)
    route = jnp.where(lane8 == 0, i1.astype(F32), 0.0)
    route = jnp.where(lane8 == 1, i2.astype(F32), route)
    route = jnp.where(lane8 == 2, g1, route)
    route = jnp.where(lane8 == 3, g2, route)
    route_ref[...] = route


def _router(x2, g, wr, tm):
    T = x2.shape[0]
    return pl.pallas_call(
        _router_kernel,
        grid=(T // tm,),
        in_specs=[pl.BlockSpec((tm, D_MODEL), lambda i: (i, 0)), _const_spec(g.shape), _const_spec(wr.shape)],
        out_specs=[pl.BlockSpec((tm, D_MODEL), lambda i: (i, 0)), pl.BlockSpec((tm, LSE_LANES), lambda i: (i, 0))],
        out_shape=[jax.ShapeDtypeStruct((T, D_MODEL), BF16), jax.ShapeDtypeStruct((T, LSE_LANES), F32)],
        compiler_params=_params("parallel"),
        name="router",
    )(x2, g, wr)


ROW_SUBLANES = D_MODEL // LANES


def _row_copy(src, dst, sem):
    return pltpu.make_async_copy(src, dst, sem)


def _dispatch_kernel(slot_ref, h_ref, zero_ref, o_ref, sem, *, tm):
    del zero_ref

    def issue(r, _):
        for k in range(TOP_K):
            _row_copy(h_ref.at[r], o_ref.at[slot_ref[k, r]], sem).start()
        return 0

    lax.fori_loop(0, tm, issue, 0)

    def drain(r, _):
        for k in range(TOP_K):
            _row_copy(h_ref.at[r], o_ref.at[slot_ref[k, r]], sem).wait()
        return 0

    lax.fori_loop(0, tm, drain, 0)


def _dispatch(h2, slots, n_slots, tm):
    T = h2.shape[0]
    h3 = h2.reshape(T, ROW_SUBLANES, LANES)
    slots3 = slots.reshape(TOP_K, T // tm, tm).transpose(1, 0, 2)
    zeros = jnp.zeros((n_slots, ROW_SUBLANES, LANES), h2.dtype)
    out = pl.pallas_call(
        functools.partial(_dispatch_kernel, tm=tm),
        grid=(T // tm,),
        in_specs=[pl.BlockSpec((None, TOP_K, tm), lambda i: (i, 0, 0), memory_space=pltpu.SMEM),
                  pl.BlockSpec((tm, ROW_SUBLANES, LANES), lambda i: (i, 0, 0)),
                  pl.BlockSpec(memory_space=pl.ANY)],
        out_specs=pl.BlockSpec(memory_space=pl.ANY),
        out_shape=jax.ShapeDtypeStruct(zeros.shape, zeros.dtype),
        scratch_shapes=[pltpu.SemaphoreType.DMA],
        input_output_aliases={2: 0},
        compiler_params=pltpu.CompilerParams(dimension_semantics=("arbitrary",), vmem_limit_bytes=VMEM_LIMIT,
                                             has_side_effects=True),
        name="dispatch",
    )(slots3, h3, zeros)
    return out.reshape(n_slots, D_MODEL)


def _combine_kernel(slot_ref, x_ref, g_ref, y_ref, o_ref, buf, sem, *, tm, final_norm):
    def issue(r, _):
        for k in range(TOP_K):
            _row_copy(y_ref.at[slot_ref[k, r]], buf.at[k, r], sem).start()
        return 0

    lax.fori_loop(0, tm, issue, 0)

    def drain(r, _):
        for k in range(TOP_K):
            _row_copy(y_ref.at[slot_ref[k, r]], buf.at[k, r], sem).wait()
        return 0

    lax.fori_loop(0, tm, drain, 0)
    out = x_ref[...] + (buf[0].astype(F32) + buf[1].astype(F32))
    if final_norm:
        ss = jnp.sum(jnp.sum(out * out, axis=2, keepdims=True), axis=1, keepdims=True)
        out = out * lax.rsqrt(ss * (1.0 / D_MODEL) + EPS) * g_ref[...]
    o_ref[...] = out


def _combine(x2, y, slots, g_final, tm, final_norm):
    T = x2.shape[0]
    x3 = x2.reshape(T, ROW_SUBLANES, LANES)
    y3 = y.reshape(y.shape[0], ROW_SUBLANES, LANES)
    g3 = g_final.reshape(1, ROW_SUBLANES, LANES)
    slots3 = slots.reshape(TOP_K, T // tm, tm).transpose(1, 0, 2)
    blk = pl.BlockSpec((tm, ROW_SUBLANES, LANES), lambda i: (i, 0, 0))
    out = pl.pallas_call(
        functools.partial(_combine_kernel, tm=tm, final_norm=final_norm),
        grid=(T // tm,),
        in_specs=[pl.BlockSpec((None, TOP_K, tm), lambda i: (i, 0, 0), memory_space=pltpu.SMEM),
                  blk, _const_spec(g3.shape), pl.BlockSpec(memory_space=pl.ANY)],
        out_specs=blk,
        out_shape=jax.ShapeDtypeStruct(x3.shape, F32),
        scratch_shapes=[pltpu.VMEM((TOP_K, tm, ROW_SUBLANES, LANES), y.dtype), pltpu.SemaphoreType.DMA],
        compiler_params=_params("arbitrary"),
        name="combine",
    )(slots3, x3, g3, y3)
    return out.reshape(T, D_MODEL)


def _plan_slots(route, tm):
    T = route.shape[0]
    experts = route[:, :TOP_K].astype(jnp.int32)
    onehot = (experts[:, :, None] == jnp.arange(N_EXPERTS)[None, None, :]).astype(jnp.int32)
    flat = onehot.reshape(T * TOP_K, N_EXPERTS)
    csum = jnp.cumsum(flat, axis=0)
    rank = jnp.sum((csum - flat) * flat, axis=-1).reshape(T, TOP_K)
    counts = csum[-1]
    tiles = (counts + tm - 1) // tm
    tile_end = jnp.cumsum(tiles)
    start = (tile_end - tiles) * tm
    slots = jnp.sum(onehot * start[None, None, :], axis=-1) + rank
    n_tiles = (T * TOP_K) // tm + N_EXPERTS
    tile_ids = jnp.arange(n_tiles)
    tile_expert = jnp.minimum(jnp.sum(tile_ids[:, None] >= tile_end[None, :], axis=-1), N_EXPERTS - 1)
    n_valid = tile_end[-1:]
    last_valid = tile_expert[jnp.maximum(n_valid[0] - 1, 0)]
    tile_expert = jnp.where(tile_ids < n_valid[0], tile_expert, last_valid).astype(jnp.int32)
    return slots.T.astype(jnp.int32), tile_expert, n_valid.astype(jnp.int32), n_tiles


def _prep_layer(l, p):
    offs = np.cumsum((0,) + IN_SIZES)
    w_in = p['w_in'][l]
    qa, ka, va, cq, ckv, kr, qc, kc, vc = [w_in[:, int(offs[i]):int(offs[i + 1])] for i in range(9)]
    scale = HEAD_DIM ** -0.5
    order = list(A_HEAD_ORDER)
    qa = qa.reshape(D_MODEL, A_Q_HEADS, HEAD_DIM)[:, order, :].reshape(D_MODEL, A_WIDTH) * scale
    half = B_ROPE // 2
    z = lambda *s: jnp.zeros(s, F32)
    kr_pad = jnp.concatenate([z(D_MODEL, B_NOPE), kr, z(D_MODEL, LANES - B_NOPE - B_ROPE)], axis=1)
    kr_rot = jnp.concatenate([z(D_MODEL, B_NOPE), -kr[:, half:], kr[:, :half], z(D_MODEL, LANES - B_NOPE - B_ROPE)], axis=1)
    uq = p['mla_w_uq'][l].reshape(B_Q_LORA, B_HEADS, B_NOPE + B_ROPE)
    padq = z(B_Q_LORA, B_HEADS, LANES - B_NOPE - B_ROPE)
    uq_pad = jnp.concatenate([uq, padq], axis=-1).reshape(B_Q_LORA, B_HEADS * LANES)
    uq_rot = jnp.concatenate([z(B_Q_LORA, B_HEADS, B_NOPE), -uq[..., B_NOPE + half:], uq[..., B_NOPE:B_NOPE + half], padq],
                             axis=-1).reshape(B_Q_LORA, B_HEADS * LANES)
    ukv = p['mla_w_ukv'][l].reshape(B_KV_LORA, B_HEADS, B_NOPE + B_V)
    zk = z(B_KV_LORA, B_HEADS, HALF)
    uk_pad = jnp.concatenate([ukv[..., :B_NOPE], zk], axis=-1).reshape(B_KV_LORA, B_HEADS * LANES)
    uv = ukv[..., B_NOPE:]
    odd = (jnp.arange(B_HEADS) % 2 == 1)[None, :, None]
    uv_pad = jnp.where(odd, jnp.concatenate([zk, uv], axis=-1), jnp.concatenate([uv, zk], axis=-1))
    uv_pad = uv_pad.reshape(B_KV_LORA, B_HEADS * LANES)

    g_out = p['mix_out_norm_g'][l]
    w_out = p['w_out'][l]
    a_rows = np.concatenate([np.arange(HEAD_DIM) + HEAD_DIM * h for h in A_HEAD_ORDER])
    w_out = jnp.concatenate([w_out[:A_WIDTH][a_rows], w_out[A_WIDTH:]], axis=0)
    g_out_a = g_out[:A_WIDTH][a_rows]
    xd = X_HEADS * X_HEAD_DIM
    row = lambda v: v.reshape(1, -1).astype(F32)
    return {
        'g_mix': row(p['norm_mix_g'][l]),
        'wa': jnp.concatenate([qa, ka, va], axis=1).astype(BF16),
        'wc': jnp.concatenate([qc * scale, kc, vc], axis=1).astype(BF16),
        'wb': jnp.concatenate([cq, ckv], axis=1).astype(BF16),
        'wkr': jnp.concatenate([kr_pad, kr_rot], axis=1).astype(BF16),
        'g_q': row(p['mla_q_norm_g'][l]),
        'g_kv': row(p['mla_kv_norm_g'][l]),
        'wuq': jnp.concatenate([uq_pad, uq_rot], axis=1).astype(BF16),
        'wukv': jnp.concatenate([uk_pad, uv_pad], axis=1).astype(BF16),
        'sink': p['sink_a'][l][np.array(A_HEAD_ORDER)].astype(F32),
        'g_out_a': row(g_out_a),
        'g_out_b': row(g_out[A_WIDTH:A_WIDTH + B_WIDTH]),
        'g_out_c': row(g_out[A_WIDTH + B_WIDTH:]),
        'w_out': w_out.astype(BF16),
        'g_x': row(p['norm_x_g'][l]),
        'g_mem': row(p['norm_mem_g'][l]),
        'w_xq': p['w_xq'][l].astype(BF16),
        'w_xkv': p['w_xkv'][l].astype(BF16),
        'w_xo': p['w_xo'][l].astype(BF16),
        'g_ffn': row(p['norm_ffn_g'][l]),
    }


def _prep_ffn(w_gu, w_down):
    dff = w_down.shape[-2]
    return w_gu[..., :dff].astype(BF16), w_gu[..., dff:].astype(BF16), w_down.astype(BF16)


def _rope_tables(L):
    half = B_ROPE // 2
    pos = jnp.arange(L, dtype=F32)
    inv = ROPE_THETA ** (-jnp.arange(half, dtype=F32) / half)
    ang = pos[:, None] * inv[None, :]
    cos2 = jnp.tile(jnp.cos(ang), (1, 2))
    sin2 = jnp.tile(jnp.sin(ang), (1, 2))
    pad = jnp.zeros((L, LANES - B_NOPE - B_ROPE), F32)
    qscale = (B_NOPE + B_ROPE) ** -0.5 * LOG2E
    cq = jnp.concatenate([jnp.ones((L, B_NOPE), F32), cos2, pad], axis=1) * qscale
    sq = jnp.concatenate([jnp.zeros((L, B_NOPE), F32), sin2, pad], axis=1) * qscale
    ck = jnp.concatenate([jnp.zeros((L, B_NOPE), F32), cos2, pad], axis=1)
    sk = jnp.concatenate([jnp.zeros((L, B_NOPE), F32), sin2, pad], axis=1)
    return cq, sq, ck, sk


TOKEN_TILE = 512
EXPERT_TILE = 1024
FFN_TILE = 1024
ROUTE_TILE = 256


def _trunk(x, mem, layers, ffn, moe, moe_router, final_g):
    bsz, L, _ = x.shape
    T = bsz * L
    depth = len(layers)
    tm = min(TOKEN_TILE, L)
    x2 = x.reshape(T, D_MODEL)
    mem2 = mem.reshape(-1, D_MODEL)
    tabs = _rope_tables(L)
    slopes_a = _alibi_slopes(A_Q_HEADS)[list(A_HEAD_ORDER)]
    slopes_c = _alibi_slopes(C_HEADS)
    normed = False
    for l, lw in enumerate(layers):
        qa, ka, va, qc, kc, vc, qb, kb, vb = _in_proj(x2, lw, tabs, L, tm)
        ya = _banded_attention(qa, ka, va, slopes_a, bsz=bsz, L=L, r=1, W=A_WINDOW, n_kg=1, G=2, sink=lw['sink'])
        yb = _latent_attention(qb, kb, vb, bsz, L)
        ycs, lses = [], []
        for (w, r) in DILATED_PAIRS:
            o, lse = _banded_attention(qc, kc, vc, slopes_c, bsz=bsz, L=L, r=r, W=w // (2 * r),
                                       n_kg=C_HEADS // 2, G=1, want_lse=True)
            ycs.append(o)
            lses.append(lse)
        x2 = _mix_out(x2, ya, yb, ycs, lses, lw, tm)

        kvm = _norm_matmul(mem2, lw['g_mem'], lw['w_xkv'], min(512, mem2.shape[0]))
        x2 = _cross_attention(x2, kvm.reshape(bsz, mem.shape[1], -1), lw, L, tm)

        if l % 2 == 0:
            wg, wu, wd = ffn[l // 2]
            ft = min(FFN_TILE, T)
            nt = T // ft
            x2 = _swiglu(x2, lw['g_ffn'], wg, wu, wd, jnp.zeros((nt,), jnp.int32),
                         jnp.full((1,), nt, jnp.int32), ft, dense=True)
        else:
            wg, wu, wd = moe[l // 2]
            rt = min(ROUTE_TILE, T)
            et = min(EXPERT_TILE, T)
            hn, route = _router(x2, lw['g_ffn'], moe_router[l // 2], tm)
            slots, tile_expert, n_valid, n_tiles = _plan_slots(route, et)
            hs = _dispatch(hn, slots, n_tiles * et, rt)
            gates = jnp.zeros((n_tiles * et,), F32).at[slots.reshape(-1)].set(route[:, TOP_K:2 * TOP_K].T.reshape(-1))
            y = _swiglu(hs, gates.reshape(-1, 1), wg, wu, wd, tile_expert, n_valid, et, dense=False)
            last = l == depth - 1
            x2 = _combine(x2, y, slots, final_g, rt, final_norm=last)
            normed = last
    assert normed, "the final rmsnorm is fused into the routed layer's combine step"
    return x2.reshape(bsz, L, D_MODEL)


def kernel(x_prompt, x_sample, mem_prompt, mem_sample, norm_mix_g, w_in, sink_a, mla_q_norm_g, mla_kv_norm_g, mla_w_uq, mla_w_ukv, mix_out_norm_g, w_out, norm_x_g, norm_mem_g, w_xq, w_xkv, w_xo, norm_ffn_g, ffn_w_gu, ffn_w_down, moe_router, moe_w_gu, moe_w_down, final_norm_g):
    p = dict(norm_mix_g=norm_mix_g, w_in=w_in, sink_a=sink_a, mla_q_norm_g=mla_q_norm_g, mla_kv_norm_g=mla_kv_norm_g,
             mla_w_uq=mla_w_uq, mla_w_ukv=mla_w_ukv, mix_out_norm_g=mix_out_norm_g, w_out=w_out, norm_x_g=norm_x_g,
             norm_mem_g=norm_mem_g, w_xq=w_xq, w_xkv=w_xkv, w_xo=w_xo, norm_ffn_g=norm_ffn_g)
    depth = w_in.shape[0]
    layers = [_prep_layer(l, p) for l in range(depth)]
    ffn = [_prep_ffn(ffn_w_gu[i][None], ffn_w_down[i][None]) for i in range(ffn_w_gu.shape[0])]
    moe = [_prep_ffn(moe_w_gu[i], moe_w_down[i]) for i in range(moe_w_gu.shape[0])]
    routers = [jnp.pad(moe_router[i].astype(F32), ((0, 0), (0, LANES - N_EXPERTS))) for i in range(moe_router.shape[0])]
    final_g = final_norm_g.reshape(1, -1).astype(F32)
    y_prompt = _trunk(x_prompt, mem_prompt, layers, ffn, moe, routers, final_g)
    y_sample = _trunk(x_sample, mem_sample, layers, ffn, moe, routers, final_g)
    return (y_prompt, y_sample)
```

```python
import functools
import math

import numpy as np
import jax
import jax.numpy as jnp
from jax import lax
from jax.experimental import pallas as pl
from jax.experimental.pallas import tpu as pltpu

D_MODEL = 1024
HEAD_DIM = 64
A_Q_HEADS = 4
A_KV_HEADS = 2
A_WINDOW = 128
B_HEADS = 6
B_NOPE = 64
B_ROPE = 32
B_V = 64
B_Q_LORA = 384
B_KV_LORA = 256
ROPE_THETA = 10000.0
C_HEADS = 6
DILATED_PAIRS = ((128, 1), (512, 4), (2048, 16))
STRIDES = tuple(r for _, r in DILATED_PAIRS)
A_WIDTH = A_Q_HEADS * HEAD_DIM
B_WIDTH = B_HEADS * B_V
C_WIDTH = C_HEADS * HEAD_DIM
IN_SIZES = (A_WIDTH, A_KV_HEADS * HEAD_DIM, A_KV_HEADS * HEAD_DIM, B_Q_LORA, B_KV_LORA, B_ROPE,
            C_WIDTH, C_WIDTH, C_WIDTH)
X_HEADS = 4
X_HEAD_DIM = 128
N_EXPERTS = 8
TOP_K = 2
EPS = 1e-6
NEG = -1e30

LANES = 128
HALF = LANES // 2
LSE_LANES = 8
VMEM_LIMIT = 56 * 1024 * 1024
LOG2E = math.log2(math.e)

BF16 = jnp.bfloat16
F32 = jnp.float32

A_HEAD_ORDER = (0, 2, 1, 3)


def _params(*sem):
    return pltpu.CompilerParams(dimension_semantics=sem, vmem_limit_bytes=VMEM_LIMIT)


def _rms(x, g):
    return x * lax.rsqrt(jnp.mean(x * x, axis=-1, keepdims=True) + EPS) * g


def _dot(a, b):
    return jnp.dot(a, b, preferred_element_type=F32)


def _dot_nt(a, b):
    return lax.dot_general(a, b, (((1,), (1,)), ((), ())), preferred_element_type=F32)


def _const_spec(shape):
    n = len(shape)
    return pl.BlockSpec(shape, lambda *_: (0,) * n)


def _phase_spec(r, tm, nl, width):
    return pl.BlockSpec((None, r, tm // r, width), lambda i: (i // nl, 0, i % nl, 0))


def _in_kernel(*refs):
    it = iter(refs)
    x_ref, g_ref, wa_ref, wc_ref, wb_ref, wkr_ref, gq_ref, gkv_ref, wuq_ref, wukv_ref = [next(it) for _ in range(10)]
    cq_ref, sq_ref, ck_ref, sk_ref = [next(it) for _ in range(4)]
    qa_ref, ka_ref, va_ref = next(it), next(it), next(it)
    c_refs = [(next(it), next(it), next(it)) for _ in STRIDES]
    qb_ref, kb_ref, vb_ref = next(it), next(it), next(it)
    zc_scr = next(it)
    tm = x_ref.shape[0]

    h = _rms(x_ref[...], g_ref[...]).astype(BF16)
    za = _dot(h, wa_ref[...])
    qa_ref[...] = za[:, :A_WIDTH].astype(BF16)
    ka_ref[...] = za[:, A_WIDTH:A_WIDTH + LANES].astype(BF16)
    va_ref[...] = za[:, A_WIDTH + LANES:].astype(BF16)

    zc = _dot(h, wc_ref[...])
    ngrp = C_WIDTH // LANES
    for g in range(3 * ngrp):
        zc_scr[g] = zc[:, g * LANES:(g + 1) * LANES]
    for r, qkv_refs in zip(STRIDES, c_refs):
        for j in range(r):
            for n, ref in enumerate(qkv_refs):
                ref[j] = jnp.concatenate(
                    [zc_scr[n * ngrp + g, pl.ds(j, tm // r, stride=r), :] for g in range(ngrp)], axis=-1).astype(BF16)

    zb = _dot(h, wb_ref[...])
    hq = _rms(zb[:, :B_Q_LORA], gq_ref[...]).astype(BF16)
    hkv = _rms(zb[:, B_Q_LORA:], gkv_ref[...]).astype(BF16)
    zq = _dot(hq, wuq_ref[...])
    zkv = _dot(hkv, wukv_ref[...])
    zkr = _dot(h, wkr_ref[...])
    kr = zkr[:, :LANES] * ck_ref[...] + zkr[:, LANES:] * sk_ref[...]
    cq = cq_ref[...]
    sq = sq_ref[...]
    lane = lax.broadcasted_iota(jnp.int32, (1, LANES), 1)
    nb = B_HEADS * LANES
    for hd in range(B_HEADS):
        lo, hi = hd * LANES, (hd + 1) * LANES
        qb_ref[hd] = (zq[:, lo:hi] * cq + zq[:, nb + lo:nb + hi] * sq).astype(BF16)
        kb_ref[hd] = (zkv[:, lo:hi] + kr).astype(BF16)
        ones = (lane == (HALF if hd % 2 == 0 else 0)).astype(F32)
        vb_ref[hd] = (zkv[:, nb + lo:nb + hi] + ones).astype(BF16)


def _in_proj(x2, lw, tabs, bsz, L, tm):
    T = x2.shape[0]
    nl = L // tm
    row = lambda w: pl.BlockSpec((tm, w), lambda i: (i, 0))
    tab = pl.BlockSpec((tm, LANES), lambda i: (i % nl, 0))
    hm = pl.BlockSpec((B_HEADS, tm, LANES), lambda i: (0, i, 0))
    weights = (lw['g_mix'], lw['wa'], lw['wc'], lw['wb'], lw['wkr'], lw['g_q'], lw['g_kv'], lw['wuq'], lw['wukv'])
    out_specs = [row(A_WIDTH), row(LANES), row(LANES)]
    out_shape = [jax.ShapeDtypeStruct((T, w), BF16) for w in (A_WIDTH, LANES, LANES)]
    for r in STRIDES:
        assert tm % (16 * r) == 0, (tm, r)
        out_specs += [_phase_spec(r, tm, nl, C_WIDTH)] * 3
        out_shape += [jax.ShapeDtypeStruct((bsz, r, L // r, C_WIDTH), BF16)] * 3
    out_specs += [hm, hm, hm]
    out_shape += [jax.ShapeDtypeStruct((B_HEADS, T, LANES), BF16)] * 3
    outs = pl.pallas_call(
        _in_kernel,
        grid=(T // tm,),
        in_specs=[row(D_MODEL)] + [_const_spec(w.shape) for w in weights] + [tab] * 4,
        out_specs=out_specs,
        out_shape=out_shape,
        scratch_shapes=[pltpu.VMEM((3 * C_WIDTH // LANES, tm, LANES), F32)],
        compiler_params=_params("parallel"),
        name="in_proj",
    )(x2, *weights, *tabs)
    qkv_c = [outs[3 + 3 * n:6 + 3 * n] for n in range(len(STRIDES))]
    return outs[:3], qkv_c, outs[-3:]


def _banded_kernel(*refs, W, Q, R, n_kg, G, has_sink, want_lse, nchunks):
    it = iter(refs)
    q_ref = next(it)
    kp_ref, km_ref, kn_ref = next(it), next(it), next(it)
    vp_ref, vm_ref, vn_ref = next(it), next(it), next(it)
    bias_ref = next(it)
    sink_ref = next(it) if has_sink else None
    o_ref = next(it)
    lse_ref = next(it) if want_lse else None
    kfull, vfull = next(it), next(it)

    c = pl.program_id(2)
    kfull[0:W] = kp_ref[...]
    kfull[W:W + R] = km_ref[...]
    kfull[W + R:] = kn_ref[...]
    vfull[0:W] = vp_ref[...]
    vfull[W:W + R] = vm_ref[...]
    vfull[W + R:] = vn_ref[...]

    nsub = R // Q
    win = Q + 2 * W
    lo = lax.broadcasted_iota(jnp.int32, (Q, LANES), 1) < HALF
    col = lax.broadcasted_iota(jnp.int32, (2 * Q, win), 1)
    top = lax.broadcasted_iota(jnp.int32, (2 * Q, 1), 0) < Q
    lane8 = lax.broadcasted_iota(jnp.int32, (Q, LSE_LANES), 1)
    before_start = jnp.logical_and(c == 0, col < W)
    after_end = jnp.logical_and(c == nchunks - 1, col >= Q + W)
    units = [(i, kg, g) for i in range(nsub) for kg in range(n_kg) for g in range(G)]

    scores = []
    for i, kg, g in units:
        qg = kg * G + g
        qblk = q_ref[i * Q:(i + 1) * Q, qg * LANES:(qg + 1) * LANES]
        zero = jnp.zeros_like(qblk)
        qm = jnp.concatenate([jnp.where(lo, qblk, zero), jnp.where(lo, zero, qblk)], axis=0)
        kwin = kfull[i * Q:i * Q + win, kg * LANES:(kg + 1) * LANES]
        s = _dot_nt(qm, kwin) + bias_ref[qg]
        if i == 0:
            s = jnp.where(before_start, NEG, s)
        if i == nsub - 1:
            s = jnp.where(after_end, NEG, s)
        scores.append(s)

    probs = []
    for (i, kg, g), s in zip(units, scores):
        qg = kg * G + g
        m = jnp.max(s, axis=-1, keepdims=True)
        if has_sink:
            sk = jnp.where(top, sink_ref[2 * qg], sink_ref[2 * qg + 1])
            m = jnp.maximum(m, sk)
        p = jnp.exp(s - m)
        den = jnp.sum(p, axis=-1, keepdims=True)
        if has_sink:
            den = den + jnp.exp(sk - m)
        probs.append((p.astype(BF16), den, m))

    lse_tiles = [jnp.zeros((Q, LSE_LANES), F32) for _ in range(nsub)]
    for (i, kg, g), (p, den, m) in zip(units, probs):
        qg = kg * G + g
        vwin = vfull[i * Q:i * Q + win, kg * LANES:(kg + 1) * LANES]
        o = _dot(p, vwin) / den
        o_ref[i * Q:(i + 1) * Q, qg * LANES:(qg + 1) * LANES] = jnp.where(lo, o[:Q], o[Q:]).astype(BF16)
        if want_lse:
            lse = m + jnp.log(den)
            lse_tiles[i] = jnp.where(lane8 == 2 * qg, lse[:Q], lse_tiles[i])
            lse_tiles[i] = jnp.where(lane8 == 2 * qg + 1, lse[Q:], lse_tiles[i])
    if want_lse:
        for i in range(nsub):
            lse_ref[i * Q:(i + 1) * Q, :] = lse_tiles[i]


def _band_bias(slopes, step, W, Q):
    row = np.arange(Q)[:, None]
    col = np.arange(Q + 2 * W)[None, :]
    dist = np.abs(row + W - col)
    bias = -np.asarray(slopes, np.float32)[:, None, None] * (step * dist).astype(np.float32)[None]
    bias = np.where(dist[None] <= W, bias, np.float32(NEG)).astype(np.float32)
    return jnp.asarray(bias.reshape(len(slopes) // 2, 2 * Q, Q + 2 * W))


def _alibi_slopes(n):
    return (2.0 ** (-8.0 * np.arange(1, n + 1, dtype=np.float32) / n)).astype(np.float32)


def _band_tiles(Ls, W):
    R = min(Ls, 512)
    Q = min(R, 128)
    assert Ls % R == 0 and R % Q == 0 and R % W == 0 and Ls % W == 0, (Ls, R, Q, W)
    return R, Q


def _banded_attention(q, k, v, slopes, *, W, n_kg, G, sink=None, want_lse=False):
    bsz, r, Ls, Cq = q.shape
    Ck = k.shape[-1]
    R, Q = _band_tiles(Ls, W)
    nchunks = Ls // R
    nblk = Ls // W
    per = R // W
    bias = _band_bias(slopes, r, W, Q)
    q_spec = pl.BlockSpec((None, None, R, Cq), lambda b, j, c: (b, j, c, 0))
    main = pl.BlockSpec((None, None, R, Ck), lambda b, j, c: (b, j, c, 0))
    prev = pl.BlockSpec((None, None, W, Ck), lambda b, j, c: (b, j, jnp.maximum(c * per - 1, 0), 0))
    nxt = pl.BlockSpec((None, None, W, Ck), lambda b, j, c: (b, j, jnp.minimum((c + 1) * per, nblk - 1), 0))
    in_specs = [q_spec, prev, main, nxt, prev, main, nxt, _const_spec(bias.shape)]
    args = [q, k, k, k, v, v, v, bias]
    if sink is not None:
        in_specs.append(pl.BlockSpec(memory_space=pltpu.SMEM))
        args.append(sink)
    out_specs = [q_spec]
    out_shape = [jax.ShapeDtypeStruct(q.shape, BF16)]
    if want_lse:
        out_specs.append(pl.BlockSpec((None, None, R, LSE_LANES), lambda b, j, c: (b, j, c, 0)))
        out_shape.append(jax.ShapeDtypeStruct((bsz, r, Ls, LSE_LANES), F32))
    kern = functools.partial(_banded_kernel, W=W, Q=Q, R=R, n_kg=n_kg, G=G, has_sink=sink is not None,
                             want_lse=want_lse, nchunks=nchunks)
    outs = pl.pallas_call(
        kern,
        grid=(bsz, r, nchunks),
        in_specs=in_specs,
        out_specs=out_specs,
        out_shape=out_shape,
        scratch_shapes=[pltpu.VMEM((R + 2 * W, Ck), BF16), pltpu.VMEM((R + 2 * W, Ck), BF16)],
        compiler_params=_params("parallel", "parallel", "parallel"),
        name=f"banded_w{W}_r{r}",
    )(*args)
    return (outs[0], outs[1]) if want_lse else outs[0]


def _flash_kernel(q_ref, k_ref, v_ref, o_ref, *, tk, nk):
    tq = q_ref.shape[1]
    qs = [q_ref[0], q_ref[1]]

    def body(t, carry):
        off = pl.multiple_of(t * tk, tk)
        ss = [_dot_nt(qs[h], k_ref[h, pl.ds(off, tk), :]) for h in range(2)]
        ms = [jnp.maximum(carry[2 * h], jnp.max(ss[h], axis=-1, keepdims=True)) for h in range(2)]
        ps = [jnp.exp2(ss[h] - ms[h]).astype(BF16) for h in range(2)]
        out = []
        for h in range(2):
            alpha = jnp.exp2(carry[2 * h] - ms[h])
            out += [ms[h], carry[2 * h + 1] * alpha + _dot(ps[h], v_ref[h, pl.ds(off, tk), :])]
        return tuple(out)

    m0 = jnp.full((tq, 1), NEG, F32)
    acc0 = jnp.zeros((tq, LANES), F32)
    res = lax.fori_loop(0, nk, body, (m0, acc0, m0, acc0), unroll=4)
    acc_even, acc_odd = res[1], res[3]
    out_even = acc_even / acc_even[:, HALF:HALF + 1]
    out_odd = acc_odd / acc_odd[:, 0:1]
    lo = lax.broadcasted_iota(jnp.int32, (tq, LANES), 1) < HALF
    o_ref[...] = jnp.where(lo, out_even, out_odd).astype(BF16)


def _flash_tiles(L):
    tq = min(L, 512)
    tk = min(L, 512)
    assert L % tq == 0 and L % tk == 0
    return tq, tk


def _latent_attention(qb, kb, vb, bsz, L):
    tq, tk = _flash_tiles(L)
    q4 = qb.reshape(B_HEADS, bsz, L, LANES)
    k4 = kb.reshape(B_HEADS, bsz, L, LANES)
    v4 = vb.reshape(B_HEADS, bsz, L, LANES)
    kv_spec = pl.BlockSpec((2, None, L, LANES), lambda b, hp, i: (hp, b, 0, 0))
    out = pl.pallas_call(
        functools.partial(_flash_kernel, tk=tk, nk=L // tk),
        grid=(bsz, B_HEADS // 2, L // tq),
        in_specs=[pl.BlockSpec((2, None, tq, LANES), lambda b, hp, i: (hp, b, i, 0)), kv_spec, kv_spec],
        out_specs=pl.BlockSpec((None, tq, LANES), lambda b, hp, i: (b, i, hp)),
        out_shape=jax.ShapeDtypeStruct((bsz, L, B_WIDTH), BF16),
        compiler_params=_params("parallel", "parallel", "parallel"),
        name="latent_flash",
    )(q4, k4, v4)
    return out.reshape(bsz * L, B_WIDTH)


def _mix_out_kernel(*refs):
    it = iter(refs)
    x_ref, ya_ref, yb_ref = next(it), next(it), next(it)
    c_refs = [next(it) for _ in STRIDES]
    l_refs = [next(it) for _ in STRIDES]
    ga_ref, gb_ref, gc_ref, w_ref = next(it), next(it), next(it), next(it)
    o_ref = next(it)
    c_scr = next(it)
    tm = x_ref.shape[0]
    ngrp = C_WIDTH // LANES

    for n, r in enumerate(STRIDES):
        for j in range(r):
            blk = c_refs[n][j].astype(F32)
            for g in range(ngrp):
                c_scr[n * ngrp + g, pl.ds(j, tm // r, stride=r), :] = blk[:, g * LANES:(g + 1) * LANES]

    ls = [l_ref[...] for l_ref in l_refs]
    mx = functools.reduce(jnp.maximum, ls)
    es = [jnp.exp(l - mx) for l in ls]
    tot = functools.reduce(lambda a, b: a + b, es)
    wts = [e / tot for e in es]
    lo = lax.broadcasted_iota(jnp.int32, (tm, LANES), 1) < HALF
    groups = []
    for kg in range(ngrp):
        acc = jnp.zeros((tm, LANES), F32)
        for n, w in enumerate(wts):
            wfull = jnp.where(lo, w[:, 2 * kg:2 * kg + 1], w[:, 2 * kg + 1:2 * kg + 2])
            acc = acc + c_scr[n * ngrp + kg] * wfull
        groups.append(acc)
    yc = jnp.concatenate(groups, axis=-1)
    y = jnp.concatenate([
        _rms(ya_ref[...].astype(F32), ga_ref[...]).astype(BF16),
        _rms(yb_ref[...].astype(F32), gb_ref[...]).astype(BF16),
        _rms(yc, gc_ref[...]).astype(BF16)], axis=-1)
    o_ref[...] = x_ref[...] + _dot(y, w_ref[...])


def _mix_out(x2, ya, yb, ycs, lses, lw, L, tm):
    T = x2.shape[0]
    nl = L // tm
    row = lambda w: pl.BlockSpec((tm, w), lambda i: (i, 0))
    weights = (lw['g_out_a'], lw['g_out_b'], lw['g_out_c'], lw['w_out'])
    n = len(STRIDES)
    return pl.pallas_call(
        _mix_out_kernel,
        grid=(T // tm,),
        in_specs=([row(D_MODEL), row(A_WIDTH), row(B_WIDTH)]
                  + [_phase_spec(r, tm, nl, C_WIDTH) for r in STRIDES]
                  + [row(LSE_LANES)] * n
                  + [_const_spec(w.shape) for w in weights]),
        out_specs=row(D_MODEL),
        out_shape=jax.ShapeDtypeStruct((T, D_MODEL), F32),
        scratch_shapes=[pltpu.VMEM((n * C_WIDTH // LANES, tm, LANES), F32)],
        compiler_params=_params("parallel"),
        name="mix_out",
    )(x2, ya, yb, *ycs, *lses, *weights)


def _norm_matmul_kernel(x_ref, g_ref, w_ref, o_ref):
    o_ref[...] = _dot(_rms(x_ref[...], g_ref[...]).astype(BF16), w_ref[...]).astype(o_ref.dtype)


def _norm_matmul(x2, g, w, tm):
    T, K = x2.shape
    N = w.shape[1]
    return pl.pallas_call(
        _norm_matmul_kernel,
        grid=(T // tm,),
        in_specs=[pl.BlockSpec((tm, K), lambda i: (i, 0)), _const_spec(g.shape), _const_spec(w.shape)],
        out_specs=pl.BlockSpec((tm, N), lambda i: (i, 0)),
        out_shape=jax.ShapeDtypeStruct((T, N), BF16),
        compiler_params=_params("parallel"),
        name="norm_matmul",
    )(x2, g, w)


def _cross_kernel(x_ref, kv_ref, g_ref, wq_ref, wo_ref, o_ref):
    x = x_ref[...]
    h = _rms(x, g_ref[...]).astype(BF16)
    q = (_dot(h, wq_ref[...]) * (X_HEAD_DIM ** -0.5)).astype(BF16)
    xd = X_HEADS * X_HEAD_DIM
    outs = []
    for hd in range(X_HEADS):
        sl = slice(hd * X_HEAD_DIM, (hd + 1) * X_HEAD_DIM)
        k = kv_ref[:, sl]
        v = kv_ref[:, xd + hd * X_HEAD_DIM:xd + (hd + 1) * X_HEAD_DIM]
        s = _dot_nt(q[:, sl], k)
        p = jnp.exp(s - jnp.max(s, axis=-1, keepdims=True))
        den = jnp.sum(p, axis=-1, keepdims=True)
        outs.append((_dot(p.astype(BF16), v) / den).astype(BF16))
    o_ref[...] = x + _dot(jnp.concatenate(outs, axis=-1), wo_ref[...])


def _cross_attention(x2, kvm, lw, L, tm):
    T = x2.shape[0]
    nl = L // tm
    M = kvm.shape[1]
    xd = X_HEADS * X_HEAD_DIM
    weights = (lw['g_x'], lw['w_xq'], lw['w_xo'])
    return pl.pallas_call(
        _cross_kernel,
        grid=(T // tm,),
        in_specs=[pl.BlockSpec((tm, D_MODEL), lambda i: (i, 0)),
                  pl.BlockSpec((None, M, 2 * xd), lambda i: (i // nl, 0, 0))]
                 + [_const_spec(w.shape) for w in weights],
        out_specs=pl.BlockSpec((tm, D_MODEL), lambda i: (i, 0)),
        out_shape=jax.ShapeDtypeStruct((T, D_MODEL), F32),
        compiler_params=_params("parallel"),
        name="cross_attention",
    )(x2, kvm, *weights)


def _swiglu_kernel(te_ref, nv_ref, x_ref, g_ref, wg_ref, wu_ref, wd_ref, o_ref, h_scr, acc_scr, *, dense):
    i = pl.program_id(0)
    j = pl.program_id(1)
    nj = pl.num_programs(1)

    @pl.when(i < nv_ref[0])
    def _():
        @pl.when(j == 0)
        def _():
            if dense:
                h_scr[...] = _rms(x_ref[...], g_ref[...]).astype(BF16)
            else:
                h_scr[...] = x_ref[...]
            acc_scr[...] = jnp.zeros_like(acc_scr)

        h = h_scr[...]
        gate = _dot(h, wg_ref[...])
        up = _dot(h, wu_ref[...])
        act = (gate * jax.nn.sigmoid(gate) * up).astype(BF16)
        acc_scr[...] += _dot(act, wd_ref[...])

        @pl.when(j == nj - 1)
        def _():
            if dense:
                o_ref[...] = x_ref[...] + acc_scr[...]
            else:
                o_ref[...] = (g_ref[...] * acc_scr[...]).astype(o_ref.dtype)

    @pl.when(jnp.logical_and(i >= nv_ref[0], j == nj - 1))
    def _():
        o_ref[...] = jnp.zeros_like(o_ref)


def _ff_chunk(dff):
    for c in (1408, 1024, 512, 256, 128):
        if dff % c == 0:
            return c
    raise ValueError(dff)


def _swiglu(x2, g, wg, wu, wd, tile_expert, n_valid, tm, dense):
    T = x2.shape[0]
    dff = wg.shape[2]
    ck = _ff_chunk(dff)
    nj = dff // ck
    nt = T // tm

    def chunk(i, j, nv):
        return jnp.where(i < nv[0], j, nj - 1)

    grid_spec = pltpu.PrefetchScalarGridSpec(
        num_scalar_prefetch=2,
        grid=(nt, nj),
        in_specs=[pl.BlockSpec((tm, D_MODEL), lambda i, j, te, nv: (i, 0)),
                  (pl.BlockSpec(g.shape, lambda i, j, te, nv: (0, 0)) if dense
                   else pl.BlockSpec((tm, 1), lambda i, j, te, nv: (i, 0))),
                  pl.BlockSpec((None, D_MODEL, ck), lambda i, j, te, nv: (te[i], 0, chunk(i, j, nv))),
                  pl.BlockSpec((None, D_MODEL, ck), lambda i, j, te, nv: (te[i], 0, chunk(i, j, nv))),
                  pl.BlockSpec((None, ck, D_MODEL), lambda i, j, te, nv: (te[i], chunk(i, j, nv), 0))],
        out_specs=pl.BlockSpec((tm, D_MODEL), lambda i, j, te, nv: (i, 0)),
        scratch_shapes=[pltpu.VMEM((tm, D_MODEL), BF16), pltpu.VMEM((tm, D_MODEL), F32)],
    )
    return pl.pallas_call(
        functools.partial(_swiglu_kernel, dense=dense),
        grid_spec=grid_spec,
        out_shape=jax.ShapeDtypeStruct((T, D_MODEL), F32 if dense else BF16),
        compiler_params=_params("parallel", "arbitrary"),
        name="swiglu_dense" if dense else "swiglu_experts",
    )(tile_expert, n_valid, x2, g, wg, wu, wd)


def _router_kernel(x_ref, g_ref, wr_ref, h_ref, route_ref):
    h = _rms(x_ref[...], g_ref[...])
    h_ref[...] = h.astype(BF16)
    logits = jnp.dot(h, wr_ref[...], preferred_element_type=F32, precision=lax.Precision.HIGHEST)
    tm = logits.shape[0]
    lane = lax.broadcasted_iota(jnp.int32, (tm, LANES), 1)
    logits = jnp.where(lane < N_EXPERTS, logits, NEG)
    v1 = jnp.max(logits, axis=-1, keepdims=True)
    i1 = jnp.min(jnp.where(logits == v1, lane, LANES), axis=-1, keepdims=True)
    rest = jnp.where(lane == i1, NEG, logits)
    v2 = jnp.max(rest, axis=-1, keepdims=True)
    i2 = jnp.min(jnp.where(rest == v2, lane, LANES), axis=-1, keepdims=True)
    e2 = jnp.exp(v2 - v1)
    g1 = 1.0 / (1.0 + e2)
    g2 = e2 / (1.0 + e2)
    lane8 = lax.broadcasted_iota(jnp.int32, (tm, LSE_LANES), 1)
    route = jnp.where(lane8 == 0, i1.astype(F32), 0.0)
    route = jnp.where(lane8 == 1, i2.astype(F32), route)
    route = jnp.where(lane8 == 2, g1, route)
    route = jnp.where(lane8 == 3, g2, route)
    route_ref[...] = route


def _router(x2, g, wr, tm):
    T = x2.shape[0]
    return pl.pallas_call(
        _router_kernel,
        grid=(T // tm,),
        in_specs=[pl.BlockSpec((tm, D_MODEL), lambda i: (i, 0)), _const_spec(g.shape), _const_spec(wr.shape)],
        out_specs=[pl.BlockSpec((tm, D_MODEL), lambda i: (i, 0)), pl.BlockSpec((tm, LSE_LANES), lambda i: (i, 0))],
        out_shape=[jax.ShapeDtypeStruct((T, D_MODEL), BF16), jax.ShapeDtypeStruct((T, LSE_LANES), F32)],
        compiler_params=_params("parallel"),
        name="router",
    )(x2, g, wr)


ROW_SUBLANES = D_MODEL // LANES


def _row_copy(src, dst, sem):
    return pltpu.make_async_copy(src, dst, sem)


def _dispatch_kernel(slot_ref, h_ref, zero_ref, o_ref, sem, *, tm):
    del zero_ref

    def issue(r, _):
        for k in range(TOP_K):
            _row_copy(h_ref.at[r], o_ref.at[slot_ref[k, r]], sem).start()
        return 0

    lax.fori_loop(0, tm, issue, 0)

    def drain(r, _):
        for k in range(TOP_K):
            _row_copy(h_ref.at[r], o_ref.at[slot_ref[k, r]], sem).wait()
        return 0

    lax.fori_loop(0, tm, drain, 0)


def _dispatch(h2, slots, n_slots, tm):
    T = h2.shape[0]
    h3 = h2.reshape(T, ROW_SUBLANES, LANES)
    slots3 = slots.reshape(TOP_K, T // tm, tm).transpose(1, 0, 2)
    zeros = jnp.zeros((n_slots, ROW_SUBLANES, LANES), h2.dtype)
    out = pl.pallas_call(
        functools.partial(_dispatch_kernel, tm=tm),
        grid=(T // tm,),
        in_specs=[pl.BlockSpec((None, TOP_K, tm), lambda i: (i, 0, 0), memory_space=pltpu.SMEM),
                  pl.BlockSpec((tm, ROW_SUBLANES, LANES), lambda i: (i, 0, 0)),
                  pl.BlockSpec(memory_space=pl.ANY)],
        out_specs=pl.BlockSpec(memory_space=pl.ANY),
        out_shape=jax.ShapeDtypeStruct(zeros.shape, zeros.dtype),
        scratch_shapes=[pltpu.SemaphoreType.DMA],
        input_output_aliases={2: 0},
        compiler_params=pltpu.CompilerParams(dimension_semantics=("arbitrary",), vmem_limit_bytes=VMEM_LIMIT,
                                             has_side_effects=True),
        name="dispatch",
    )(slots3, h3, zeros)
    return out.reshape(n_slots, D_MODEL)


def _combine_kernel(slot_ref, x_ref, g_ref, y_ref, o_ref, buf, sem, *, tm, final_norm):
    def issue(r, _):
        for k in range(TOP_K):
            _row_copy(y_ref.at[slot_ref[k, r]], buf.at[k, r], sem).start()
        return 0

    lax.fori_loop(0, tm, issue, 0)

    def drain(r, _):
        for k in range(TOP_K):
            _row_copy(y_ref.at[slot_ref[k, r]], buf.at[k, r], sem).wait()
        return 0

    lax.fori_loop(0, tm, drain, 0)
    out = x_ref[...] + (buf[0].astype(F32) + buf[1].astype(F32))
    if final_norm:
        ss = jnp.sum(jnp.sum(out * out, axis=2, keepdims=True), axis=1, keepdims=True)
        out = out * lax.rsqrt(ss * (1.0 / D_MODEL) + EPS) * g_ref[...]
    o_ref[...] = out


def _combine(x2, y, slots, g_final, tm, final_norm):
    T = x2.shape[0]
    x3 = x2.reshape(T, ROW_SUBLANES, LANES)
    y3 = y.reshape(y.shape[0], ROW_SUBLANES, LANES)
    g3 = g_final.reshape(1, ROW_SUBLANES, LANES)
    slots3 = slots.reshape(TOP_K, T // tm, tm).transpose(1, 0, 2)
    blk = pl.BlockSpec((tm, ROW_SUBLANES, LANES), lambda i: (i, 0, 0))
    out = pl.pallas_call(
        functools.partial(_combine_kernel, tm=tm, final_norm=final_norm),
        grid=(T // tm,),
        in_specs=[pl.BlockSpec((None, TOP_K, tm), lambda i: (i, 0, 0), memory_space=pltpu.SMEM),
                  blk, _const_spec(g3.shape), pl.BlockSpec(memory_space=pl.ANY)],
        out_specs=blk,
        out_shape=jax.ShapeDtypeStruct(x3.shape, F32),
        scratch_shapes=[pltpu.VMEM((TOP_K, tm, ROW_SUBLANES, LANES), y.dtype), pltpu.SemaphoreType.DMA],
        compiler_params=_params("arbitrary"),
        name="combine",
    )(slots3, x3, g3, y3)
    return out.reshape(T, D_MODEL)


def _plan_slots(route, tm):
    T = route.shape[0]
    experts = route[:, :TOP_K].astype(jnp.int32)
    onehot = (experts[:, :, None] == jnp.arange(N_EXPERTS)[None, None, :]).astype(jnp.int32)
    flat = onehot.reshape(T * TOP_K, N_EXPERTS)
    csum = jnp.cumsum(flat, axis=0)
    rank = jnp.sum((csum - flat) * flat, axis=-1).reshape(T, TOP_K)
    counts = csum[-1]
    tiles = (counts + tm - 1) // tm
    tile_end = jnp.cumsum(tiles)
    start = (tile_end - tiles) * tm
    slots = jnp.sum(onehot * start[None, None, :], axis=-1) + rank
    n_tiles = (T * TOP_K) // tm + N_EXPERTS
    tile_ids = jnp.arange(n_tiles)
    tile_expert = jnp.minimum(jnp.sum(tile_ids[:, None] >= tile_end[None, :], axis=-1), N_EXPERTS - 1)
    n_valid = tile_end[-1:]
    last_valid = tile_expert[jnp.maximum(n_valid[0] - 1, 0)]
    tile_expert = jnp.where(tile_ids < n_valid[0], tile_expert, last_valid).astype(jnp.int32)
    return slots.T.astype(jnp.int32), tile_expert, n_valid.astype(jnp.int32), n_tiles


def _prep_layer(l, p):
    offs = np.cumsum((0,) + IN_SIZES)
    w_in = p['w_in'][l]
    qa, ka, va, cq, ckv, kr, qc, kc, vc = [w_in[:, int(offs[i]):int(offs[i + 1])] for i in range(9)]
    scale = HEAD_DIM ** -0.5
    order = list(A_HEAD_ORDER)
    qa = qa.reshape(D_MODEL, A_Q_HEADS, HEAD_DIM)[:, order, :].reshape(D_MODEL, A_WIDTH) * scale
    half = B_ROPE // 2
    z = lambda *s: jnp.zeros(s, F32)
    kr_pad = jnp.concatenate([z(D_MODEL, B_NOPE), kr, z(D_MODEL, LANES - B_NOPE - B_ROPE)], axis=1)
    kr_rot = jnp.concatenate([z(D_MODEL, B_NOPE), -kr[:, half:], kr[:, :half], z(D_MODEL, LANES - B_NOPE - B_ROPE)], axis=1)
    uq = p['mla_w_uq'][l].reshape(B_Q_LORA, B_HEADS, B_NOPE + B_ROPE)
    padq = z(B_Q_LORA, B_HEADS, LANES - B_NOPE - B_ROPE)
    uq_pad = jnp.concatenate([uq, padq], axis=-1).reshape(B_Q_LORA, B_HEADS * LANES)
    uq_rot = jnp.concatenate([z(B_Q_LORA, B_HEADS, B_NOPE), -uq[..., B_NOPE + half:], uq[..., B_NOPE:B_NOPE + half], padq],
                             axis=-1).reshape(B_Q_LORA, B_HEADS * LANES)
    ukv = p['mla_w_ukv'][l].reshape(B_KV_LORA, B_HEADS, B_NOPE + B_V)
    zk = z(B_KV_LORA, B_HEADS, HALF)
    uk_pad = jnp.concatenate([ukv[..., :B_NOPE], zk], axis=-1).reshape(B_KV_LORA, B_HEADS * LANES)
    uv = ukv[..., B_NOPE:]
    odd = (jnp.arange(B_HEADS) % 2 == 1)[None, :, None]
    uv_pad = jnp.where(odd, jnp.concatenate([zk, uv], axis=-1), jnp.concatenate([uv, zk], axis=-1))
    uv_pad = uv_pad.reshape(B_KV_LORA, B_HEADS * LANES)

    g_out = p['mix_out_norm_g'][l]
    w_out = p['w_out'][l]
    a_rows = np.concatenate([np.arange(HEAD_DIM) + HEAD_DIM * h for h in A_HEAD_ORDER])
    w_out = jnp.concatenate([w_out[:A_WIDTH][a_rows], w_out[A_WIDTH:]], axis=0)
    g_out_a = g_out[:A_WIDTH][a_rows]
    row = lambda v: v.reshape(1, -1).astype(F32)
    return {
        'g_mix': row(p['norm_mix_g'][l]),
        'wa': jnp.concatenate([qa, ka, va], axis=1).astype(BF16),
        'wc': jnp.concatenate([qc * scale, kc, vc], axis=1).astype(BF16),
        'wb': jnp.concatenate([cq, ckv], axis=1).astype(BF16),
        'wkr': jnp.concatenate([kr_pad, kr_rot], axis=1).astype(BF16),
        'g_q': row(p['mla_q_norm_g'][l]),
        'g_kv': row(p['mla_kv_norm_g'][l]),
        'wuq': jnp.concatenate([uq_pad, uq_rot], axis=1).astype(BF16),
        'wukv': jnp.concatenate([uk_pad, uv_pad], axis=1).astype(BF16),
        'sink': p['sink_a'][l][np.array(A_HEAD_ORDER)].astype(F32),
        'g_out_a': row(g_out_a),
        'g_out_b': row(g_out[A_WIDTH:A_WIDTH + B_WIDTH]),
        'g_out_c': row(g_out[A_WIDTH + B_WIDTH:]),
        'w_out': w_out.astype(BF16),
        'g_x': row(p['norm_x_g'][l]),
        'g_mem': row(p['norm_mem_g'][l]),
        'w_xq': p['w_xq'][l].astype(BF16),
        'w_xkv': p['w_xkv'][l].astype(BF16),
        'w_xo': p['w_xo'][l].astype(BF16),
        'g_ffn': row(p['norm_ffn_g'][l]),
    }


def _prep_ffn(w_gu, w_down):
    dff = w_down.shape[-2]
    return w_gu[..., :dff].astype(BF16), w_gu[..., dff:].astype(BF16), w_down.astype(BF16)


def _rope_tables(L):
    half = B_ROPE // 2
    pos = jnp.arange(L, dtype=F32)
    inv = ROPE_THETA ** (-jnp.arange(half, dtype=F32) / half)
    ang = pos[:, None] * inv[None, :]
    cos2 = jnp.tile(jnp.cos(ang), (1, 2))
    sin2 = jnp.tile(jnp.sin(ang), (1, 2))
    pad = jnp.zeros((L, LANES - B_NOPE - B_ROPE), F32)
    qscale = (B_NOPE + B_ROPE) ** -0.5 * LOG2E
    cq = jnp.concatenate([jnp.ones((L, B_NOPE), F32), cos2, pad], axis=1) * qscale
    sq = jnp.concatenate([jnp.zeros((L, B_NOPE), F32), sin2, pad], axis=1) * qscale
    ck = jnp.concatenate([jnp.zeros((L, B_NOPE), F32), cos2, pad], axis=1)
    sk = jnp.concatenate([jnp.zeros((L, B_NOPE), F32), sin2, pad], axis=1)
    return cq, sq, ck, sk


TOKEN_TILE = 512
EXPERT_TILE = 1024
FFN_TILE = 1024
ROUTE_TILE = 256


def _trunk(x, mem, layers, ffn, moe, moe_router, final_g):
    bsz, L, _ = x.shape
    T = bsz * L
    depth = len(layers)
    tm = min(TOKEN_TILE, L)
    x2 = x.reshape(T, D_MODEL)
    mem2 = mem.reshape(-1, D_MODEL)
    tabs = _rope_tables(L)
    slopes_a = _alibi_slopes(A_Q_HEADS)[list(A_HEAD_ORDER)]
    slopes_c = _alibi_slopes(C_HEADS)
    normed = False
    for l, lw in enumerate(layers):
        (qa, ka, va), qkv_c, (qb, kb, vb) = _in_proj(x2, lw, tabs, bsz, L, tm)
        as_seq = lambda t: t.reshape(bsz, 1, L, t.shape[-1])
        ya = _banded_attention(as_seq(qa), as_seq(ka), as_seq(va), slopes_a, W=A_WINDOW, n_kg=1, G=2, sink=lw['sink'])
        ya = ya.reshape(T, A_WIDTH)
        yb = _latent_attention(qb, kb, vb, bsz, L)
        ycs, lses = [], []
        for (w, r), (qc, kc, vc) in zip(DILATED_PAIRS, qkv_c):
            o, lse = _banded_attention(qc, kc, vc, slopes_c, W=w // (2 * r), n_kg=C_HEADS // 2, G=1, want_lse=True)
            ycs.append(o)
            lses.append(jnp.transpose(lse, (0, 2, 1, 3)).reshape(T, LSE_LANES))
        x2 = _mix_out(x2, ya, yb, ycs, lses, lw, L, tm)

        kvm = _norm_matmul(mem2, lw['g_mem'], lw['w_xkv'], min(512, mem2.shape[0]))
        x2 = _cross_attention(x2, kvm.reshape(bsz, mem.shape[1], -1), lw, L, tm)

        if l % 2 == 0:
            wg, wu, wd = ffn[l // 2]
            ft = min(FFN_TILE, T)
            nt = T // ft
            x2 = _swiglu(x2, lw['g_ffn'], wg, wu, wd, jnp.zeros((nt,), jnp.int32),
                         jnp.full((1,), nt, jnp.int32), ft, dense=True)
        else:
            wg, wu, wd = moe[l // 2]
            rt = min(ROUTE_TILE, T)
            et = min(EXPERT_TILE, T)
            hn, route = _router(x2, lw['g_ffn'], moe_router[l // 2], tm)
            slots, tile_expert, n_valid, n_tiles = _plan_slots(route, et)
            hs = _dispatch(hn, slots, n_tiles * et, rt)
            gates = jnp.zeros((n_tiles * et,), F32).at[slots.reshape(-1)].set(route[:, TOP_K:2 * TOP_K].T.reshape(-1))
            y = _swiglu(hs, gates.reshape(-1, 1), wg, wu, wd, tile_expert, n_valid, et, dense=False)
            last = l == depth - 1
            x2 = _combine(x2, y, slots, final_g, rt, final_norm=last)
            normed = last
    assert normed, "the final rmsnorm is fused into the routed layer's combine step"
    return x2.reshape(bsz, L, D_MODEL)


def kernel(x_prompt, x_sample, mem_prompt, mem_sample, norm_mix_g, w_in, sink_a, mla_q_norm_g, mla_kv_norm_g, mla_w_uq, mla_w_ukv, mix_out_norm_g, w_out, norm_x_g, norm_mem_g, w_xq, w_xkv, w_xo, norm_ffn_g, ffn_w_gu, ffn_w_down, moe_router, moe_w_gu, moe_w_down, final_norm_g):
    p = dict(norm_mix_g=norm_mix_g, w_in=w_in, sink_a=sink_a, mla_q_norm_g=mla_q_norm_g, mla_kv_norm_g=mla_kv_norm_g,
             mla_w_uq=mla_w_uq, mla_w_ukv=mla_w_ukv, mix_out_norm_g=mix_out_norm_g, w_out=w_out, norm_x_g=norm_x_g,
             norm_mem_g=norm_mem_g, w_xq=w_xq, w_xkv=w_xkv, w_xo=w_xo, norm_ffn_g=norm_ffn_g)
    depth = w_in.shape[0]
    layers = [_prep_layer(l, p) for l in range(depth)]
    ffn = [_prep_ffn(ffn_w_gu[i][None], ffn_w_down[i][None]) for i in range(ffn_w_gu.shape[0])]
    moe = [_prep_ffn(moe_w_gu[i], moe_w_down[i]) for i in range(moe_w_gu.shape[0])]
    routers = [jnp.pad(moe_router[i].astype(F32), ((0, 0), (0, LANES - N_EXPERTS))) for i in range(moe_router.shape[0])]
    final_g = final_norm_g.reshape(1, -1).astype(F32)
    y_prompt = _trunk(x_prompt, mem_prompt, layers, ffn, moe, routers, final_g)
    y_sample = _trunk(x_sample, mem_sample, layers, ffn, moe, routers, final_g)
    return (y_prompt, y_sample)
```

```python
import functools
import math

import numpy as np
import jax
import jax.numpy as jnp
from jax import lax
from jax.experimental import pallas as pl
from jax.experimental.pallas import tpu as pltpu

D_MODEL = 1024
HEAD_DIM = 64
A_Q_HEADS = 4
A_KV_HEADS = 2
A_WINDOW = 128
B_HEADS = 6
B_NOPE = 64
B_ROPE = 32
B_V = 64
B_Q_LORA = 384
B_KV_LORA = 256
ROPE_THETA = 10000.0
C_HEADS = 6
DILATED_PAIRS = ((128, 1), (512, 4), (2048, 16))
STRIDES = tuple(r for _, r in DILATED_PAIRS)
A_WIDTH = A_Q_HEADS * HEAD_DIM
B_WIDTH = B_HEADS * B_V
C_WIDTH = C_HEADS * HEAD_DIM
IN_SIZES = (A_WIDTH, A_KV_HEADS * HEAD_DIM, A_KV_HEADS * HEAD_DIM, B_Q_LORA, B_KV_LORA, B_ROPE,
            C_WIDTH, C_WIDTH, C_WIDTH)
X_HEADS = 4
X_HEAD_DIM = 128
N_EXPERTS = 8
TOP_K = 2
EPS = 1e-6
NEG = -1e30

LANES = 128
HALF = LANES // 2
LSE_LANES = 8
VMEM_LIMIT = 56 * 1024 * 1024
LOG2E = math.log2(math.e)

BF16 = jnp.bfloat16
F32 = jnp.float32

A_HEAD_ORDER = (0, 2, 1, 3)


def _params(*sem):
    return pltpu.CompilerParams(dimension_semantics=sem, vmem_limit_bytes=VMEM_LIMIT)


def _rms(x, g):
    return x * lax.rsqrt(jnp.mean(x * x, axis=-1, keepdims=True) + EPS) * g


def _dot(a, b):
    return jnp.dot(a, b, preferred_element_type=F32)


def _dot_nt(a, b):
    return lax.dot_general(a, b, (((1,), (1,)), ((), ())), preferred_element_type=F32)


def _const_spec(shape):
    n = len(shape)
    return pl.BlockSpec(shape, lambda *_: (0,) * n)


def _phase_spec(r, tm, nl, width):
    return pl.BlockSpec((None, r, tm // r, width), lambda i: (i // nl, 0, i % nl, 0))


def _in_kernel(*refs):
    it = iter(refs)
    x_ref, g_ref, wa_ref, wc_ref, wb_ref, wkr_ref, gq_ref, gkv_ref, wuq_ref, wukv_ref = [next(it) for _ in range(10)]
    cq_ref, sq_ref, ck_ref, sk_ref = [next(it) for _ in range(4)]
    qa_ref, ka_ref, va_ref = next(it), next(it), next(it)
    c_refs = [(next(it), next(it), next(it)) for _ in STRIDES]
    qb_ref, kb_ref, vb_ref = next(it), next(it), next(it)
    zc_scr = next(it)
    tm = x_ref.shape[0]

    h = _rms(x_ref[...], g_ref[...]).astype(BF16)
    za = _dot(h, wa_ref[...])
    qa_ref[...] = za[:, :A_WIDTH].astype(BF16)
    ka_ref[...] = za[:, A_WIDTH:A_WIDTH + LANES].astype(BF16)
    va_ref[...] = za[:, A_WIDTH + LANES:].astype(BF16)

    zc = _dot(h, wc_ref[...])
    ngrp = C_WIDTH // LANES
    for g in range(3 * ngrp):
        zc_scr[g] = zc[:, g * LANES:(g + 1) * LANES]
    for r, qkv_refs in zip(STRIDES, c_refs):
        for j in range(r):
            for n, ref in enumerate(qkv_refs):
                ref[j] = jnp.concatenate(
                    [zc_scr[n * ngrp + g, pl.ds(j, tm // r, stride=r), :] for g in range(ngrp)], axis=-1).astype(BF16)

    zb = _dot(h, wb_ref[...])
    hq = _rms(zb[:, :B_Q_LORA], gq_ref[...]).astype(BF16)
    hkv = _rms(zb[:, B_Q_LORA:], gkv_ref[...]).astype(BF16)
    zq = _dot(hq, wuq_ref[...])
    zkv = _dot(hkv, wukv_ref[...])
    zkr = _dot(h, wkr_ref[...])
    kr = zkr[:, :LANES] * ck_ref[...] + zkr[:, LANES:] * sk_ref[...]
    cq = cq_ref[...]
    sq = sq_ref[...]
    lane = lax.broadcasted_iota(jnp.int32, (1, LANES), 1)
    nb = B_HEADS * LANES
    for hd in range(B_HEADS):
        lo, hi = hd * LANES, (hd + 1) * LANES
        qb_ref[hd] = (zq[:, lo:hi] * cq + zq[:, nb + lo:nb + hi] * sq).astype(BF16)
        kb_ref[hd] = (zkv[:, lo:hi] + kr).astype(BF16)
        ones = (lane == (HALF if hd % 2 == 0 else 0)).astype(F32)
        vb_ref[hd] = (zkv[:, nb + lo:nb + hi] + ones).astype(BF16)


def _in_proj(x2, lw, tabs, bsz, L, tm):
    T = x2.shape[0]
    nl = L // tm
    row = lambda w: pl.BlockSpec((tm, w), lambda i: (i, 0))
    tab = pl.BlockSpec((tm, LANES), lambda i: (i % nl, 0))
    hm = pl.BlockSpec((B_HEADS, tm, LANES), lambda i: (0, i, 0))
    weights = (lw['g_mix'], lw['wa'], lw['wc'], lw['wb'], lw['wkr'], lw['g_q'], lw['g_kv'], lw['wuq'], lw['wukv'])
    out_specs = [row(A_WIDTH), row(LANES), row(LANES)]
    out_shape = [jax.ShapeDtypeStruct((T, w), BF16) for w in (A_WIDTH, LANES, LANES)]
    for r in STRIDES:
        assert tm % (16 * r) == 0, (tm, r)
        out_specs += [_phase_spec(r, tm, nl, C_WIDTH)] * 3
        out_shape += [jax.ShapeDtypeStruct((bsz, r, L // r, C_WIDTH), BF16)] * 3
    out_specs += [hm, hm, hm]
    out_shape += [jax.ShapeDtypeStruct((B_HEADS, T, LANES), BF16)] * 3
    outs = pl.pallas_call(
        _in_kernel,
        grid=(T // tm,),
        in_specs=[row(D_MODEL)] + [_const_spec(w.shape) for w in weights] + [tab] * 4,
        out_specs=out_specs,
        out_shape=out_shape,
        scratch_shapes=[pltpu.VMEM((3 * C_WIDTH // LANES, tm, LANES), F32)],
        compiler_params=_params("parallel"),
        name="in_proj",
    )(x2, *weights, *tabs)
    qkv_c = [outs[3 + 3 * n:6 + 3 * n] for n in range(len(STRIDES))]
    return outs[:3], qkv_c, outs[-3:]


def _banded_kernel(*refs, W, Q, R, n_kg, G, has_sink, want_lse, nchunks):
    it = iter(refs)
    q_ref = next(it)
    kp_ref, km_ref, kn_ref = next(it), next(it), next(it)
    vp_ref, vm_ref, vn_ref = next(it), next(it), next(it)
    bias_ref = next(it)
    sink_ref = next(it) if has_sink else None
    o_ref = next(it)
    lse_ref = next(it) if want_lse else None
    kfull, vfull = next(it), next(it)

    c = pl.program_id(2)
    P = q_ref.shape[0]
    kfull[:, 0:W] = kp_ref[...]
    kfull[:, W:W + R] = km_ref[...]
    kfull[:, W + R:] = kn_ref[...]
    vfull[:, 0:W] = vp_ref[...]
    vfull[:, W:W + R] = vm_ref[...]
    vfull[:, W + R:] = vn_ref[...]

    nsub = R // Q
    win = Q + 2 * W
    lo = lax.broadcasted_iota(jnp.int32, (Q, LANES), 1) < HALF
    col = lax.broadcasted_iota(jnp.int32, (2 * Q, win), 1)
    top = lax.broadcasted_iota(jnp.int32, (2 * Q, 1), 0) < Q
    lane8 = lax.broadcasted_iota(jnp.int32, (Q, LSE_LANES), 1)
    before_start = jnp.logical_and(c == 0, col < W)
    after_end = jnp.logical_and(c == nchunks - 1, col >= Q + W)
    units = [(ph, i, kg, g) for ph in range(P) for i in range(nsub) for kg in range(n_kg) for g in range(G)]

    scores = []
    for ph, i, kg, g in units:
        qg = kg * G + g
        qblk = q_ref[ph, i * Q:(i + 1) * Q, qg * LANES:(qg + 1) * LANES]
        zero = jnp.zeros_like(qblk)
        qm = jnp.concatenate([jnp.where(lo, qblk, zero), jnp.where(lo, zero, qblk)], axis=0)
        kwin = kfull[ph, i * Q:i * Q + win, kg * LANES:(kg + 1) * LANES]
        s = _dot_nt(qm, kwin) + bias_ref[qg]
        if i == 0:
            s = jnp.where(before_start, NEG, s)
        if i == nsub - 1:
            s = jnp.where(after_end, NEG, s)
        scores.append(s)

    probs = []
    for (ph, i, kg, g), s in zip(units, scores):
        qg = kg * G + g
        m = jnp.max(s, axis=-1, keepdims=True)
        if has_sink:
            sk = jnp.where(top, sink_ref[2 * qg], sink_ref[2 * qg + 1])
            m = jnp.maximum(m, sk)
        p = jnp.exp(s - m)
        den = jnp.sum(p, axis=-1, keepdims=True)
        if has_sink:
            den = den + jnp.exp(sk - m)
        probs.append((p.astype(BF16), den, m))

    lse_tiles = {(ph, i): jnp.zeros((Q, LSE_LANES), F32) for ph in range(P) for i in range(nsub)}
    for (ph, i, kg, g), (p, den, m) in zip(units, probs):
        qg = kg * G + g
        vwin = vfull[ph, i * Q:i * Q + win, kg * LANES:(kg + 1) * LANES]
        o = _dot(p, vwin) / den
        o_ref[ph, i * Q:(i + 1) * Q, qg * LANES:(qg + 1) * LANES] = jnp.where(lo, o[:Q], o[Q:]).astype(BF16)
        if want_lse:
            lse = m + jnp.log(den)
            tile = jnp.where(lane8 == 2 * qg, lse[:Q], lse_tiles[ph, i])
            lse_tiles[ph, i] = jnp.where(lane8 == 2 * qg + 1, lse[Q:], tile)
    if want_lse:
        for (ph, i), tile in lse_tiles.items():
            lse_ref[ph, i * Q:(i + 1) * Q, :] = tile


def _band_bias(slopes, step, W, Q):
    row = np.arange(Q)[:, None]
    col = np.arange(Q + 2 * W)[None, :]
    dist = np.abs(row + W - col)
    bias = -np.asarray(slopes, np.float32)[:, None, None] * (step * dist).astype(np.float32)[None]
    bias = np.where(dist[None] <= W, bias, np.float32(NEG)).astype(np.float32)
    return jnp.asarray(bias.reshape(len(slopes) // 2, 2 * Q, Q + 2 * W))


def _alibi_slopes(n):
    return (2.0 ** (-8.0 * np.arange(1, n + 1, dtype=np.float32) / n)).astype(np.float32)


BAND_ROWS = 512


def _band_tiles(Ls, W):
    R = min(Ls, BAND_ROWS)
    Q = min(R, 128)
    assert Ls % R == 0 and R % Q == 0 and R % W == 0 and Ls % W == 0, (Ls, R, Q, W)
    return R, Q


def _banded_attention(q, k, v, slopes, *, W, n_kg, G, sink=None, want_lse=False):
    bsz, r, Ls, Cq = q.shape
    Ck = k.shape[-1]
    R, Q = _band_tiles(Ls, W)
    nchunks = Ls // R
    nblk = Ls // W
    per = R // W
    P = math.gcd(r, max(1, BAND_ROWS // R))
    bias = _band_bias(slopes, r, W, Q)
    q_spec = pl.BlockSpec((None, P, R, Cq), lambda b, j, c: (b, j, c, 0))
    main = pl.BlockSpec((None, P, R, Ck), lambda b, j, c: (b, j, c, 0))
    prev = pl.BlockSpec((None, P, W, Ck), lambda b, j, c: (b, j, jnp.maximum(c * per - 1, 0), 0))
    nxt = pl.BlockSpec((None, P, W, Ck), lambda b, j, c: (b, j, jnp.minimum((c + 1) * per, nblk - 1), 0))
    in_specs = [q_spec, prev, main, nxt, prev, main, nxt, _const_spec(bias.shape)]
    args = [q, k, k, k, v, v, v, bias]
    if sink is not None:
        in_specs.append(pl.BlockSpec(memory_space=pltpu.SMEM))
        args.append(sink)
    out_specs = [q_spec]
    out_shape = [jax.ShapeDtypeStruct(q.shape, BF16)]
    if want_lse:
        out_specs.append(pl.BlockSpec((None, P, R, LSE_LANES), lambda b, j, c: (b, j, c, 0)))
        out_shape.append(jax.ShapeDtypeStruct((bsz, r, Ls, LSE_LANES), F32))
    kern = functools.partial(_banded_kernel, W=W, Q=Q, R=R, n_kg=n_kg, G=G, has_sink=sink is not None,
                             want_lse=want_lse, nchunks=nchunks)
    outs = pl.pallas_call(
        kern,
        grid=(bsz, r // P, nchunks),
        in_specs=in_specs,
        out_specs=out_specs,
        out_shape=out_shape,
        scratch_shapes=[pltpu.VMEM((P, R + 2 * W, Ck), BF16), pltpu.VMEM((P, R + 2 * W, Ck), BF16)],
        compiler_params=_params("parallel", "parallel", "parallel"),
        name=f"banded_w{W}_r{r}",
    )(*args)
    return (outs[0], outs[1]) if want_lse else outs[0]


def _flash_kernel(q_ref, k_ref, v_ref, o_ref, *, tk, nk):
    tq = q_ref.shape[1]
    qs = [q_ref[0], q_ref[1]]

    def body(t, carry):
        off = pl.multiple_of(t * tk, tk)
        ss = [_dot_nt(qs[h], k_ref[h, pl.ds(off, tk), :]) for h in range(2)]
        ms = [jnp.maximum(carry[2 * h], jnp.max(ss[h], axis=-1, keepdims=True)) for h in range(2)]
        ps = [jnp.exp2(ss[h] - ms[h]).astype(BF16) for h in range(2)]
        out = []
        for h in range(2):
            alpha = jnp.exp2(carry[2 * h] - ms[h])
            out += [ms[h], carry[2 * h + 1] * alpha + _dot(ps[h], v_ref[h, pl.ds(off, tk), :])]
        return tuple(out)

    m0 = jnp.full((tq, 1), NEG, F32)
    acc0 = jnp.zeros((tq, LANES), F32)
    res = lax.fori_loop(0, nk, body, (m0, acc0, m0, acc0), unroll=4)
    acc_even, acc_odd = res[1], res[3]
    out_even = acc_even / acc_even[:, HALF:HALF + 1]
    out_odd = acc_odd / acc_odd[:, 0:1]
    lo = lax.broadcasted_iota(jnp.int32, (tq, LANES), 1) < HALF
    o_ref[...] = jnp.where(lo, out_even, out_odd).astype(BF16)


def _flash_tiles(L):
    tq = min(L, 512)
    tk = min(L, 512)
    assert L % tq == 0 and L % tk == 0
    return tq, tk


def _latent_attention(qb, kb, vb, bsz, L):
    tq, tk = _flash_tiles(L)
    q4 = qb.reshape(B_HEADS, bsz, L, LANES)
    k4 = kb.reshape(B_HEADS, bsz, L, LANES)
    v4 = vb.reshape(B_HEADS, bsz, L, LANES)
    kv_spec = pl.BlockSpec((2, None, L, LANES), lambda b, hp, i: (hp, b, 0, 0))
    out = pl.pallas_call(
        functools.partial(_flash_kernel, tk=tk, nk=L // tk),
        grid=(bsz, B_HEADS // 2, L // tq),
        in_specs=[pl.BlockSpec((2, None, tq, LANES), lambda b, hp, i: (hp, b, i, 0)), kv_spec, kv_spec],
        out_specs=pl.BlockSpec((None, tq, LANES), lambda b, hp, i: (b, i, hp)),
        out_shape=jax.ShapeDtypeStruct((bsz, L, B_WIDTH), BF16),
        compiler_params=_params("parallel", "parallel", "parallel"),
        name="latent_flash",
    )(q4, k4, v4)
    return out.reshape(bsz * L, B_WIDTH)


def _mix_out_kernel(*refs):
    it = iter(refs)
    x_ref, ya_ref, yb_ref = next(it), next(it), next(it)
    c_refs = [next(it) for _ in STRIDES]
    l_refs = [next(it) for _ in STRIDES]
    ga_ref, gb_ref, gc_ref, w_ref = next(it), next(it), next(it), next(it)
    o_ref = next(it)
    c_scr = next(it)
    tm = x_ref.shape[0]
    ngrp = C_WIDTH // LANES

    for n, r in enumerate(STRIDES):
        for j in range(r):
            blk = c_refs[n][j].astype(F32)
            for g in range(ngrp):
                c_scr[n * ngrp + g, pl.ds(j, tm // r, stride=r), :] = blk[:, g * LANES:(g + 1) * LANES]

    ls = [l_ref[...] for l_ref in l_refs]
    mx = functools.reduce(jnp.maximum, ls)
    es = [jnp.exp(l - mx) for l in ls]
    tot = functools.reduce(lambda a, b: a + b, es)
    wts = [e / tot for e in es]
    lo = lax.broadcasted_iota(jnp.int32, (tm, LANES), 1) < HALF
    groups = []
    for kg in range(ngrp):
        acc = jnp.zeros((tm, LANES), F32)
        for n, w in enumerate(wts):
            wfull = jnp.where(lo, w[:, 2 * kg:2 * kg + 1], w[:, 2 * kg + 1:2 * kg + 2])
            acc = acc + c_scr[n * ngrp + kg] * wfull
        groups.append(acc)
    yc = jnp.concatenate(groups, axis=-1)
    y = jnp.concatenate([
        _rms(ya_ref[...].astype(F32), ga_ref[...]).astype(BF16),
        _rms(yb_ref[...].astype(F32), gb_ref[...]).astype(BF16),
        _rms(yc, gc_ref[...]).astype(BF16)], axis=-1)
    o_ref[...] = x_ref[...] + _dot(y, w_ref[...])


def _mix_out(x2, ya, yb, ycs, lses, lw, L, tm):
    T = x2.shape[0]
    nl = L // tm
    row = lambda w: pl.BlockSpec((tm, w), lambda i: (i, 0))
    weights = (lw['g_out_a'], lw['g_out_b'], lw['g_out_c'], lw['w_out'])
    n = len(STRIDES)
    return pl.pallas_call(
        _mix_out_kernel,
        grid=(T // tm,),
        in_specs=([row(D_MODEL), row(A_WIDTH), row(B_WIDTH)]
                  + [_phase_spec(r, tm, nl, C_WIDTH) for r in STRIDES]
                  + [row(LSE_LANES)] * n
                  + [_const_spec(w.shape) for w in weights]),
        out_specs=row(D_MODEL),
        out_shape=jax.ShapeDtypeStruct((T, D_MODEL), F32),
        scratch_shapes=[pltpu.VMEM((n * C_WIDTH // LANES, tm, LANES), F32)],
        compiler_params=_params("parallel"),
        name="mix_out",
    )(x2, ya, yb, *ycs, *lses, *weights)


def _norm_matmul_kernel(x_ref, g_ref, w_ref, o_ref):
    o_ref[...] = _dot(_rms(x_ref[...], g_ref[...]).astype(BF16), w_ref[...]).astype(o_ref.dtype)


def _norm_matmul(x2, g, w, tm):
    T, K = x2.shape
    N = w.shape[1]
    return pl.pallas_call(
        _norm_matmul_kernel,
        grid=(T // tm,),
        in_specs=[pl.BlockSpec((tm, K), lambda i: (i, 0)), _const_spec(g.shape), _const_spec(w.shape)],
        out_specs=pl.BlockSpec((tm, N), lambda i: (i, 0)),
        out_shape=jax.ShapeDtypeStruct((T, N), BF16),
        compiler_params=_params("parallel"),
        name="norm_matmul",
    )(x2, g, w)


def _cross_kernel(x_ref, kv_ref, g_ref, wq_ref, wo_ref, o_ref):
    x = x_ref[...]
    h = _rms(x, g_ref[...]).astype(BF16)
    q = (_dot(h, wq_ref[...]) * (X_HEAD_DIM ** -0.5)).astype(BF16)
    xd = X_HEADS * X_HEAD_DIM
    outs = []
    for hd in range(X_HEADS):
        sl = slice(hd * X_HEAD_DIM, (hd + 1) * X_HEAD_DIM)
        k = kv_ref[:, sl]
        v = kv_ref[:, xd + hd * X_HEAD_DIM:xd + (hd + 1) * X_HEAD_DIM]
        s = _dot_nt(q[:, sl], k)
        p = jnp.exp(s - jnp.max(s, axis=-1, keepdims=True))
        den = jnp.sum(p, axis=-1, keepdims=True)
        outs.append((_dot(p.astype(BF16), v) / den).astype(BF16))
    o_ref[...] = x + _dot(jnp.concatenate(outs, axis=-1), wo_ref[...])


def _cross_attention(x2, kvm, lw, L, tm):
    T = x2.shape[0]
    nl = L // tm
    M = kvm.shape[1]
    xd = X_HEADS * X_HEAD_DIM
    weights = (lw['g_x'], lw['w_xq'], lw['w_xo'])
    return pl.pallas_call(
        _cross_kernel,
        grid=(T // tm,),
        in_specs=[pl.BlockSpec((tm, D_MODEL), lambda i: (i, 0)),
                  pl.BlockSpec((None, M, 2 * xd), lambda i: (i // nl, 0, 0))]
                 + [_const_spec(w.shape) for w in weights],
        out_specs=pl.BlockSpec((tm, D_MODEL), lambda i: (i, 0)),
        out_shape=jax.ShapeDtypeStruct((T, D_MODEL), F32),
        compiler_params=_params("parallel"),
        name="cross_attention",
    )(x2, kvm, *weights)


def _silu_mul(gate, up):
    return gate * jax.nn.sigmoid(gate) * up


def _swiglu_kernel(x_ref, g_ref, wg_ref, wu_ref, wd_ref, o_ref, h_scr, acc_scr):
    j = pl.program_id(1)

    @pl.when(j == 0)
    def _():
        h_scr[...] = _rms(x_ref[...], g_ref[...]).astype(BF16)
        acc_scr[...] = jnp.zeros_like(acc_scr)

    h = h_scr[...]
    act = _silu_mul(_dot(h, wg_ref[...]), _dot(h, wu_ref[...])).astype(BF16)
    acc_scr[...] += _dot(act, wd_ref[...])

    @pl.when(j == pl.num_programs(1) - 1)
    def _():
        o_ref[...] = x_ref[...] + acc_scr[...]


def _ff_chunk(dff, limit):
    for c in (1792, 1408, 1024, 512, 256, 128):
        if c <= limit and dff % c == 0:
            return c
    raise ValueError(dff)


def _swiglu(x2, g, wg, wu, wd, tm):
    T = x2.shape[0]
    dff = wg.shape[1]
    ck = _ff_chunk(dff, 1408)
    return pl.pallas_call(
        _swiglu_kernel,
        grid=(T // tm, dff // ck),
        in_specs=[pl.BlockSpec((tm, D_MODEL), lambda i, j: (i, 0)),
                  _const_spec(g.shape),
                  pl.BlockSpec((D_MODEL, ck), lambda i, j: (0, j)),
                  pl.BlockSpec((D_MODEL, ck), lambda i, j: (0, j)),
                  pl.BlockSpec((ck, D_MODEL), lambda i, j: (j, 0))],
        out_specs=pl.BlockSpec((tm, D_MODEL), lambda i, j: (i, 0)),
        out_shape=jax.ShapeDtypeStruct((T, D_MODEL), F32),
        scratch_shapes=[pltpu.VMEM((tm, D_MODEL), BF16), pltpu.VMEM((tm, D_MODEL), F32)],
        compiler_params=_params("parallel", "arbitrary"),
        name="swiglu_dense",
    )(x2, g, wg, wu, wd)


def _router_logits(h, wr):
    return jnp.dot(h, wr, preferred_element_type=F32, precision=lax.Precision.HIGHEST)


def _router_kernel(x_ref, g_ref, wr_ref, tri_ref, c0_ref, route_ref, cnt_ref, run_scr):
    @pl.when(pl.program_id(0) == 0)
    def _():
        run_scr[...] = c0_ref[...]

    logits = _router_logits(_rms(x_ref[...], g_ref[...]), wr_ref[...])
    tm = logits.shape[0]
    lane = lax.broadcasted_iota(jnp.int32, (tm, LANES), 1)
    logits = jnp.where(lane < N_EXPERTS, logits, NEG)
    v1 = jnp.max(logits, axis=-1, keepdims=True)
    i1 = jnp.min(jnp.where(logits == v1, lane, LANES), axis=-1, keepdims=True)
    rest = jnp.where(lane == i1, NEG, logits)
    v2 = jnp.max(rest, axis=-1, keepdims=True)
    i2 = jnp.min(jnp.where(rest == v2, lane, LANES), axis=-1, keepdims=True)
    pid = jnp.minimum(i1, i2) * N_EXPERTS + jnp.maximum(i1, i2)
    onehot = (lane == pid).astype(F32)
    before = _dot(tri_ref[...], onehot.astype(BF16)) + run_scr[...]
    rank = jnp.sum(onehot * before, axis=-1, keepdims=True)
    run_scr[...] += jnp.sum(onehot, axis=0, keepdims=True)
    cnt_ref[...] = run_scr[...]
    lane8 = lax.broadcasted_iota(jnp.int32, (tm, LSE_LANES), 1)
    route_ref[...] = jnp.where(lane8 == 0, pid, jnp.where(lane8 == 1, rank.astype(jnp.int32), 0))


def _router(x2, g, wr, counts0, tm):
    T = x2.shape[0]
    tri = (np.arange(tm)[:, None] > np.arange(tm)[None, :]).astype(np.float32)
    tri = jnp.asarray(tri, BF16)
    return pl.pallas_call(
        _router_kernel,
        grid=(T // tm,),
        in_specs=[pl.BlockSpec((tm, D_MODEL), lambda i: (i, 0)), _const_spec(g.shape), _const_spec(wr.shape),
                  _const_spec(tri.shape), _const_spec(counts0.shape)],
        out_specs=[pl.BlockSpec((tm, LSE_LANES), lambda i: (i, 0)), _const_spec(counts0.shape)],
        out_shape=[jax.ShapeDtypeStruct((T, LSE_LANES), jnp.int32), jax.ShapeDtypeStruct(counts0.shape, F32)],
        scratch_shapes=[pltpu.VMEM(counts0.shape, F32)],
        compiler_params=_params("arbitrary"),
        name="router",
    )(x2, g, wr, tri, counts0)


ROW_SUBLANES = D_MODEL // LANES


def _slab_to_rows(ref, n):
    return jnp.concatenate([ref[pl.ds(c, n, stride=ROW_SUBLANES), :] for c in range(ROW_SUBLANES)], axis=-1)


def _rows_to_slab(ref, rows, n):
    for c in range(ROW_SUBLANES):
        ref[pl.ds(c, n, stride=ROW_SUBLANES), :] = rows[:, c * LANES:(c + 1) * LANES]


def _slab(ref, row):
    return ref.at[pl.ds(pl.multiple_of(row * ROW_SUBLANES, ROW_SUBLANES), ROW_SUBLANES)]


def _move_rows(n, copy_of_row):
    def issue(r, _):
        copy_of_row(r).start()
        return 0

    def drain(r, _):
        copy_of_row(r).wait()
        return 0

    lax.fori_loop(0, n, issue, 0)
    lax.fori_loop(0, n, drain, 0)


def _dispatch_kernel(route_ref, start_ref, x_ref, prev_ref, o_ref, slab, sem, *, tm):
    del prev_ref
    _rows_to_slab(slab, x_ref[...], tm)

    def copy_of_row(r):
        slot = start_ref[route_ref[0, r]] + route_ref[1, r]
        return pltpu.make_async_copy(_slab(slab, r), _slab(o_ref, slot), sem)

    _move_rows(tm, copy_of_row)


def _route_blocks(route, tm):
    T = route.shape[0]
    return route[:, :2].T.reshape(2, T // tm, tm).transpose(1, 0, 2)


def _dispatch(x2, route, start, sorted_rows, tm):
    T = x2.shape[0]
    return pl.pallas_call(
        functools.partial(_dispatch_kernel, tm=tm),
        grid=(T // tm,),
        in_specs=[pl.BlockSpec((None, 2, tm), lambda i: (i, 0, 0), memory_space=pltpu.SMEM),
                  pl.BlockSpec(memory_space=pltpu.SMEM),
                  pl.BlockSpec((tm, D_MODEL), lambda i: (i, 0)),
                  pl.BlockSpec(memory_space=pl.ANY)],
        out_specs=pl.BlockSpec(memory_space=pl.ANY),
        out_shape=jax.ShapeDtypeStruct(sorted_rows.shape, sorted_rows.dtype),
        scratch_shapes=[pltpu.VMEM((tm * ROW_SUBLANES, LANES), F32), pltpu.SemaphoreType.DMA],
        input_output_aliases={3: 0},
        compiler_params=pltpu.CompilerParams(dimension_semantics=("arbitrary",), vmem_limit_bytes=VMEM_LIMIT,
                                             has_side_effects=True),
        name="dispatch",
    )(_route_blocks(route, tm), start, x2, sorted_rows)


def _unpermute_kernel(route_ref, start_ref, y_ref, o_ref, slab, sem, *, tm):
    def copy_of_row(r):
        slot = start_ref[route_ref[0, r]] + route_ref[1, r]
        return pltpu.make_async_copy(_slab(y_ref, slot), _slab(slab, r), sem)

    _move_rows(tm, copy_of_row)
    o_ref[...] = _slab_to_rows(slab, tm)


def _unpermute(y_sorted, route, start, T, tm):
    return pl.pallas_call(
        functools.partial(_unpermute_kernel, tm=tm),
        grid=(T // tm,),
        in_specs=[pl.BlockSpec((None, 2, tm), lambda i: (i, 0, 0), memory_space=pltpu.SMEM),
                  pl.BlockSpec(memory_space=pltpu.SMEM),
                  pl.BlockSpec(memory_space=pl.ANY)],
        out_specs=pl.BlockSpec((tm, D_MODEL), lambda i: (i, 0)),
        out_shape=jax.ShapeDtypeStruct((T, D_MODEL), F32),
        scratch_shapes=[pltpu.VMEM((tm * ROW_SUBLANES, LANES), F32), pltpu.SemaphoreType.DMA],
        compiler_params=_params("arbitrary"),
        name="unpermute",
    )(_route_blocks(route, tm), start, y_sorted)


def _pair_experts_kernel(ta_ref, tb_ref, nv_ref, xs_ref, g_ref, wr_ref, wg_ref, wu_ref, wd_ref, gf_ref, o_ref,
                         x_scr, h_scr, gate_scr, acc_scr, *, nj, final_norm):
    i = pl.program_id(0)
    j = pl.program_id(1)
    tm = x_scr.shape[0]

    @pl.when(i < nv_ref[0])
    def _():
        @pl.when(j == 0)
        def _():
            x = _slab_to_rows(xs_ref, tm)
            x_scr[...] = x
            h = _rms(x, g_ref[...])
            h_scr[...] = h.astype(BF16)
            logits = _router_logits(h, wr_ref[...])
            lane = lax.broadcasted_iota(jnp.int32, logits.shape, 1)
            la = jnp.sum(jnp.where(lane == ta_ref[i], logits, 0.0), axis=-1, keepdims=True)
            lb = jnp.sum(jnp.where(lane == tb_ref[i], logits, 0.0), axis=-1, keepdims=True)
            mx = jnp.maximum(la, lb)
            ea, eb = jnp.exp(la - mx), jnp.exp(lb - mx)
            gate_scr[0] = ea / (ea + eb)
            gate_scr[1] = eb / (ea + eb)
            acc_scr[...] = jnp.zeros_like(acc_scr)

        h = h_scr[...]
        gate = jnp.where(j < nj, gate_scr[0], gate_scr[1])
        act = (_silu_mul(_dot(h, wg_ref[...]), _dot(h, wu_ref[...])) * gate).astype(BF16)
        acc_scr[...] += _dot(act, wd_ref[...])

        @pl.when(j == 2 * nj - 1)
        def _():
            out = x_scr[...] + acc_scr[...]
            if final_norm:
                out = _rms(out, gf_ref[...])
            _rows_to_slab(o_ref, out, tm)

    @pl.when(jnp.logical_and(i >= nv_ref[0], j == 2 * nj - 1))
    def _():
        o_ref[...] = jnp.zeros_like(o_ref)


def _pair_experts(xs, g, wr, wg, wu, wd, g_final, tile_a, tile_b, n_valid, tm, final_norm):
    rows = xs.shape[0] // ROW_SUBLANES
    dff = wg.shape[2]
    ck = _ff_chunk(dff, 1792)
    nj = dff // ck

    def expert(i, j, ta, tb):
        return jnp.where(j < nj, ta[i], tb[i])

    def chunk(i, j, nv):
        return jnp.where(i < nv[0], j % nj, nj - 1)

    slab_spec = pl.BlockSpec((tm * ROW_SUBLANES, LANES), lambda i, j, ta, tb, nv: (i, 0))
    const = lambda a: pl.BlockSpec(a.shape, lambda i, j, ta, tb, nv: (0,) * a.ndim)
    grid_spec = pltpu.PrefetchScalarGridSpec(
        num_scalar_prefetch=3,
        grid=(rows // tm, 2 * nj),
        in_specs=[slab_spec, const(g), const(wr),
                  pl.BlockSpec((None, D_MODEL, ck), lambda i, j, ta, tb, nv: (expert(i, j, ta, tb), 0, chunk(i, j, nv))),
                  pl.BlockSpec((None, D_MODEL, ck), lambda i, j, ta, tb, nv: (expert(i, j, ta, tb), 0, chunk(i, j, nv))),
                  pl.BlockSpec((None, ck, D_MODEL), lambda i, j, ta, tb, nv: (expert(i, j, ta, tb), chunk(i, j, nv), 0)),
                  const(g_final)],
        out_specs=slab_spec,
        scratch_shapes=[pltpu.VMEM((tm, D_MODEL), F32), pltpu.VMEM((tm, D_MODEL), BF16),
                        pltpu.VMEM((2, tm, 1), F32), pltpu.VMEM((tm, D_MODEL), F32)],
    )
    return pl.pallas_call(
        functools.partial(_pair_experts_kernel, nj=nj, final_norm=final_norm),
        grid_spec=grid_spec,
        out_shape=jax.ShapeDtypeStruct(xs.shape, F32),
        compiler_params=_params("parallel", "arbitrary"),
        name="pair_experts",
    )(tile_a, tile_b, n_valid, xs, g, wr, wg, wu, wd, g_final)


def _plan_pairs(counts, n_tiles, tm):
    counts = counts.reshape(-1).astype(jnp.int32)
    tiles = (counts + tm - 1) // tm
    tile_end = jnp.cumsum(tiles)
    start = (tile_end - tiles) * tm
    n_valid = tile_end[-1:]
    tile_ids = jnp.minimum(jnp.arange(n_tiles), n_valid[0] - 1)
    tile_pid = jnp.sum(tile_ids[:, None] >= tile_end[None, :], axis=-1)
    tile_a = (tile_pid // N_EXPERTS).astype(jnp.int32)
    tile_b = (tile_pid % N_EXPERTS).astype(jnp.int32)
    return start.astype(jnp.int32), tile_a, tile_b, n_valid.astype(jnp.int32)


def _prep_layer(l, p):
    offs = np.cumsum((0,) + IN_SIZES)
    w_in = p['w_in'][l]
    qa, ka, va, cq, ckv, kr, qc, kc, vc = [w_in[:, int(offs[i]):int(offs[i + 1])] for i in range(9)]
    scale = HEAD_DIM ** -0.5
    order = list(A_HEAD_ORDER)
    qa = qa.reshape(D_MODEL, A_Q_HEADS, HEAD_DIM)[:, order, :].reshape(D_MODEL, A_WIDTH) * scale
    half = B_ROPE // 2
    z = lambda *s: jnp.zeros(s, F32)
    kr_pad = jnp.concatenate([z(D_MODEL, B_NOPE), kr, z(D_MODEL, LANES - B_NOPE - B_ROPE)], axis=1)
    kr_rot = jnp.concatenate([z(D_MODEL, B_NOPE), -kr[:, half:], kr[:, :half], z(D_MODEL, LANES - B_NOPE - B_ROPE)], axis=1)
    uq = p['mla_w_uq'][l].reshape(B_Q_LORA, B_HEADS, B_NOPE + B_ROPE)
    padq = z(B_Q_LORA, B_HEADS, LANES - B_NOPE - B_ROPE)
    uq_pad = jnp.concatenate([uq, padq], axis=-1).reshape(B_Q_LORA, B_HEADS * LANES)
    uq_rot = jnp.concatenate([z(B_Q_LORA, B_HEADS, B_NOPE), -uq[..., B_NOPE + half:], uq[..., B_NOPE:B_NOPE + half], padq],
                             axis=-1).reshape(B_Q_LORA, B_HEADS * LANES)
    ukv = p['mla_w_ukv'][l].reshape(B_KV_LORA, B_HEADS, B_NOPE + B_V)
    zk = z(B_KV_LORA, B_HEADS, HALF)
    uk_pad = jnp.concatenate([ukv[..., :B_NOPE], zk], axis=-1).reshape(B_KV_LORA, B_HEADS * LANES)
    uv = ukv[..., B_NOPE:]
    odd = (jnp.arange(B_HEADS) % 2 == 1)[None, :, None]
    uv_pad = jnp.where(odd, jnp.concatenate([zk, uv], axis=-1), jnp.concatenate([uv, zk], axis=-1))
    uv_pad = uv_pad.reshape(B_KV_LORA, B_HEADS * LANES)

    g_out = p['mix_out_norm_g'][l]
    w_out = p['w_out'][l]
    a_rows = np.concatenate([np.arange(HEAD_DIM) + HEAD_DIM * h for h in A_HEAD_ORDER])
    w_out = jnp.concatenate([w_out[:A_WIDTH][a_rows], w_out[A_WIDTH:]], axis=0)
    g_out_a = g_out[:A_WIDTH][a_rows]
    row = lambda v: v.reshape(1, -1).astype(F32)
    return {
        'g_mix': row(p['norm_mix_g'][l]),
        'wa': jnp.concatenate([qa, ka, va], axis=1).astype(BF16),
        'wc': jnp.concatenate([qc * scale, kc, vc], axis=1).astype(BF16),
        'wb': jnp.concatenate([cq, ckv], axis=1).astype(BF16),
        'wkr': jnp.concatenate([kr_pad, kr_rot], axis=1).astype(BF16),
        'g_q': row(p['mla_q_norm_g'][l]),
        'g_kv': row(p['mla_kv_norm_g'][l]),
        'wuq': jnp.concatenate([uq_pad, uq_rot], axis=1).astype(BF16),
        'wukv': jnp.concatenate([uk_pad, uv_pad], axis=1).astype(BF16),
        'sink': p['sink_a'][l][np.array(A_HEAD_ORDER)].astype(F32),
        'g_out_a': row(g_out_a),
        'g_out_b': row(g_out[A_WIDTH:A_WIDTH + B_WIDTH]),
        'g_out_c': row(g_out[A_WIDTH + B_WIDTH:]),
        'w_out': w_out.astype(BF16),
        'g_x': row(p['norm_x_g'][l]),
        'g_mem': row(p['norm_mem_g'][l]),
        'w_xq': p['w_xq'][l].astype(BF16),
        'w_xkv': p['w_xkv'][l].astype(BF16),
        'w_xo': p['w_xo'][l].astype(BF16),
        'g_ffn': row(p['norm_ffn_g'][l]),
    }


def _prep_ffn(w_gu, w_down):
    dff = w_down.shape[-2]
    return w_gu[..., :dff].astype(BF16), w_gu[..., dff:].astype(BF16), w_down.astype(BF16)


def _rope_tables(L):
    half = B_ROPE // 2
    pos = jnp.arange(L, dtype=F32)
    inv = ROPE_THETA ** (-jnp.arange(half, dtype=F32) / half)
    ang = pos[:, None] * inv[None, :]
    cos2 = jnp.tile(jnp.cos(ang), (1, 2))
    sin2 = jnp.tile(jnp.sin(ang), (1, 2))
    pad = jnp.zeros((L, LANES - B_NOPE - B_ROPE), F32)
    qscale = (B_NOPE + B_ROPE) ** -0.5 * LOG2E
    cq = jnp.concatenate([jnp.ones((L, B_NOPE), F32), cos2, pad], axis=1) * qscale
    sq = jnp.concatenate([jnp.zeros((L, B_NOPE), F32), sin2, pad], axis=1) * qscale
    ck = jnp.concatenate([jnp.zeros((L, B_NOPE), F32), cos2, pad], axis=1)
    sk = jnp.concatenate([jnp.zeros((L, B_NOPE), F32), sin2, pad], axis=1)
    return cq, sq, ck, sk


TOKEN_TILE = 512
EXPERT_TILE = 512
FFN_TILE = 1024
MOVE_TILE = 256
N_PAIRS_USED = N_EXPERTS * (N_EXPERTS - 1) // 2


def _mixer_and_cross(x2, mem, lw, bsz, L):
    T = bsz * L
    tm = min(TOKEN_TILE, L)
    tabs = _rope_tables(L)
    slopes_a = _alibi_slopes(A_Q_HEADS)[list(A_HEAD_ORDER)]
    slopes_c = _alibi_slopes(C_HEADS)
    (qa, ka, va), qkv_c, (qb, kb, vb) = _in_proj(x2, lw, tabs, bsz, L, tm)
    as_seq = lambda t: t.reshape(bsz, 1, L, t.shape[-1])
    ya = _banded_attention(as_seq(qa), as_seq(ka), as_seq(va), slopes_a, W=A_WINDOW, n_kg=1, G=2, sink=lw['sink'])
    ya = ya.reshape(T, A_WIDTH)
    yb = _latent_attention(qb, kb, vb, bsz, L)
    ycs, lses = [], []
    for (w, r), (qc, kc, vc) in zip(DILATED_PAIRS, qkv_c):
        o, lse = _banded_attention(qc, kc, vc, slopes_c, W=w // (2 * r), n_kg=C_HEADS // 2, G=1, want_lse=True)
        ycs.append(o)
        lses.append(jnp.transpose(lse, (0, 2, 1, 3)).reshape(T, LSE_LANES))
    x2 = _mix_out(x2, ya, yb, ycs, lses, lw, L, tm)
    mem2 = mem.reshape(-1, D_MODEL)
    kvm = _norm_matmul(mem2, lw['g_mem'], lw['w_xkv'], min(512, mem2.shape[0]))
    return _cross_attention(x2, kvm.reshape(bsz, mem.shape[1], -1), lw, L, tm)


def _routed_swiglu(xs2, g, wr, wg, wu, wd, final_g, final_norm):
    total = sum(x.shape[0] for x in xs2)
    et = min(EXPERT_TILE, total)
    n_tiles = total // et + N_PAIRS_USED
    counts = jnp.zeros((1, LANES), F32)
    routes = []
    for x2 in xs2:
        route, counts = _router(x2, g, wr, counts, min(TOKEN_TILE, x2.shape[0]))
        routes.append(route)
    start, tile_a, tile_b, n_valid = _plan_pairs(counts, n_tiles, et)
    sorted_rows = jnp.zeros((n_tiles * et * ROW_SUBLANES, LANES), F32)
    for x2, route in zip(xs2, routes):
        sorted_rows = _dispatch(x2, route, start, sorted_rows, min(MOVE_TILE, x2.shape[0]))
    y_sorted = _pair_experts(sorted_rows, g, wr, wg, wu, wd, final_g, tile_a, tile_b, n_valid, et, final_norm)
    return [_unpermute(y_sorted, route, start, x2.shape[0], min(MOVE_TILE, x2.shape[0]))
            for x2, route in zip(xs2, routes)]


def _encoder(groups, layers, ffn, moe, routers, final_g):
    shapes = [x.shape for x, _ in groups]
    xs2 = [x.reshape(-1, D_MODEL) for x, _ in groups]
    depth = len(layers)
    normed = False
    for l, lw in enumerate(layers):
        xs2 = [_mixer_and_cross(x2, mem, lw, shp[0], shp[1]) for x2, (_, mem), shp in zip(xs2, groups, shapes)]
        if l % 2 == 0:
            wg, wu, wd = ffn[l // 2]
            xs2 = [_swiglu(x2, lw['g_ffn'], wg, wu, wd, min(FFN_TILE, x2.shape[0])) for x2 in xs2]
        else:
            wg, wu, wd = moe[l // 2]
            normed = l == depth - 1
            xs2 = _routed_swiglu(xs2, lw['g_ffn'], routers[l // 2], wg, wu, wd, final_g, final_norm=normed)
    assert normed, "the final rmsnorm is fused into the last layer's routed SwiGLU"
    return tuple(x2.reshape(shp) for x2, shp in zip(xs2, shapes))


def kernel(x_prompt, x_sample, mem_prompt, mem_sample, norm_mix_g, w_in, sink_a, mla_q_norm_g, mla_kv_norm_g, mla_w_uq, mla_w_ukv, mix_out_norm_g, w_out, norm_x_g, norm_mem_g, w_xq, w_xkv, w_xo, norm_ffn_g, ffn_w_gu, ffn_w_down, moe_router, moe_w_gu, moe_w_down, final_norm_g):
    p = dict(norm_mix_g=norm_mix_g, w_in=w_in, sink_a=sink_a, mla_q_norm_g=mla_q_norm_g, mla_kv_norm_g=mla_kv_norm_g,
             mla_w_uq=mla_w_uq, mla_w_ukv=mla_w_ukv, mix_out_norm_g=mix_out_norm_g, w_out=w_out, norm_x_g=norm_x_g,
             norm_mem_g=norm_mem_g, w_xq=w_xq, w_xkv=w_xkv, w_xo=w_xo, norm_ffn_g=norm_ffn_g)
    depth = w_in.shape[0]
    layers = [_prep_layer(l, p) for l in range(depth)]
    ffn = [_prep_ffn(ffn_w_gu[i], ffn_w_down[i]) for i in range(ffn_w_gu.shape[0])]
    moe = [_prep_ffn(moe_w_gu[i], moe_w_down[i]) for i in range(moe_w_gu.shape[0])]
    routers = [jnp.pad(moe_router[i].astype(F32), ((0, 0), (0, LANES - N_EXPERTS))) for i in range(moe_router.shape[0])]
    final_g = final_norm_g.reshape(1, -1).astype(F32)
    return _encoder([(x_prompt, mem_prompt), (x_sample, mem_sample)], layers, ffn, moe, routers, final_g)
```

```python
import functools
import math

import numpy as np
import jax
import jax.numpy as jnp
from jax import lax
from jax.experimental import pallas as pl
from jax.experimental.pallas import tpu as pltpu

D_MODEL = 1024
HEAD_DIM = 64
A_Q_HEADS = 4
A_KV_HEADS = 2
A_WINDOW = 128
B_HEADS = 6
B_NOPE = 64
B_ROPE = 32
B_V = 64
B_Q_LORA = 384
B_KV_LORA = 256
ROPE_THETA = 10000.0
C_HEADS = 6
DILATED_PAIRS = ((128, 1), (512, 4), (2048, 16))
STRIDES = tuple(r for _, r in DILATED_PAIRS)
A_WIDTH = A_Q_HEADS * HEAD_DIM
B_WIDTH = B_HEADS * B_V
C_WIDTH = C_HEADS * HEAD_DIM
IN_SIZES = (A_WIDTH, A_KV_HEADS * HEAD_DIM, A_KV_HEADS * HEAD_DIM, B_Q_LORA, B_KV_LORA, B_ROPE,
            C_WIDTH, C_WIDTH, C_WIDTH)
X_HEADS = 4
X_HEAD_DIM = 128
N_EXPERTS = 8
TOP_K = 2
EPS = 1e-6
NEG = -1e30

LANES = 128
HALF = LANES // 2
LSE_LANES = 8
VMEM_LIMIT = 56 * 1024 * 1024
LOG2E = math.log2(math.e)

BF16 = jnp.bfloat16
F32 = jnp.float32

A_HEAD_ORDER = (0, 2, 1, 3)


def _params(*sem):
    return pltpu.CompilerParams(dimension_semantics=sem, vmem_limit_bytes=VMEM_LIMIT)


def _rms(x, g):
    return x * lax.rsqrt(jnp.mean(x * x, axis=-1, keepdims=True) + EPS) * g


def _dot(a, b):
    return jnp.dot(a, b, preferred_element_type=F32)


def _dot_nt(a, b):
    return lax.dot_general(a, b, (((1,), (1,)), ((), ())), preferred_element_type=F32)


def _const_spec(shape):
    n = len(shape)
    return pl.BlockSpec(shape, lambda *_: (0,) * n)


def _phase_spec(r, tm, nl, width):
    return pl.BlockSpec((None, r, tm // r, width), lambda i: (i // nl, 0, i % nl, 0))


def _in_kernel(*refs):
    it = iter(refs)
    x_ref, g_ref, wa_ref, wc_ref, wb_ref, wkr_ref, gq_ref, gkv_ref, wuq_ref, wukv_ref = [next(it) for _ in range(10)]
    cq_ref, sq_ref, ck_ref, sk_ref = [next(it) for _ in range(4)]
    qa_ref, ka_ref, va_ref = next(it), next(it), next(it)
    c_refs = [(next(it), next(it), next(it)) for _ in STRIDES]
    qb_ref, kb_ref, vb_ref = next(it), next(it), next(it)
    zc_scr = next(it)
    tm = x_ref.shape[0]

    h = _rms(x_ref[...], g_ref[...]).astype(BF16)
    za = _dot(h, wa_ref[...])
    qa_ref[...] = za[:, :A_WIDTH].astype(BF16)
    ka_ref[...] = za[:, A_WIDTH:A_WIDTH + LANES].astype(BF16)
    va_ref[...] = za[:, A_WIDTH + LANES:].astype(BF16)

    zc = _dot(h, wc_ref[...])
    ngrp = C_WIDTH // LANES
    for g in range(3 * ngrp):
        zc_scr[g] = zc[:, g * LANES:(g + 1) * LANES]
    for r, qkv_refs in zip(STRIDES, c_refs):
        for j in range(r):
            for n, ref in enumerate(qkv_refs):
                ref[j] = jnp.concatenate(
                    [zc_scr[n * ngrp + g, pl.ds(j, tm // r, stride=r), :] for g in range(ngrp)], axis=-1).astype(BF16)

    zb = _dot(h, wb_ref[...])
    hq = _rms(zb[:, :B_Q_LORA], gq_ref[...]).astype(BF16)
    hkv = _rms(zb[:, B_Q_LORA:], gkv_ref[...]).astype(BF16)
    zq = _dot(hq, wuq_ref[...])
    zkv = _dot(hkv, wukv_ref[...])
    zkr = _dot(h, wkr_ref[...])
    kr = zkr[:, :LANES] * ck_ref[...] + zkr[:, LANES:] * sk_ref[...]
    cq = cq_ref[...]
    sq = sq_ref[...]
    lane = lax.broadcasted_iota(jnp.int32, (1, LANES), 1)
    nb = B_HEADS * LANES
    for hd in range(B_HEADS):
        lo, hi = hd * LANES, (hd + 1) * LANES
        qb_ref[hd] = (zq[:, lo:hi] * cq + zq[:, nb + lo:nb + hi] * sq).astype(BF16)
        kb_ref[hd] = (zkv[:, lo:hi] + kr).astype(BF16)
        ones = (lane == (HALF if hd % 2 == 0 else 0)).astype(F32)
        vb_ref[hd] = (zkv[:, nb + lo:nb + hi] + ones).astype(BF16)


def _in_proj(x2, lw, tabs, bsz, L, tm):
    T = x2.shape[0]
    nl = L // tm
    row = lambda w: pl.BlockSpec((tm, w), lambda i: (i, 0))
    tab = pl.BlockSpec((tm, LANES), lambda i: (i % nl, 0))
    hm = pl.BlockSpec((B_HEADS, tm, LANES), lambda i: (0, i, 0))
    weights = (lw['g_mix'], lw['wa'], lw['wc'], lw['wb'], lw['wkr'], lw['g_q'], lw['g_kv'], lw['wuq'], lw['wukv'])
    out_specs = [row(A_WIDTH), row(LANES), row(LANES)]
    out_shape = [jax.ShapeDtypeStruct((T, w), BF16) for w in (A_WIDTH, LANES, LANES)]
    for r in STRIDES:
        assert tm % (16 * r) == 0, (tm, r)
        out_specs += [_phase_spec(r, tm, nl, C_WIDTH)] * 3
        out_shape += [jax.ShapeDtypeStruct((bsz, r, L // r, C_WIDTH), BF16)] * 3
    out_specs += [hm, hm, hm]
    out_shape += [jax.ShapeDtypeStruct((B_HEADS, T, LANES), BF16)] * 3
    outs = pl.pallas_call(
        _in_kernel,
        grid=(T // tm,),
        in_specs=[row(D_MODEL)] + [_const_spec(w.shape) for w in weights] + [tab] * 4,
        out_specs=out_specs,
        out_shape=out_shape,
        scratch_shapes=[pltpu.VMEM((3 * C_WIDTH // LANES, tm, LANES), F32)],
        compiler_params=_params("parallel"),
        name="in_proj",
    )(x2, *weights, *tabs)
    qkv_c = [outs[3 + 3 * n:6 + 3 * n] for n in range(len(STRIDES))]
    return outs[:3], qkv_c, outs[-3:]


def _banded_kernel(*refs, W, Q, R, n_kg, G, has_sink, want_lse, nchunks):
    it = iter(refs)
    q_ref = next(it)
    kp_ref, km_ref, kn_ref = next(it), next(it), next(it)
    vp_ref, vm_ref, vn_ref = next(it), next(it), next(it)
    bias_ref = next(it)
    sink_ref = next(it) if has_sink else None
    o_ref = next(it)
    lse_ref = next(it) if want_lse else None
    kfull, vfull = next(it), next(it)

    c = pl.program_id(2)
    P = q_ref.shape[0]
    kfull[:, 0:W] = kp_ref[...]
    kfull[:, W:W + R] = km_ref[...]
    kfull[:, W + R:] = kn_ref[...]
    vfull[:, 0:W] = vp_ref[...]
    vfull[:, W:W + R] = vm_ref[...]
    vfull[:, W + R:] = vn_ref[...]

    nsub = R // Q
    win = Q + 2 * W
    lo = lax.broadcasted_iota(jnp.int32, (Q, LANES), 1) < HALF
    col = lax.broadcasted_iota(jnp.int32, (2 * Q, win), 1)
    top = lax.broadcasted_iota(jnp.int32, (2 * Q, 1), 0) < Q
    lane8 = lax.broadcasted_iota(jnp.int32, (Q, LSE_LANES), 1)
    before_start = jnp.logical_and(c == 0, col < W)
    after_end = jnp.logical_and(c == nchunks - 1, col >= Q + W)
    units = [(ph, i, kg, g) for ph in range(P) for i in range(nsub) for kg in range(n_kg) for g in range(G)]

    scores = []
    for ph, i, kg, g in units:
        qg = kg * G + g
        qblk = q_ref[ph, i * Q:(i + 1) * Q, qg * LANES:(qg + 1) * LANES]
        zero = jnp.zeros_like(qblk)
        qm = jnp.concatenate([jnp.where(lo, qblk, zero), jnp.where(lo, zero, qblk)], axis=0)
        kwin = kfull[ph, i * Q:i * Q + win, kg * LANES:(kg + 1) * LANES]
        s = _dot_nt(qm, kwin) + bias_ref[qg]
        if i == 0:
            s = jnp.where(before_start, NEG, s)
        if i == nsub - 1:
            s = jnp.where(after_end, NEG, s)
        scores.append(s)

    probs = []
    for (ph, i, kg, g), s in zip(units, scores):
        qg = kg * G + g
        m = jnp.max(s, axis=-1, keepdims=True)
        if has_sink:
            sk = jnp.where(top, sink_ref[2 * qg], sink_ref[2 * qg + 1])
            m = jnp.maximum(m, sk)
        p = jnp.exp(s - m)
        den = jnp.sum(p, axis=-1, keepdims=True)
        if has_sink:
            den = den + jnp.exp(sk - m)
        probs.append((p.astype(BF16), den, m))

    lse_tiles = {(ph, i): jnp.zeros((Q, LSE_LANES), F32) for ph in range(P) for i in range(nsub)}
    for (ph, i, kg, g), (p, den, m) in zip(units, probs):
        qg = kg * G + g
        vwin = vfull[ph, i * Q:i * Q + win, kg * LANES:(kg + 1) * LANES]
        o = _dot(p, vwin) / den
        o_ref[ph, i * Q:(i + 1) * Q, qg * LANES:(qg + 1) * LANES] = jnp.where(lo, o[:Q], o[Q:]).astype(BF16)
        if want_lse:
            lse = m + jnp.log(den)
            tile = jnp.where(lane8 == 2 * qg, lse[:Q], lse_tiles[ph, i])
            lse_tiles[ph, i] = jnp.where(lane8 == 2 * qg + 1, lse[Q:], tile)
    if want_lse:
        for (ph, i), tile in lse_tiles.items():
            lse_ref[ph, i * Q:(i + 1) * Q, :] = tile


def _band_bias(slopes, step, W, Q):
    row = np.arange(Q)[:, None]
    col = np.arange(Q + 2 * W)[None, :]
    dist = np.abs(row + W - col)
    bias = -np.asarray(slopes, np.float32)[:, None, None] * (step * dist).astype(np.float32)[None]
    bias = np.where(dist[None] <= W, bias, np.float32(NEG)).astype(np.float32)
    return jnp.asarray(bias.reshape(len(slopes) // 2, 2 * Q, Q + 2 * W))


def _alibi_slopes(n):
    return (2.0 ** (-8.0 * np.arange(1, n + 1, dtype=np.float32) / n)).astype(np.float32)


BAND_ROWS = 512


def _band_tiles(Ls, W):
    R = min(Ls, BAND_ROWS)
    Q = min(R, 128)
    assert Ls % R == 0 and R % Q == 0 and R % W == 0 and Ls % W == 0, (Ls, R, Q, W)
    return R, Q


def _banded_attention(q, k, v, slopes, *, W, n_kg, G, sink=None, want_lse=False):
    bsz, r, Ls, Cq = q.shape
    Ck = k.shape[-1]
    R, Q = _band_tiles(Ls, W)
    nchunks = Ls // R
    nblk = Ls // W
    per = R // W
    P = math.gcd(r, max(1, BAND_ROWS // R))
    bias = _band_bias(slopes, r, W, Q)
    q_spec = pl.BlockSpec((None, P, R, Cq), lambda b, j, c: (b, j, c, 0))
    main = pl.BlockSpec((None, P, R, Ck), lambda b, j, c: (b, j, c, 0))
    prev = pl.BlockSpec((None, P, W, Ck), lambda b, j, c: (b, j, jnp.maximum(c * per - 1, 0), 0))
    nxt = pl.BlockSpec((None, P, W, Ck), lambda b, j, c: (b, j, jnp.minimum((c + 1) * per, nblk - 1), 0))
    in_specs = [q_spec, prev, main, nxt, prev, main, nxt, _const_spec(bias.shape)]
    args = [q, k, k, k, v, v, v, bias]
    if sink is not None:
        in_specs.append(pl.BlockSpec(memory_space=pltpu.SMEM))
        args.append(sink)
    out_specs = [q_spec]
    out_shape = [jax.ShapeDtypeStruct(q.shape, BF16)]
    if want_lse:
        out_specs.append(pl.BlockSpec((None, P, R, LSE_LANES), lambda b, j, c: (b, j, c, 0)))
        out_shape.append(jax.ShapeDtypeStruct((bsz, r, Ls, LSE_LANES), F32))
    kern = functools.partial(_banded_kernel, W=W, Q=Q, R=R, n_kg=n_kg, G=G, has_sink=sink is not None,
                             want_lse=want_lse, nchunks=nchunks)
    outs = pl.pallas_call(
        kern,
        grid=(bsz, r // P, nchunks),
        in_specs=in_specs,
        out_specs=out_specs,
        out_shape=out_shape,
        scratch_shapes=[pltpu.VMEM((P, R + 2 * W, Ck), BF16), pltpu.VMEM((P, R + 2 * W, Ck), BF16)],
        compiler_params=_params("parallel", "parallel", "parallel"),
        name=f"banded_w{W}_r{r}",
    )(*args)
    return (outs[0], outs[1]) if want_lse else outs[0]


def _flash_kernel(q_ref, k_ref, v_ref, o_ref, *, tk, nk):
    tq = q_ref.shape[1]
    qs = [q_ref[0], q_ref[1]]

    def body(t, carry):
        off = pl.multiple_of(t * tk, tk)
        ss = [_dot_nt(qs[h], k_ref[h, pl.ds(off, tk), :]) for h in range(2)]
        ms = [jnp.maximum(carry[2 * h], jnp.max(ss[h], axis=-1, keepdims=True)) for h in range(2)]
        ps = [jnp.exp2(ss[h] - ms[h]).astype(BF16) for h in range(2)]
        out = []
        for h in range(2):
            alpha = jnp.exp2(carry[2 * h] - ms[h])
            out += [ms[h], carry[2 * h + 1] * alpha + _dot(ps[h], v_ref[h, pl.ds(off, tk), :])]
        return tuple(out)

    m0 = jnp.full((tq, 1), NEG, F32)
    acc0 = jnp.zeros((tq, LANES), F32)
    res = lax.fori_loop(0, nk, body, (m0, acc0, m0, acc0), unroll=4)
    acc_even, acc_odd = res[1], res[3]
    out_even = acc_even / acc_even[:, HALF:HALF + 1]
    out_odd = acc_odd / acc_odd[:, 0:1]
    lo = lax.broadcasted_iota(jnp.int32, (tq, LANES), 1) < HALF
    o_ref[...] = jnp.where(lo, out_even, out_odd).astype(BF16)


def _flash_tiles(L):
    tq = min(L, 1024)
    tk = min(L, 512)
    assert L % tq == 0 and L % tk == 0
    return tq, tk


def _latent_attention(qb, kb, vb, bsz, L):
    tq, tk = _flash_tiles(L)
    q4 = qb.reshape(B_HEADS, bsz, L, LANES)
    k4 = kb.reshape(B_HEADS, bsz, L, LANES)
    v4 = vb.reshape(B_HEADS, bsz, L, LANES)
    kv_spec = pl.BlockSpec((2, None, L, LANES), lambda b, hp, i: (hp, b, 0, 0))
    out = pl.pallas_call(
        functools.partial(_flash_kernel, tk=tk, nk=L // tk),
        grid=(bsz, B_HEADS // 2, L // tq),
        in_specs=[pl.BlockSpec((2, None, tq, LANES), lambda b, hp, i: (hp, b, i, 0)), kv_spec, kv_spec],
        out_specs=pl.BlockSpec((None, tq, LANES), lambda b, hp, i: (b, i, hp)),
        out_shape=jax.ShapeDtypeStruct((bsz, L, B_WIDTH), BF16),
        compiler_params=_params("parallel", "parallel", "parallel"),
        name="latent_flash",
    )(q4, k4, v4)
    return out.reshape(bsz * L, B_WIDTH)


def _mix_out_kernel(*refs):
    it = iter(refs)
    x_ref, ya_ref, yb_ref = next(it), next(it), next(it)
    c_refs = [next(it) for _ in STRIDES]
    l_refs = [next(it) for _ in STRIDES]
    ga_ref, gb_ref, gc_ref, w_ref = next(it), next(it), next(it), next(it)
    o_ref = next(it)
    c_scr = next(it)
    tm = x_ref.shape[0]
    ngrp = C_WIDTH // LANES

    for n, r in enumerate(STRIDES):
        for j in range(r):
            blk = c_refs[n][j].astype(F32)
            for g in range(ngrp):
                c_scr[n * ngrp + g, pl.ds(j, tm // r, stride=r), :] = blk[:, g * LANES:(g + 1) * LANES]

    ls = [l_ref[...] for l_ref in l_refs]
    mx = functools.reduce(jnp.maximum, ls)
    es = [jnp.exp(l - mx) for l in ls]
    tot = functools.reduce(lambda a, b: a + b, es)
    head_of_lane = lax.broadcasted_iota(jnp.int32, (LSE_LANES, C_WIDTH), 1) // HEAD_DIM
    spread = (head_of_lane == lax.broadcasted_iota(jnp.int32, (LSE_LANES, C_WIDTH), 0)).astype(BF16)
    wide = []
    for e in es:
        w = e / tot
        hi = w.astype(BF16)
        lo = (w - hi.astype(F32)).astype(BF16)
        wide.append(_dot(hi, spread) + _dot(lo, spread))
    groups = []
    for kg in range(ngrp):
        acc = jnp.zeros((tm, LANES), F32)
        for n, w in enumerate(wide):
            acc = acc + c_scr[n * ngrp + kg] * w[:, kg * LANES:(kg + 1) * LANES]
        groups.append(acc)
    yc = jnp.concatenate(groups, axis=-1)
    y = jnp.concatenate([
        _rms(ya_ref[...].astype(F32), ga_ref[...]).astype(BF16),
        _rms(yb_ref[...].astype(F32), gb_ref[...]).astype(BF16),
        _rms(yc, gc_ref[...]).astype(BF16)], axis=-1)
    o_ref[...] = x_ref[...] + _dot(y, w_ref[...])


def _mix_out(x2, ya, yb, ycs, lses, lw, L, tm):
    T = x2.shape[0]
    nl = L // tm
    row = lambda w: pl.BlockSpec((tm, w), lambda i: (i, 0))
    weights = (lw['g_out_a'], lw['g_out_b'], lw['g_out_c'], lw['w_out'])
    n = len(STRIDES)
    return pl.pallas_call(
        _mix_out_kernel,
        grid=(T // tm,),
        in_specs=([row(D_MODEL), row(A_WIDTH), row(B_WIDTH)]
                  + [_phase_spec(r, tm, nl, C_WIDTH) for r in STRIDES]
                  + [row(LSE_LANES)] * n
                  + [_const_spec(w.shape) for w in weights]),
        out_specs=row(D_MODEL),
        out_shape=jax.ShapeDtypeStruct((T, D_MODEL), F32),
        scratch_shapes=[pltpu.VMEM((n * C_WIDTH // LANES, tm, LANES), F32)],
        compiler_params=_params("parallel"),
        name="mix_out",
    )(x2, ya, yb, *ycs, *lses, *weights)


def _norm_matmul_kernel(x_ref, g_ref, w_ref, o_ref):
    o_ref[...] = _dot(_rms(x_ref[...], g_ref[...]).astype(BF16), w_ref[...]).astype(o_ref.dtype)


def _norm_matmul(x2, g, w, tm):
    T, K = x2.shape
    N = w.shape[1]
    return pl.pallas_call(
        _norm_matmul_kernel,
        grid=(T // tm,),
        in_specs=[pl.BlockSpec((tm, K), lambda i: (i, 0)), _const_spec(g.shape), _const_spec(w.shape)],
        out_specs=pl.BlockSpec((tm, N), lambda i: (i, 0)),
        out_shape=jax.ShapeDtypeStruct((T, N), BF16),
        compiler_params=_params("parallel"),
        name="norm_matmul",
    )(x2, g, w)


def _cross_kernel(x_ref, kv_ref, g_ref, wq_ref, wo_ref, o_ref):
    x = x_ref[...]
    h = _rms(x, g_ref[...]).astype(BF16)
    q = (_dot(h, wq_ref[...]) * (X_HEAD_DIM ** -0.5)).astype(BF16)
    xd = X_HEADS * X_HEAD_DIM
    outs = []
    for hd in range(X_HEADS):
        sl = slice(hd * X_HEAD_DIM, (hd + 1) * X_HEAD_DIM)
        k = kv_ref[:, sl]
        v = kv_ref[:, xd + hd * X_HEAD_DIM:xd + (hd + 1) * X_HEAD_DIM]
        s = _dot_nt(q[:, sl], k)
        p = jnp.exp(s - jnp.max(s, axis=-1, keepdims=True))
        den = jnp.sum(p, axis=-1, keepdims=True)
        outs.append((_dot(p.astype(BF16), v) / den).astype(BF16))
    o_ref[...] = x + _dot(jnp.concatenate(outs, axis=-1), wo_ref[...])


def _cross_attention(x2, kvm, lw, L, tm):
    T = x2.shape[0]
    nl = L // tm
    M = kvm.shape[1]
    xd = X_HEADS * X_HEAD_DIM
    weights = (lw['g_x'], lw['w_xq'], lw['w_xo'])
    return pl.pallas_call(
        _cross_kernel,
        grid=(T // tm,),
        in_specs=[pl.BlockSpec((tm, D_MODEL), lambda i: (i, 0)),
                  pl.BlockSpec((None, M, 2 * xd), lambda i: (i // nl, 0, 0))]
                 + [_const_spec(w.shape) for w in weights],
        out_specs=pl.BlockSpec((tm, D_MODEL), lambda i: (i, 0)),
        out_shape=jax.ShapeDtypeStruct((T, D_MODEL), F32),
        compiler_params=_params("parallel"),
        name="cross_attention",
    )(x2, kvm, *weights)


def _silu_mul(gate, up):
    return gate * jax.nn.sigmoid(gate) * up


def _swiglu_kernel(x_ref, g_ref, wg_ref, wu_ref, wd_ref, o_ref, h_scr, acc_scr):
    j = pl.program_id(1)

    @pl.when(j == 0)
    def _():
        h_scr[...] = _rms(x_ref[...], g_ref[...]).astype(BF16)
        acc_scr[...] = jnp.zeros_like(acc_scr)

    h = h_scr[...]
    act = _silu_mul(_dot(h, wg_ref[...]), _dot(h, wu_ref[...])).astype(BF16)
    acc_scr[...] += _dot(act, wd_ref[...])

    @pl.when(j == pl.num_programs(1) - 1)
    def _():
        o_ref[...] = x_ref[...] + acc_scr[...]


def _ff_chunk(dff, limit):
    for c in (1792, 1408, 1024, 512, 256, 128):
        if c <= limit and dff % c == 0:
            return c
    raise ValueError(dff)


def _swiglu(x2, g, wg, wu, wd, tm):
    T = x2.shape[0]
    dff = wg.shape[1]
    ck = _ff_chunk(dff, 1408)
    return pl.pallas_call(
        _swiglu_kernel,
        grid=(T // tm, dff // ck),
        in_specs=[pl.BlockSpec((tm, D_MODEL), lambda i, j: (i, 0)),
                  _const_spec(g.shape),
                  pl.BlockSpec((D_MODEL, ck), lambda i, j: (0, j)),
                  pl.BlockSpec((D_MODEL, ck), lambda i, j: (0, j)),
                  pl.BlockSpec((ck, D_MODEL), lambda i, j: (j, 0))],
        out_specs=pl.BlockSpec((tm, D_MODEL), lambda i, j: (i, 0)),
        out_shape=jax.ShapeDtypeStruct((T, D_MODEL), F32),
        scratch_shapes=[pltpu.VMEM((tm, D_MODEL), BF16), pltpu.VMEM((tm, D_MODEL), F32)],
        compiler_params=_params("parallel", "arbitrary"),
        name="swiglu_dense",
    )(x2, g, wg, wu, wd)


def _router_logits(h, wr):
    return jnp.dot(h, wr, preferred_element_type=F32, precision=lax.Precision.HIGHEST)


def _router_kernel(x_ref, g_ref, wr_ref, tri_ref, c0_ref, route_ref, cnt_ref, run_scr):
    @pl.when(pl.program_id(0) == 0)
    def _():
        run_scr[...] = c0_ref[...]

    logits = _router_logits(_rms(x_ref[...], g_ref[...]), wr_ref[...])
    tm = logits.shape[0]
    lane = lax.broadcasted_iota(jnp.int32, (tm, LANES), 1)
    logits = jnp.where(lane < N_EXPERTS, logits, NEG)
    v1 = jnp.max(logits, axis=-1, keepdims=True)
    i1 = jnp.min(jnp.where(logits == v1, lane, LANES), axis=-1, keepdims=True)
    rest = jnp.where(lane == i1, NEG, logits)
    v2 = jnp.max(rest, axis=-1, keepdims=True)
    i2 = jnp.min(jnp.where(rest == v2, lane, LANES), axis=-1, keepdims=True)
    pid = jnp.minimum(i1, i2) * N_EXPERTS + jnp.maximum(i1, i2)
    onehot = (lane == pid).astype(F32)
    before = _dot(tri_ref[...], onehot.astype(BF16)) + run_scr[...]
    rank = jnp.sum(onehot * before, axis=-1, keepdims=True)
    run_scr[...] += jnp.sum(onehot, axis=0, keepdims=True)
    cnt_ref[...] = run_scr[...]
    lane8 = lax.broadcasted_iota(jnp.int32, (tm, LSE_LANES), 1)
    route_ref[...] = jnp.where(lane8 == 0, pid, jnp.where(lane8 == 1, rank.astype(jnp.int32), 0))


def _router(x2, g, wr, counts0, tm):
    T = x2.shape[0]
    tri = (np.arange(tm)[:, None] > np.arange(tm)[None, :]).astype(np.float32)
    tri = jnp.asarray(tri, BF16)
    return pl.pallas_call(
        _router_kernel,
        grid=(T // tm,),
        in_specs=[pl.BlockSpec((tm, D_MODEL), lambda i: (i, 0)), _const_spec(g.shape), _const_spec(wr.shape),
                  _const_spec(tri.shape), _const_spec(counts0.shape)],
        out_specs=[pl.BlockSpec((tm, LSE_LANES), lambda i: (i, 0)), _const_spec(counts0.shape)],
        out_shape=[jax.ShapeDtypeStruct((T, LSE_LANES), jnp.int32), jax.ShapeDtypeStruct(counts0.shape, F32)],
        scratch_shapes=[pltpu.VMEM(counts0.shape, F32)],
        compiler_params=_params("arbitrary"),
        name="router",
    )(x2, g, wr, tri, counts0)


ROW_SUBLANES = D_MODEL // LANES


def _slab_to_rows(ref, n):
    return jnp.concatenate([ref[pl.ds(c, n, stride=ROW_SUBLANES), :] for c in range(ROW_SUBLANES)], axis=-1)


def _rows_to_slab(ref, rows, n):
    for c in range(ROW_SUBLANES):
        ref[pl.ds(c, n, stride=ROW_SUBLANES), :] = rows[:, c * LANES:(c + 1) * LANES]


def _slab(ref, row):
    return ref.at[pl.ds(pl.multiple_of(row * ROW_SUBLANES, ROW_SUBLANES), ROW_SUBLANES)]


MOVE_UNROLL = 8


def _move_rows(n, copy_of_row, copy_of_all):
    def issue(r, _):
        copy_of_row(r).start()
        return 0

    lax.fori_loop(0, n, issue, 0, unroll=MOVE_UNROLL)
    copy_of_all().wait()


def _dispatch_kernel(route_ref, start_ref, x_ref, prev_ref, o_ref, slab, sem, *, tm):
    del prev_ref
    _rows_to_slab(slab, x_ref[...], tm)

    def copy_of_row(r):
        slot = start_ref[route_ref[0, r]] + route_ref[1, r]
        return pltpu.make_async_copy(_slab(slab, r), _slab(o_ref, slot), sem)

    def copy_of_all():
        return pltpu.make_async_copy(slab, o_ref.at[pl.ds(0, tm * ROW_SUBLANES)], sem)

    _move_rows(tm, copy_of_row, copy_of_all)


def _route_blocks(route, tm):
    T = route.shape[0]
    return route[:, :2].T.reshape(2, T // tm, tm).transpose(1, 0, 2)


def _dispatch(x2, route, start, sorted_rows, tm):
    T = x2.shape[0]
    return pl.pallas_call(
        functools.partial(_dispatch_kernel, tm=tm),
        grid=(T // tm,),
        in_specs=[pl.BlockSpec((None, 2, tm), lambda i: (i, 0, 0), memory_space=pltpu.SMEM),
                  pl.BlockSpec(memory_space=pltpu.SMEM),
                  pl.BlockSpec((tm, D_MODEL), lambda i: (i, 0)),
                  pl.BlockSpec(memory_space=pl.ANY)],
        out_specs=pl.BlockSpec(memory_space=pl.ANY),
        out_shape=jax.ShapeDtypeStruct(sorted_rows.shape, sorted_rows.dtype),
        scratch_shapes=[pltpu.VMEM((tm * ROW_SUBLANES, LANES), F32), pltpu.SemaphoreType.DMA],
        input_output_aliases={3: 0},
        compiler_params=pltpu.CompilerParams(dimension_semantics=("arbitrary",), vmem_limit_bytes=VMEM_LIMIT,
                                             has_side_effects=True),
        name="dispatch",
    )(_route_blocks(route, tm), start, x2, sorted_rows)


def _unpermute_kernel(route_ref, start_ref, y_ref, o_ref, slab, sem, *, tm):
    def copy_of_row(r):
        slot = start_ref[route_ref[0, r]] + route_ref[1, r]
        return pltpu.make_async_copy(_slab(y_ref, slot), _slab(slab, r), sem)

    def copy_of_all():
        return pltpu.make_async_copy(y_ref.at[pl.ds(0, tm * ROW_SUBLANES)], slab, sem)

    _move_rows(tm, copy_of_row, copy_of_all)
    o_ref[...] = _slab_to_rows(slab, tm)


def _unpermute(y_sorted, route, start, T, tm):
    return pl.pallas_call(
        functools.partial(_unpermute_kernel, tm=tm),
        grid=(T // tm,),
        in_specs=[pl.BlockSpec((None, 2, tm), lambda i: (i, 0, 0), memory_space=pltpu.SMEM),
                  pl.BlockSpec(memory_space=pltpu.SMEM),
                  pl.BlockSpec(memory_space=pl.ANY)],
        out_specs=pl.BlockSpec((tm, D_MODEL), lambda i: (i, 0)),
        out_shape=jax.ShapeDtypeStruct((T, D_MODEL), F32),
        scratch_shapes=[pltpu.VMEM((tm * ROW_SUBLANES, LANES), F32), pltpu.SemaphoreType.DMA],
        compiler_params=_params("arbitrary"),
        name="unpermute",
    )(_route_blocks(route, tm), start, y_sorted)


def _pair_experts_kernel(ta_ref, tb_ref, nv_ref, xs_ref, g_ref, wr_ref, wg_ref, wu_ref, wd_ref, gf_ref, o_ref,
                         x_scr, h_scr, gate_scr, acc_scr, *, nj, final_norm):
    i = pl.program_id(0)
    j = pl.program_id(1)
    tm = x_scr.shape[0]

    @pl.when(i < nv_ref[0])
    def _():
        @pl.when(j == 0)
        def _():
            x = _slab_to_rows(xs_ref, tm)
            x_scr[...] = x
            h = _rms(x, g_ref[...])
            h_scr[...] = h.astype(BF16)
            logits = _router_logits(h, wr_ref[...])
            lane = lax.broadcasted_iota(jnp.int32, logits.shape, 1)
            la = jnp.sum(jnp.where(lane == ta_ref[i], logits, 0.0), axis=-1, keepdims=True)
            lb = jnp.sum(jnp.where(lane == tb_ref[i], logits, 0.0), axis=-1, keepdims=True)
            mx = jnp.maximum(la, lb)
            ea, eb = jnp.exp(la - mx), jnp.exp(lb - mx)
            gate_scr[0] = ea / (ea + eb)
            gate_scr[1] = eb / (ea + eb)
            acc_scr[...] = jnp.zeros_like(acc_scr)

        h = h_scr[...]
        gate = jnp.where(j < nj, gate_scr[0], gate_scr[1])
        act = (_silu_mul(_dot(h, wg_ref[...]), _dot(h, wu_ref[...])) * gate).astype(BF16)
        acc_scr[...] += _dot(act, wd_ref[...])

        @pl.when(j == 2 * nj - 1)
        def _():
            out = x_scr[...] + acc_scr[...]
            if final_norm:
                out = _rms(out, gf_ref[...])
            _rows_to_slab(o_ref, out, tm)

    @pl.when(jnp.logical_and(i >= nv_ref[0], j == 2 * nj - 1))
    def _():
        o_ref[...] = jnp.zeros_like(o_ref)


def _pair_experts(xs, g, wr, wg, wu, wd, g_final, tile_a, tile_b, n_valid, tm, final_norm):
    rows = xs.shape[0] // ROW_SUBLANES
    dff = wg.shape[2]
    ck = _ff_chunk(dff, 1792)
    nj = dff // ck

    def expert(i, j, ta, tb):
        return jnp.where(j < nj, ta[i], tb[i])

    def chunk(i, j, nv):
        return jnp.where(i < nv[0], j % nj, nj - 1)

    slab_spec = pl.BlockSpec((tm * ROW_SUBLANES, LANES), lambda i, j, ta, tb, nv: (i, 0))
    const = lambda a: pl.BlockSpec(a.shape, lambda i, j, ta, tb, nv: (0,) * a.ndim)
    grid_spec = pltpu.PrefetchScalarGridSpec(
        num_scalar_prefetch=3,
        grid=(rows // tm, 2 * nj),
        in_specs=[slab_spec, const(g), const(wr),
                  pl.BlockSpec((None, D_MODEL, ck), lambda i, j, ta, tb, nv: (expert(i, j, ta, tb), 0, chunk(i, j, nv))),
                  pl.BlockSpec((None, D_MODEL, ck), lambda i, j, ta, tb, nv: (expert(i, j, ta, tb), 0, chunk(i, j, nv))),
                  pl.BlockSpec((None, ck, D_MODEL), lambda i, j, ta, tb, nv: (expert(i, j, ta, tb), chunk(i, j, nv), 0)),
                  const(g_final)],
        out_specs=slab_spec,
        scratch_shapes=[pltpu.VMEM((tm, D_MODEL), F32), pltpu.VMEM((tm, D_MODEL), BF16),
                        pltpu.VMEM((2, tm, 1), F32), pltpu.VMEM((tm, D_MODEL), F32)],
    )
    return pl.pallas_call(
        functools.partial(_pair_experts_kernel, nj=nj, final_norm=final_norm),
        grid_spec=grid_spec,
        out_shape=jax.ShapeDtypeStruct(xs.shape, F32),
        compiler_params=_params("parallel", "arbitrary"),
        name="pair_experts",
    )(tile_a, tile_b, n_valid, xs, g, wr, wg, wu, wd, g_final)


def _plan_pairs(counts, n_tiles, tm):
    counts = counts.reshape(-1).astype(jnp.int32)
    tiles = (counts + tm - 1) // tm
    tile_end = jnp.cumsum(tiles)
    start = (tile_end - tiles) * tm
    n_valid = tile_end[-1:]
    tile_ids = jnp.minimum(jnp.arange(n_tiles), n_valid[0] - 1)
    tile_pid = jnp.sum(tile_ids[:, None] >= tile_end[None, :], axis=-1)
    tile_a = (tile_pid // N_EXPERTS).astype(jnp.int32)
    tile_b = (tile_pid % N_EXPERTS).astype(jnp.int32)
    return start.astype(jnp.int32), tile_a, tile_b, n_valid.astype(jnp.int32)


def _prep_layer(l, p):
    offs = np.cumsum((0,) + IN_SIZES)
    w_in = p['w_in'][l]
    qa, ka, va, cq, ckv, kr, qc, kc, vc = [w_in[:, int(offs[i]):int(offs[i + 1])] for i in range(9)]
    scale = HEAD_DIM ** -0.5
    order = list(A_HEAD_ORDER)
    qa = qa.reshape(D_MODEL, A_Q_HEADS, HEAD_DIM)[:, order, :].reshape(D_MODEL, A_WIDTH) * scale
    half = B_ROPE // 2
    z = lambda *s: jnp.zeros(s, F32)
    kr_pad = jnp.concatenate([z(D_MODEL, B_NOPE), kr, z(D_MODEL, LANES - B_NOPE - B_ROPE)], axis=1)
    kr_rot = jnp.concatenate([z(D_MODEL, B_NOPE), -kr[:, half:], kr[:, :half], z(D_MODEL, LANES - B_NOPE - B_ROPE)], axis=1)
    uq = p['mla_w_uq'][l].reshape(B_Q_LORA, B_HEADS, B_NOPE + B_ROPE)
    padq = z(B_Q_LORA, B_HEADS, LANES - B_NOPE - B_ROPE)
    uq_pad = jnp.concatenate([uq, padq], axis=-1).reshape(B_Q_LORA, B_HEADS * LANES)
    uq_rot = jnp.concatenate([z(B_Q_LORA, B_HEADS, B_NOPE), -uq[..., B_NOPE + half:], uq[..., B_NOPE:B_NOPE + half], padq],
                             axis=-1).reshape(B_Q_LORA, B_HEADS * LANES)
    ukv = p['mla_w_ukv'][l].reshape(B_KV_LORA, B_HEADS, B_NOPE + B_V)
    zk = z(B_KV_LORA, B_HEADS, HALF)
    uk_pad = jnp.concatenate([ukv[..., :B_NOPE], zk], axis=-1).reshape(B_KV_LORA, B_HEADS * LANES)
    uv = ukv[..., B_NOPE:]
    odd = (jnp.arange(B_HEADS) % 2 == 1)[None, :, None]
    uv_pad = jnp.where(odd, jnp.concatenate([zk, uv], axis=-1), jnp.concatenate([uv, zk], axis=-1))
    uv_pad = uv_pad.reshape(B_KV_LORA, B_HEADS * LANES)

    g_out = p['mix_out_norm_g'][l]
    w_out = p['w_out'][l]
    a_rows = np.concatenate([np.arange(HEAD_DIM) + HEAD_DIM * h for h in A_HEAD_ORDER])
    w_out = jnp.concatenate([w_out[:A_WIDTH][a_rows], w_out[A_WIDTH:]], axis=0)
    g_out_a = g_out[:A_WIDTH][a_rows]
    row = lambda v: v.reshape(1, -1).astype(F32)
    return {
        'g_mix': row(p['norm_mix_g'][l]),
        'wa': jnp.concatenate([qa, ka, va], axis=1).astype(BF16),
        'wc': jnp.concatenate([qc * scale, kc, vc], axis=1).astype(BF16),
        'wb': jnp.concatenate([cq, ckv], axis=1).astype(BF16),
        'wkr': jnp.concatenate([kr_pad, kr_rot], axis=1).astype(BF16),
        'g_q': row(p['mla_q_norm_g'][l]),
        'g_kv': row(p['mla_kv_norm_g'][l]),
        'wuq': jnp.concatenate([uq_pad, uq_rot], axis=1).astype(BF16),
        'wukv': jnp.concatenate([uk_pad, uv_pad], axis=1).astype(BF16),
        'sink': p['sink_a'][l][np.array(A_HEAD_ORDER)].astype(F32),
        'g_out_a': row(g_out_a),
        'g_out_b': row(g_out[A_WIDTH:A_WIDTH + B_WIDTH]),
        'g_out_c': row(g_out[A_WIDTH + B_WIDTH:]),
        'w_out': w_out.astype(BF16),
        'g_x': row(p['norm_x_g'][l]),
        'g_mem': row(p['norm_mem_g'][l]),
        'w_xq': p['w_xq'][l].astype(BF16),
        'w_xkv': p['w_xkv'][l].astype(BF16),
        'w_xo': p['w_xo'][l].astype(BF16),
        'g_ffn': row(p['norm_ffn_g'][l]),
    }


def _prep_ffn(w_gu, w_down):
    dff = w_down.shape[-2]
    return w_gu[..., :dff].astype(BF16), w_gu[..., dff:].astype(BF16), w_down.astype(BF16)


def _rope_tables(L):
    half = B_ROPE // 2
    pos = jnp.arange(L, dtype=F32)
    inv = ROPE_THETA ** (-jnp.arange(half, dtype=F32) / half)
    ang = pos[:, None] * inv[None, :]
    cos2 = jnp.tile(jnp.cos(ang), (1, 2))
    sin2 = jnp.tile(jnp.sin(ang), (1, 2))
    pad = jnp.zeros((L, LANES - B_NOPE - B_ROPE), F32)
    qscale = (B_NOPE + B_ROPE) ** -0.5 * LOG2E
    cq = jnp.concatenate([jnp.ones((L, B_NOPE), F32), cos2, pad], axis=1) * qscale
    sq = jnp.concatenate([jnp.zeros((L, B_NOPE), F32), sin2, pad], axis=1) * qscale
    ck = jnp.concatenate([jnp.zeros((L, B_NOPE), F32), cos2, pad], axis=1)
    sk = jnp.concatenate([jnp.zeros((L, B_NOPE), F32), sin2, pad], axis=1)
    return cq, sq, ck, sk


TOKEN_TILE = 512
EXPERT_TILE = 512
FFN_TILE = 1024
MOVE_TILE = 1024
N_PAIRS_USED = N_EXPERTS * (N_EXPERTS - 1) // 2


def _mixer_and_cross(x2, mem, lw, bsz, L):
    T = bsz * L
    tm = min(TOKEN_TILE, L)
    tabs = _rope_tables(L)
    slopes_a = _alibi_slopes(A_Q_HEADS)[list(A_HEAD_ORDER)]
    slopes_c = _alibi_slopes(C_HEADS)
    (qa, ka, va), qkv_c, (qb, kb, vb) = _in_proj(x2, lw, tabs, bsz, L, tm)
    as_seq = lambda t: t.reshape(bsz, 1, L, t.shape[-1])
    ya = _banded_attention(as_seq(qa), as_seq(ka), as_seq(va), slopes_a, W=A_WINDOW, n_kg=1, G=2, sink=lw['sink'])
    ya = ya.reshape(T, A_WIDTH)
    yb = _latent_attention(qb, kb, vb, bsz, L)
    ycs, lses = [], []
    for (w, r), (qc, kc, vc) in zip(DILATED_PAIRS, qkv_c):
        o, lse = _banded_attention(qc, kc, vc, slopes_c, W=w // (2 * r), n_kg=C_HEADS // 2, G=1, want_lse=True)
        ycs.append(o)
        lses.append(jnp.transpose(lse, (0, 2, 1, 3)).reshape(T, LSE_LANES))
    x2 = _mix_out(x2, ya, yb, ycs, lses, lw, L, tm)
    mem2 = mem.reshape(-1, D_MODEL)
    kvm = _norm_matmul(mem2, lw['g_mem'], lw['w_xkv'], min(512, mem2.shape[0]))
    return _cross_attention(x2, kvm.reshape(bsz, mem.shape[1], -1), lw, L, tm)


def _routed_swiglu(xs2, g, wr, wg, wu, wd, final_g, final_norm):
    total = sum(x.shape[0] for x in xs2)
    et = min(EXPERT_TILE, total)
    n_tiles = total // et + N_PAIRS_USED
    counts = jnp.zeros((1, LANES), F32)
    routes = []
    for x2 in xs2:
        route, counts = _router(x2, g, wr, counts, min(TOKEN_TILE, x2.shape[0]))
        routes.append(route)
    start, tile_a, tile_b, n_valid = _plan_pairs(counts, n_tiles, et)
    sorted_rows = jnp.zeros((n_tiles * et * ROW_SUBLANES, LANES), F32)
    for x2, route in zip(xs2, routes):
        sorted_rows = _dispatch(x2, route, start, sorted_rows, min(MOVE_TILE, x2.shape[0]))
    y_sorted = _pair_experts(sorted_rows, g, wr, wg, wu, wd, final_g, tile_a, tile_b, n_valid, et, final_norm)
    return [_unpermute(y_sorted, route, start, x2.shape[0], min(MOVE_TILE, x2.shape[0]))
            for x2, route in zip(xs2, routes)]


def _encoder(groups, layers, ffn, moe, routers, final_g):
    shapes = [x.shape for x, _ in groups]
    xs2 = [x.reshape(-1, D_MODEL) for x, _ in groups]
    depth = len(layers)
    normed = False
    for l, lw in enumerate(layers):
        xs2 = [_mixer_and_cross(x2, mem, lw, shp[0], shp[1]) for x2, (_, mem), shp in zip(xs2, groups, shapes)]
        if l % 2 == 0:
            wg, wu, wd = ffn[l // 2]
            xs2 = [_swiglu(x2, lw['g_ffn'], wg, wu, wd, min(FFN_TILE, x2.shape[0])) for x2 in xs2]
        else:
            wg, wu, wd = moe[l // 2]
            normed = l == depth - 1
            xs2 = _routed_swiglu(xs2, lw['g_ffn'], routers[l // 2], wg, wu, wd, final_g, final_norm=normed)
    assert normed, "the final rmsnorm is fused into the last layer's routed SwiGLU"
    return tuple(x2.reshape(shp) for x2, shp in zip(xs2, shapes))


def kernel(x_prompt, x_sample, mem_prompt, mem_sample, norm_mix_g, w_in, sink_a, mla_q_norm_g, mla_kv_norm_g, mla_w_uq, mla_w_ukv, mix_out_norm_g, w_out, norm_x_g, norm_mem_g, w_xq, w_xkv, w_xo, norm_ffn_g, ffn_w_gu, ffn_w_down, moe_router, moe_w_gu, moe_w_down, final_norm_g):
    p = dict(norm_mix_g=norm_mix_g, w_in=w_in, sink_a=sink_a, mla_q_norm_g=mla_q_norm_g, mla_kv_norm_g=mla_kv_norm_g,
             mla_w_uq=mla_w_uq, mla_w_ukv=mla_w_ukv, mix_out_norm_g=mix_out_norm_g, w_out=w_out, norm_x_g=norm_x_g,
             norm_mem_g=norm_mem_g, w_xq=w_xq, w_xkv=w_xkv, w_xo=w_xo, norm_ffn_g=norm_ffn_g)
    depth = w_in.shape[0]
    layers = [_prep_layer(l, p) for l in range(depth)]
    ffn = [_prep_ffn(ffn_w_gu[i], ffn_w_down[i]) for i in range(ffn_w_gu.shape[0])]
    moe = [_prep_ffn(moe_w_gu[i], moe_w_down[i]) for i in range(moe_w_gu.shape[0])]
    routers = [jnp.pad(moe_router[i].astype(F32), ((0, 0), (0, LANES - N_EXPERTS))) for i in range(moe_router.shape[0])]
    final_g = final_norm_g.reshape(1, -1).astype(F32)
    return _encoder([(x_prompt, mem_prompt), (x_sample, mem_sample)], layers, ffn, moe, routers, final_g)
```

```python
import functools
import math

import numpy as np
import jax
import jax.numpy as jnp
from jax import lax
from jax.experimental import pallas as pl
from jax.experimental.pallas import tpu as pltpu

D_MODEL = 1024
HEAD_DIM = 64
A_Q_HEADS = 4
A_KV_HEADS = 2
A_WINDOW = 128
B_HEADS = 6
B_NOPE = 64
B_ROPE = 32
B_V = 64
B_Q_LORA = 384
B_KV_LORA = 256
ROPE_THETA = 10000.0
C_HEADS = 6
DILATED_PAIRS = ((128, 1), (512, 4), (2048, 16))
STRIDES = tuple(r for _, r in DILATED_PAIRS)
A_WIDTH = A_Q_HEADS * HEAD_DIM
B_WIDTH = B_HEADS * B_V
C_WIDTH = C_HEADS * HEAD_DIM
IN_SIZES = (A_WIDTH, A_KV_HEADS * HEAD_DIM, A_KV_HEADS * HEAD_DIM, B_Q_LORA, B_KV_LORA, B_ROPE,
            C_WIDTH, C_WIDTH, C_WIDTH)
X_HEADS = 4
X_HEAD_DIM = 128
N_EXPERTS = 8
TOP_K = 2
EPS = 1e-6
NEG = -1e30

LANES = 128
HALF = LANES // 2
LSE_LANES = 8
VMEM_LIMIT = 56 * 1024 * 1024
LOG2E = math.log2(math.e)

BF16 = jnp.bfloat16
F32 = jnp.float32

A_HEAD_ORDER = (0, 2, 1, 3)


def _params(*sem):
    return pltpu.CompilerParams(dimension_semantics=sem, vmem_limit_bytes=VMEM_LIMIT)


def _rms(x, g):
    return x * lax.rsqrt(jnp.mean(x * x, axis=-1, keepdims=True) + EPS) * g


def _dot(a, b):
    return jnp.dot(a, b, preferred_element_type=F32)


def _dot_nt(a, b):
    return lax.dot_general(a, b, (((1,), (1,)), ((), ())), preferred_element_type=F32)


def _const_spec(shape):
    n = len(shape)
    return pl.BlockSpec(shape, lambda *_: (0,) * n)


def _phase_spec(r, tm, nl, width):
    return pl.BlockSpec((None, r, tm // r, width), lambda i: (i // nl, 0, i % nl, 0))


def _in_kernel(*refs):
    it = iter(refs)
    x_ref, g_ref, wa_ref, wc_ref, wb_ref, wkr_ref, gq_ref, gkv_ref, wuq_ref, wukv_ref = [next(it) for _ in range(10)]
    cq_ref, sq_ref, ck_ref, sk_ref = [next(it) for _ in range(4)]
    qa_ref, ka_ref, va_ref = next(it), next(it), next(it)
    c_refs = [(next(it), next(it), next(it)) for _ in STRIDES]
    qb_ref, kb_ref, vb_ref = next(it), next(it), next(it)
    zc_scr = next(it)
    tm = x_ref.shape[0]

    h = _rms(x_ref[...], g_ref[...]).astype(BF16)
    za = _dot(h, wa_ref[...])
    qa_ref[...] = za[:, :A_WIDTH].astype(BF16)
    ka_ref[...] = za[:, A_WIDTH:A_WIDTH + LANES].astype(BF16)
    va_ref[...] = za[:, A_WIDTH + LANES:].astype(BF16)

    zc = _dot(h, wc_ref[...])
    ngrp = C_WIDTH // LANES
    for g in range(3 * ngrp):
        zc_scr[g] = zc[:, g * LANES:(g + 1) * LANES]
    for r, qkv_refs in zip(STRIDES, c_refs):
        for j in range(r):
            for n, ref in enumerate(qkv_refs):
                ref[j] = jnp.concatenate(
                    [zc_scr[n * ngrp + g, pl.ds(j, tm // r, stride=r), :] for g in range(ngrp)], axis=-1).astype(BF16)

    zb = _dot(h, wb_ref[...])
    hq = _rms(zb[:, :B_Q_LORA], gq_ref[...]).astype(BF16)
    hkv = _rms(zb[:, B_Q_LORA:], gkv_ref[...]).astype(BF16)
    zq = _dot(hq, wuq_ref[...])
    zkv = _dot(hkv, wukv_ref[...])
    zkr = _dot(h, wkr_ref[...])
    kr = zkr[:, :LANES] * ck_ref[...] + zkr[:, LANES:] * sk_ref[...]
    cq = cq_ref[...]
    sq = sq_ref[...]
    lane = lax.broadcasted_iota(jnp.int32, (1, LANES), 1)
    nb = B_HEADS * LANES
    for hd in range(B_HEADS):
        lo, hi = hd * LANES, (hd + 1) * LANES
        qb_ref[hd] = (zq[:, lo:hi] * cq + zq[:, nb + lo:nb + hi] * sq).astype(BF16)
        kb_ref[hd] = (zkv[:, lo:hi] + kr).astype(BF16)
        ones = (lane == B_V).astype(F32)
        vb_ref[hd] = (zkv[:, nb + lo:nb + hi] + ones).T.astype(BF16)


def _in_proj(x2, lw, tabs, bsz, L, tm):
    T = x2.shape[0]
    nl = L // tm
    row = lambda w: pl.BlockSpec((tm, w), lambda i: (i, 0))
    tab = pl.BlockSpec((tm, LANES), lambda i: (i % nl, 0))
    hm = pl.BlockSpec((B_HEADS, tm, LANES), lambda i: (0, i, 0))
    weights = (lw['g_mix'], lw['wa'], lw['wc'], lw['wb'], lw['wkr'], lw['g_q'], lw['g_kv'], lw['wuq'], lw['wukv'])
    out_specs = [row(A_WIDTH), row(LANES), row(LANES)]
    out_shape = [jax.ShapeDtypeStruct((T, w), BF16) for w in (A_WIDTH, LANES, LANES)]
    for r in STRIDES:
        assert tm % (16 * r) == 0, (tm, r)
        out_specs += [_phase_spec(r, tm, nl, C_WIDTH)] * 3
        out_shape += [jax.ShapeDtypeStruct((bsz, r, L // r, C_WIDTH), BF16)] * 3
    out_specs += [hm, hm, pl.BlockSpec((B_HEADS, LANES, tm), lambda i: (0, 0, i))]
    out_shape += [jax.ShapeDtypeStruct((B_HEADS, T, LANES), BF16)] * 2
    out_shape += [jax.ShapeDtypeStruct((B_HEADS, LANES, T), BF16)]
    outs = pl.pallas_call(
        _in_kernel,
        grid=(T // tm,),
        in_specs=[row(D_MODEL)] + [_const_spec(w.shape) for w in weights] + [tab] * 4,
        out_specs=out_specs,
        out_shape=out_shape,
        scratch_shapes=[pltpu.VMEM((3 * C_WIDTH // LANES, tm, LANES), F32)],
        compiler_params=_params("parallel"),
        name="in_proj",
    )(x2, *weights, *tabs)
    qkv_c = [outs[3 + 3 * n:6 + 3 * n] for n in range(len(STRIDES))]
    return outs[:3], qkv_c, outs[-3:]


def _banded_kernel(*refs, W, Q, R, n_kg, G, has_sink, want_lse, nchunks):
    it = iter(refs)
    q_ref = next(it)
    kp_ref, km_ref, kn_ref = next(it), next(it), next(it)
    vp_ref, vm_ref, vn_ref = next(it), next(it), next(it)
    bias_ref = next(it)
    sink_ref = next(it) if has_sink else None
    o_ref = next(it)
    lse_ref = next(it) if want_lse else None
    kfull, vfull = next(it), next(it)

    c = pl.program_id(2)
    P = q_ref.shape[0]
    kfull[:, 0:W] = kp_ref[...]
    kfull[:, W:W + R] = km_ref[...]
    kfull[:, W + R:] = kn_ref[...]
    vfull[:, 0:W] = vp_ref[...]
    vfull[:, W:W + R] = vm_ref[...]
    vfull[:, W + R:] = vn_ref[...]

    nsub = R // Q
    win = Q + 2 * W
    lo = lax.broadcasted_iota(jnp.int32, (Q, LANES), 1) < HALF
    col = lax.broadcasted_iota(jnp.int32, (2 * Q, win), 1)
    top = lax.broadcasted_iota(jnp.int32, (2 * Q, 1), 0) < Q
    lane8 = lax.broadcasted_iota(jnp.int32, (Q, LSE_LANES), 1)
    before_start = jnp.logical_and(c == 0, col < W)
    after_end = jnp.logical_and(c == nchunks - 1, col >= Q + W)
    units = [(ph, i, kg, g) for ph in range(P) for i in range(nsub) for kg in range(n_kg) for g in range(G)]

    scores = []
    for ph, i, kg, g in units:
        qg = kg * G + g
        qblk = q_ref[ph, i * Q:(i + 1) * Q, qg * LANES:(qg + 1) * LANES]
        zero = jnp.zeros_like(qblk)
        qm = jnp.concatenate([jnp.where(lo, qblk, zero), jnp.where(lo, zero, qblk)], axis=0)
        kwin = kfull[ph, i * Q:i * Q + win, kg * LANES:(kg + 1) * LANES]
        s = _dot_nt(qm, kwin) + bias_ref[qg]
        if i == 0:
            s = jnp.where(before_start, NEG, s)
        if i == nsub - 1:
            s = jnp.where(after_end, NEG, s)
        scores.append(s)

    probs = []
    for (ph, i, kg, g), s in zip(units, scores):
        qg = kg * G + g
        m = jnp.max(s, axis=-1, keepdims=True)
        if has_sink:
            sk = jnp.where(top, sink_ref[2 * qg], sink_ref[2 * qg + 1])
            m = jnp.maximum(m, sk)
        p = jnp.exp(s - m)
        den = jnp.sum(p, axis=-1, keepdims=True)
        if has_sink:
            den = den + jnp.exp(sk - m)
        probs.append((p.astype(BF16), den, m))

    lse_tiles = {(ph, i): jnp.zeros((Q, LSE_LANES), F32) for ph in range(P) for i in range(nsub)}
    for (ph, i, kg, g), (p, den, m) in zip(units, probs):
        qg = kg * G + g
        vwin = vfull[ph, i * Q:i * Q + win, kg * LANES:(kg + 1) * LANES]
        o = _dot(p, vwin) / den
        o_ref[ph, i * Q:(i + 1) * Q, qg * LANES:(qg + 1) * LANES] = jnp.where(lo, o[:Q], o[Q:]).astype(BF16)
        if want_lse:
            lse = m + jnp.log(den)
            tile = jnp.where(lane8 == 2 * qg, lse[:Q], lse_tiles[ph, i])
            lse_tiles[ph, i] = jnp.where(lane8 == 2 * qg + 1, lse[Q:], tile)
    if want_lse:
        for (ph, i), tile in lse_tiles.items():
            lse_ref[ph, i * Q:(i + 1) * Q, :] = tile


def _band_bias(slopes, step, W, Q):
    row = np.arange(Q)[:, None]
    col = np.arange(Q + 2 * W)[None, :]
    dist = np.abs(row + W - col)
    bias = -np.asarray(slopes, np.float32)[:, None, None] * (step * dist).astype(np.float32)[None]
    bias = np.where(dist[None] <= W, bias, np.float32(NEG)).astype(np.float32)
    return jnp.asarray(bias.reshape(len(slopes) // 2, 2 * Q, Q + 2 * W))


def _alibi_slopes(n):
    return (2.0 ** (-8.0 * np.arange(1, n + 1, dtype=np.float32) / n)).astype(np.float32)


BAND_ROWS = 512


def _band_tiles(Ls, W):
    R = min(Ls, BAND_ROWS)
    Q = min(R, 128)
    assert Ls % R == 0 and R % Q == 0 and R % W == 0 and Ls % W == 0, (Ls, R, Q, W)
    return R, Q


def _banded_attention(q, k, v, slopes, *, W, n_kg, G, sink=None, want_lse=False):
    bsz, r, Ls, Cq = q.shape
    Ck = k.shape[-1]
    R, Q = _band_tiles(Ls, W)
    nchunks = Ls // R
    nblk = Ls // W
    per = R // W
    P = math.gcd(r, max(1, BAND_ROWS // R))
    bias = _band_bias(slopes, r, W, Q)
    q_spec = pl.BlockSpec((None, P, R, Cq), lambda b, j, c: (b, j, c, 0))
    main = pl.BlockSpec((None, P, R, Ck), lambda b, j, c: (b, j, c, 0))
    prev = pl.BlockSpec((None, P, W, Ck), lambda b, j, c: (b, j, jnp.maximum(c * per - 1, 0), 0))
    nxt = pl.BlockSpec((None, P, W, Ck), lambda b, j, c: (b, j, jnp.minimum((c + 1) * per, nblk - 1), 0))
    in_specs = [q_spec, prev, main, nxt, prev, main, nxt, _const_spec(bias.shape)]
    args = [q, k, k, k, v, v, v, bias]
    if sink is not None:
        in_specs.append(pl.BlockSpec(memory_space=pltpu.SMEM))
        args.append(sink)
    out_specs = [q_spec]
    out_shape = [jax.ShapeDtypeStruct(q.shape, BF16)]
    if want_lse:
        out_specs.append(pl.BlockSpec((None, P, R, LSE_LANES), lambda b, j, c: (b, j, c, 0)))
        out_shape.append(jax.ShapeDtypeStruct((bsz, r, Ls, LSE_LANES), F32))
    kern = functools.partial(_banded_kernel, W=W, Q=Q, R=R, n_kg=n_kg, G=G, has_sink=sink is not None,
                             want_lse=want_lse, nchunks=nchunks)
    outs = pl.pallas_call(
        kern,
        grid=(bsz, r // P, nchunks),
        in_specs=in_specs,
        out_specs=out_specs,
        out_shape=out_shape,
        scratch_shapes=[pltpu.VMEM((P, R + 2 * W, Ck), BF16), pltpu.VMEM((P, R + 2 * W, Ck), BF16)],
        compiler_params=_params("parallel", "parallel", "parallel"),
        name=f"banded_w{W}_r{r}",
    )(*args)
    return (outs[0], outs[1]) if want_lse else outs[0]


VT_ROWS = 80
FLASH_STRIP = 256
FLASH_UNROLL = 4


def _flash_kernel(q_ref, k_ref, vt_ref, o_ref, *, tk, nk):
    tq = q_ref.shape[1]
    ns = tq // FLASH_STRIP
    chains = [(h, c) for h in range(2) for c in range(ns)]
    qs = [q_ref[h, c * FLASH_STRIP:(c + 1) * FLASH_STRIP, :] for h, c in chains]

    def body(t, carry):
        off = pl.multiple_of(t * tk, tk)
        kt = [k_ref[h, pl.ds(off, tk), :] for h in range(2)]
        vt = [vt_ref[h, :, pl.ds(off, tk)] for h in range(2)]
        ss = [_dot_nt(kt[h], qs[n]) for n, (h, _) in enumerate(chains)]
        ms = [jnp.maximum(carry[2 * n], jnp.max(ss[n], axis=0, keepdims=True)) for n in range(len(chains))]
        ps = [jnp.exp2(ss[n] - ms[n]).astype(BF16) for n in range(len(chains))]
        out = []
        for n, (h, _) in enumerate(chains):
            alpha = jnp.exp2(carry[2 * n] - ms[n])
            out += [ms[n], carry[2 * n + 1] * alpha + _dot(vt[h], ps[n])]
        return tuple(out)

    m0 = jnp.full((1, FLASH_STRIP), NEG, F32)
    acc0 = jnp.zeros((VT_ROWS, FLASH_STRIP), F32)
    res = lax.fori_loop(0, nk, body, (m0, acc0) * len(chains), unroll=FLASH_UNROLL)
    accs = [jnp.concatenate([res[2 * (h * ns + c) + 1] for c in range(ns)], axis=1) for h in range(2)]
    outs = [acc[:B_V] / acc[B_V:B_V + 1] for acc in accs]
    o_ref[...] = jnp.concatenate(outs, axis=0).T.astype(BF16)


def _flash_tiles(L):
    tq = min(L, 1024)
    tk = min(L, 512)
    assert L % tq == 0 and L % tk == 0
    return tq, tk


def _latent_attention(qb, kb, vt, bsz, L):
    tq, tk = _flash_tiles(L)
    q4 = qb.reshape(B_HEADS, bsz, L, LANES)
    k4 = kb.reshape(B_HEADS, bsz, L, LANES)
    k_spec = pl.BlockSpec((2, None, L, LANES), lambda b, hp, i: (hp, b, 0, 0))
    vt_spec = pl.BlockSpec((2, VT_ROWS, L), lambda b, hp, i: (hp, 0, b))
    out = pl.pallas_call(
        functools.partial(_flash_kernel, tk=tk, nk=L // tk),
        grid=(bsz, B_HEADS // 2, L // tq),
        in_specs=[pl.BlockSpec((2, None, tq, LANES), lambda b, hp, i: (hp, b, i, 0)), k_spec, vt_spec],
        out_specs=pl.BlockSpec((None, tq, LANES), lambda b, hp, i: (b, i, hp)),
        out_shape=jax.ShapeDtypeStruct((bsz, L, B_WIDTH), BF16),
        compiler_params=_params("parallel", "parallel", "parallel"),
        name="latent_flash",
    )(q4, k4, vt)
    return out.reshape(bsz * L, B_WIDTH)


def _mix_out_kernel(*refs):
    it = iter(refs)
    x_ref, ya_ref, yb_ref = next(it), next(it), next(it)
    c_refs = [next(it) for _ in STRIDES]
    l_refs = [next(it) for _ in STRIDES]
    ga_ref, gb_ref, gc_ref, w_ref = next(it), next(it), next(it), next(it)
    kv_ref, gx_ref, wq_ref, wo_ref = next(it), next(it), next(it), next(it)
    o_ref = next(it)
    c_scr = next(it)
    tm = x_ref.shape[0]
    ngrp = C_WIDTH // LANES

    for n, r in enumerate(STRIDES):
        for j in range(r):
            blk = c_refs[n][j].astype(F32)
            for g in range(ngrp):
                c_scr[n * ngrp + g, pl.ds(j, tm // r, stride=r), :] = blk[:, g * LANES:(g + 1) * LANES]

    ls = [l_ref[...] for l_ref in l_refs]
    mx = functools.reduce(jnp.maximum, ls)
    es = [jnp.exp(l - mx) for l in ls]
    tot = functools.reduce(lambda a, b: a + b, es)
    head_of_lane = lax.broadcasted_iota(jnp.int32, (LSE_LANES, C_WIDTH), 1) // HEAD_DIM
    spread = (head_of_lane == lax.broadcasted_iota(jnp.int32, (LSE_LANES, C_WIDTH), 0)).astype(BF16)
    wide = []
    for e in es:
        w = e / tot
        hi = w.astype(BF16)
        lo = (w - hi.astype(F32)).astype(BF16)
        wide.append(_dot(hi, spread) + _dot(lo, spread))
    groups = []
    for kg in range(ngrp):
        acc = jnp.zeros((tm, LANES), F32)
        for n, w in enumerate(wide):
            acc = acc + c_scr[n * ngrp + kg] * w[:, kg * LANES:(kg + 1) * LANES]
        groups.append(acc)
    yc = jnp.concatenate(groups, axis=-1)
    y = jnp.concatenate([
        _rms(ya_ref[...].astype(F32), ga_ref[...]).astype(BF16),
        _rms(yb_ref[...].astype(F32), gb_ref[...]).astype(BF16),
        _rms(yc, gc_ref[...]).astype(BF16)], axis=-1)
    x = x_ref[...] + _dot(y, w_ref[...])
    o_ref[...] = _cross_attention(x, kv_ref, gx_ref, wq_ref, wo_ref)


def _mix_out(x2, ya, yb, ycs, lses, kvm, lw, L, tm):
    T = x2.shape[0]
    nl = L // tm
    row = lambda w: pl.BlockSpec((tm, w), lambda i: (i, 0))
    weights = (lw['g_out_a'], lw['g_out_b'], lw['g_out_c'], lw['w_out'])
    x_weights = (lw['g_x'], lw['w_xq'], lw['w_xo'])
    n = len(STRIDES)
    return pl.pallas_call(
        _mix_out_kernel,
        grid=(T // tm,),
        in_specs=([row(D_MODEL), row(A_WIDTH), row(B_WIDTH)]
                  + [_phase_spec(r, tm, nl, C_WIDTH) for r in STRIDES]
                  + [row(LSE_LANES)] * n
                  + [_const_spec(w.shape) for w in weights]
                  + [pl.BlockSpec((None,) + kvm.shape[1:], lambda i: (i // nl, 0, 0))]
                  + [_const_spec(w.shape) for w in x_weights]),
        out_specs=row(D_MODEL),
        out_shape=jax.ShapeDtypeStruct((T, D_MODEL), F32),
        scratch_shapes=[pltpu.VMEM((n * C_WIDTH // LANES, tm, LANES), F32)],
        compiler_params=_params("parallel"),
        name="mix_out_cross",
    )(x2, ya, yb, *ycs, *lses, *weights, kvm, *x_weights)


def _norm_matmul_kernel(x_ref, g_ref, w_ref, o_ref):
    o_ref[...] = _dot(_rms(x_ref[...], g_ref[...]).astype(BF16), w_ref[...]).astype(o_ref.dtype)


def _norm_matmul(x2, g, w, tm):
    T, K = x2.shape
    N = w.shape[1]
    return pl.pallas_call(
        _norm_matmul_kernel,
        grid=(T // tm,),
        in_specs=[pl.BlockSpec((tm, K), lambda i: (i, 0)), _const_spec(g.shape), _const_spec(w.shape)],
        out_specs=pl.BlockSpec((tm, N), lambda i: (i, 0)),
        out_shape=jax.ShapeDtypeStruct((T, N), BF16),
        compiler_params=_params("parallel"),
        name="norm_matmul",
    )(x2, g, w)


def _cross_attention(x, kv_ref, g_ref, wq_ref, wo_ref):
    h = _rms(x, g_ref[...]).astype(BF16)
    q = (_dot(h, wq_ref[...]) * (X_HEAD_DIM ** -0.5)).astype(BF16)
    xd = X_HEADS * X_HEAD_DIM
    outs = []
    for hd in range(X_HEADS):
        sl = slice(hd * X_HEAD_DIM, (hd + 1) * X_HEAD_DIM)
        k = kv_ref[:, sl]
        v = kv_ref[:, xd + hd * X_HEAD_DIM:xd + (hd + 1) * X_HEAD_DIM]
        s = _dot_nt(q[:, sl], k)
        p = jnp.exp(s - jnp.max(s, axis=-1, keepdims=True))
        den = jnp.sum(p, axis=-1, keepdims=True)
        outs.append((_dot(p.astype(BF16), v) / den).astype(BF16))
    return x + _dot(jnp.concatenate(outs, axis=-1), wo_ref[...])


def _silu_mul(gate, up):
    return gate * jax.nn.sigmoid(gate) * up


def _swiglu_kernel(x_ref, g_ref, wg_ref, wu_ref, wd_ref, o_ref, h_scr, acc_scr):
    j = pl.program_id(1)

    @pl.when(j == 0)
    def _():
        h_scr[...] = _rms(x_ref[...], g_ref[...]).astype(BF16)
        acc_scr[...] = jnp.zeros_like(acc_scr)

    h = h_scr[...]
    act = _silu_mul(_dot(h, wg_ref[...]), _dot(h, wu_ref[...])).astype(BF16)
    acc_scr[...] += _dot(act, wd_ref[...])

    @pl.when(j == pl.num_programs(1) - 1)
    def _():
        o_ref[...] = x_ref[...] + acc_scr[...]


def _ff_chunk(dff, limit):
    for c in (1792, 1408, 1024, 512, 256, 128):
        if c <= limit and dff % c == 0:
            return c
    raise ValueError(dff)


def _swiglu(x2, g, wg, wu, wd, tm):
    T = x2.shape[0]
    dff = wg.shape[1]
    ck = _ff_chunk(dff, 1408)
    return pl.pallas_call(
        _swiglu_kernel,
        grid=(T // tm, dff // ck),
        in_specs=[pl.BlockSpec((tm, D_MODEL), lambda i, j: (i, 0)),
                  _const_spec(g.shape),
                  pl.BlockSpec((D_MODEL, ck), lambda i, j: (0, j)),
                  pl.BlockSpec((D_MODEL, ck), lambda i, j: (0, j)),
                  pl.BlockSpec((ck, D_MODEL), lambda i, j: (j, 0))],
        out_specs=pl.BlockSpec((tm, D_MODEL), lambda i, j: (i, 0)),
        out_shape=jax.ShapeDtypeStruct((T, D_MODEL), F32),
        scratch_shapes=[pltpu.VMEM((tm, D_MODEL), BF16), pltpu.VMEM((tm, D_MODEL), F32)],
        compiler_params=_params("parallel", "arbitrary"),
        name="swiglu_dense",
    )(x2, g, wg, wu, wd)


def _router_logits(h, wr):
    return jnp.dot(h, wr, preferred_element_type=F32, precision=lax.Precision.HIGHEST)


def _router_kernel(x_ref, g_ref, wr_ref, tri_ref, c0_ref, route_ref, cnt_ref, run_scr):
    @pl.when(pl.program_id(0) == 0)
    def _():
        run_scr[...] = c0_ref[...]

    logits = _router_logits(_rms(x_ref[...], g_ref[...]), wr_ref[...])
    tm = logits.shape[0]
    lane = lax.broadcasted_iota(jnp.int32, (tm, LANES), 1)
    logits = jnp.where(lane < N_EXPERTS, logits, NEG)
    v1 = jnp.max(logits, axis=-1, keepdims=True)
    i1 = jnp.min(jnp.where(logits == v1, lane, LANES), axis=-1, keepdims=True)
    rest = jnp.where(lane == i1, NEG, logits)
    v2 = jnp.max(rest, axis=-1, keepdims=True)
    i2 = jnp.min(jnp.where(rest == v2, lane, LANES), axis=-1, keepdims=True)
    pid = jnp.minimum(i1, i2) * N_EXPERTS + jnp.maximum(i1, i2)
    onehot = (lane == pid).astype(F32)
    before = _dot(tri_ref[...], onehot.astype(BF16)) + run_scr[...]
    rank = jnp.sum(onehot * before, axis=-1, keepdims=True)
    run_scr[...] += jnp.sum(onehot, axis=0, keepdims=True)
    cnt_ref[...] = run_scr[...]
    lane8 = lax.broadcasted_iota(jnp.int32, (tm, LSE_LANES), 1)
    route_ref[...] = jnp.where(lane8 == 0, pid, jnp.where(lane8 == 1, rank.astype(jnp.int32), 0))


def _router(x2, g, wr, counts0, tm):
    T = x2.shape[0]
    tri = (np.arange(tm)[:, None] > np.arange(tm)[None, :]).astype(np.float32)
    tri = jnp.asarray(tri, BF16)
    return pl.pallas_call(
        _router_kernel,
        grid=(T // tm,),
        in_specs=[pl.BlockSpec((tm, D_MODEL), lambda i: (i, 0)), _const_spec(g.shape), _const_spec(wr.shape),
                  _const_spec(tri.shape), _const_spec(counts0.shape)],
        out_specs=[pl.BlockSpec((tm, LSE_LANES), lambda i: (i, 0)), _const_spec(counts0.shape)],
        out_shape=[jax.ShapeDtypeStruct((T, LSE_LANES), jnp.int32), jax.ShapeDtypeStruct(counts0.shape, F32)],
        scratch_shapes=[pltpu.VMEM(counts0.shape, F32)],
        compiler_params=_params("arbitrary"),
        name="router",
    )(x2, g, wr, tri, counts0)


ROW_SUBLANES = D_MODEL // LANES


def _slab_to_rows(ref, n):
    return jnp.concatenate([ref[pl.ds(c, n, stride=ROW_SUBLANES), :] for c in range(ROW_SUBLANES)], axis=-1)


def _rows_to_slab(ref, rows, n):
    for c in range(ROW_SUBLANES):
        ref[pl.ds(c, n, stride=ROW_SUBLANES), :] = rows[:, c * LANES:(c + 1) * LANES]


def _slab(ref, row):
    return ref.at[pl.ds(pl.multiple_of(row * ROW_SUBLANES, ROW_SUBLANES), ROW_SUBLANES)]


MOVE_UNROLL = 8


def _move_rows(n, copy_of_row, copy_of_all):
    def issue(r, _):
        copy_of_row(r).start()
        return 0

    lax.fori_loop(0, n, issue, 0, unroll=MOVE_UNROLL)
    copy_of_all().wait()


def _dispatch_kernel(route_ref, start_ref, x_ref, prev_ref, o_ref, slab, sem, *, tm):
    del prev_ref
    _rows_to_slab(slab, x_ref[...], tm)

    def copy_of_row(r):
        slot = start_ref[route_ref[0, r]] + route_ref[1, r]
        return pltpu.make_async_copy(_slab(slab, r), _slab(o_ref, slot), sem)

    def copy_of_all():
        return pltpu.make_async_copy(slab, o_ref.at[pl.ds(0, tm * ROW_SUBLANES)], sem)

    _move_rows(tm, copy_of_row, copy_of_all)


def _route_blocks(route, tm):
    T = route.shape[0]
    return route[:, :2].T.reshape(2, T // tm, tm).transpose(1, 0, 2)


def _dispatch(x2, route, start, sorted_rows, tm):
    T = x2.shape[0]
    return pl.pallas_call(
        functools.partial(_dispatch_kernel, tm=tm),
        grid=(T // tm,),
        in_specs=[pl.BlockSpec((None, 2, tm), lambda i: (i, 0, 0), memory_space=pltpu.SMEM),
                  pl.BlockSpec(memory_space=pltpu.SMEM),
                  pl.BlockSpec((tm, D_MODEL), lambda i: (i, 0)),
                  pl.BlockSpec(memory_space=pl.ANY)],
        out_specs=pl.BlockSpec(memory_space=pl.ANY),
        out_shape=jax.ShapeDtypeStruct(sorted_rows.shape, sorted_rows.dtype),
        scratch_shapes=[pltpu.VMEM((tm * ROW_SUBLANES, LANES), F32), pltpu.SemaphoreType.DMA],
        input_output_aliases={3: 0},
        compiler_params=pltpu.CompilerParams(dimension_semantics=("arbitrary",), vmem_limit_bytes=VMEM_LIMIT,
                                             has_side_effects=True),
        name="dispatch",
    )(_route_blocks(route, tm), start, x2, sorted_rows)


def _unpermute_kernel(route_ref, start_ref, y_ref, o_ref, slab, sem, *, tm):
    def copy_of_row(r):
        slot = start_ref[route_ref[0, r]] + route_ref[1, r]
        return pltpu.make_async_copy(_slab(y_ref, slot), _slab(slab, r), sem)

    def copy_of_all():
        return pltpu.make_async_copy(y_ref.at[pl.ds(0, tm * ROW_SUBLANES)], slab, sem)

    _move_rows(tm, copy_of_row, copy_of_all)
    o_ref[...] = _slab_to_rows(slab, tm)


def _unpermute(y_sorted, route, start, T, tm):
    return pl.pallas_call(
        functools.partial(_unpermute_kernel, tm=tm),
        grid=(T // tm,),
        in_specs=[pl.BlockSpec((None, 2, tm), lambda i: (i, 0, 0), memory_space=pltpu.SMEM),
                  pl.BlockSpec(memory_space=pltpu.SMEM),
                  pl.BlockSpec(memory_space=pl.ANY)],
        out_specs=pl.BlockSpec((tm, D_MODEL), lambda i: (i, 0)),
        out_shape=jax.ShapeDtypeStruct((T, D_MODEL), F32),
        scratch_shapes=[pltpu.VMEM((tm * ROW_SUBLANES, LANES), F32), pltpu.SemaphoreType.DMA],
        compiler_params=_params("arbitrary"),
        name="unpermute",
    )(_route_blocks(route, tm), start, y_sorted)


def _pair_experts_kernel(ta_ref, tb_ref, nv_ref, xs_ref, g_ref, wr_ref, wg_ref, wu_ref, wd_ref, gf_ref, o_ref,
                         x_scr, h_scr, gate_scr, acc_scr, *, nj, final_norm):
    i = pl.program_id(0)
    j = pl.program_id(1)
    tm = x_scr.shape[0]

    @pl.when(i < nv_ref[0])
    def _():
        @pl.when(j == 0)
        def _():
            x = _slab_to_rows(xs_ref, tm)
            x_scr[...] = x
            h = _rms(x, g_ref[...])
            h_scr[...] = h.astype(BF16)
            logits = _router_logits(h, wr_ref[...])
            lane = lax.broadcasted_iota(jnp.int32, logits.shape, 1)
            la = jnp.sum(jnp.where(lane == ta_ref[i], logits, 0.0), axis=-1, keepdims=True)
            lb = jnp.sum(jnp.where(lane == tb_ref[i], logits, 0.0), axis=-1, keepdims=True)
            mx = jnp.maximum(la, lb)
            ea, eb = jnp.exp(la - mx), jnp.exp(lb - mx)
            gate_scr[0] = ea / (ea + eb)
            gate_scr[1] = eb / (ea + eb)
            acc_scr[...] = jnp.zeros_like(acc_scr)

        h = h_scr[...]
        gate = jnp.where(j < nj, gate_scr[0], gate_scr[1])
        act = (_silu_mul(_dot(h, wg_ref[...]), _dot(h, wu_ref[...])) * gate).astype(BF16)
        acc_scr[...] += _dot(act, wd_ref[...])

        @pl.when(j == 2 * nj - 1)
        def _():
            out = x_scr[...] + acc_scr[...]
            if final_norm:
                out = _rms(out, gf_ref[...])
            _rows_to_slab(o_ref, out, tm)

    @pl.when(jnp.logical_and(i >= nv_ref[0], j == 2 * nj - 1))
    def _():
        o_ref[...] = jnp.zeros_like(o_ref)


def _pair_experts(xs, g, wr, wg, wu, wd, g_final, tile_a, tile_b, n_valid, tm, final_norm):
    rows = xs.shape[0] // ROW_SUBLANES
    dff = wg.shape[2]
    ck = _ff_chunk(dff, 1792)
    nj = dff // ck

    def expert(i, j, ta, tb):
        return jnp.where(j < nj, ta[i], tb[i])

    def chunk(i, j, nv):
        return jnp.where(i < nv[0], j % nj, nj - 1)

    slab_spec = pl.BlockSpec((tm * ROW_SUBLANES, LANES), lambda i, j, ta, tb, nv: (i, 0))
    const = lambda a: pl.BlockSpec(a.shape, lambda i, j, ta, tb, nv: (0,) * a.ndim)
    grid_spec = pltpu.PrefetchScalarGridSpec(
        num_scalar_prefetch=3,
        grid=(rows // tm, 2 * nj),
        in_specs=[slab_spec, const(g), const(wr),
                  pl.BlockSpec((None, D_MODEL, ck), lambda i, j, ta, tb, nv: (expert(i, j, ta, tb), 0, chunk(i, j, nv))),
                  pl.BlockSpec((None, D_MODEL, ck), lambda i, j, ta, tb, nv: (expert(i, j, ta, tb), 0, chunk(i, j, nv))),
                  pl.BlockSpec((None, ck, D_MODEL), lambda i, j, ta, tb, nv: (expert(i, j, ta, tb), chunk(i, j, nv), 0)),
                  const(g_final)],
        out_specs=slab_spec,
        scratch_shapes=[pltpu.VMEM((tm, D_MODEL), F32), pltpu.VMEM((tm, D_MODEL), BF16),
                        pltpu.VMEM((2, tm, 1), F32), pltpu.VMEM((tm, D_MODEL), F32)],
    )
    return pl.pallas_call(
        functools.partial(_pair_experts_kernel, nj=nj, final_norm=final_norm),
        grid_spec=grid_spec,
        out_shape=jax.ShapeDtypeStruct(xs.shape, F32),
        compiler_params=_params("parallel", "arbitrary"),
        name="pair_experts",
    )(tile_a, tile_b, n_valid, xs, g, wr, wg, wu, wd, g_final)


def _plan_pairs(counts, n_tiles, tm):
    counts = counts.reshape(-1).astype(jnp.int32)
    tiles = (counts + tm - 1) // tm
    tile_end = jnp.cumsum(tiles)
    start = (tile_end - tiles) * tm
    n_valid = tile_end[-1:]
    tile_ids = jnp.minimum(jnp.arange(n_tiles), n_valid[0] - 1)
    tile_pid = jnp.sum(tile_ids[:, None] >= tile_end[None, :], axis=-1)
    tile_a = (tile_pid // N_EXPERTS).astype(jnp.int32)
    tile_b = (tile_pid % N_EXPERTS).astype(jnp.int32)
    return start.astype(jnp.int32), tile_a, tile_b, n_valid.astype(jnp.int32)


def _prep_layer(l, p):
    offs = np.cumsum((0,) + IN_SIZES)
    w_in = p['w_in'][l]
    qa, ka, va, cq, ckv, kr, qc, kc, vc = [w_in[:, int(offs[i]):int(offs[i + 1])] for i in range(9)]
    scale = HEAD_DIM ** -0.5
    order = list(A_HEAD_ORDER)
    qa = qa.reshape(D_MODEL, A_Q_HEADS, HEAD_DIM)[:, order, :].reshape(D_MODEL, A_WIDTH) * scale
    half = B_ROPE // 2
    z = lambda *s: jnp.zeros(s, F32)
    kr_pad = jnp.concatenate([z(D_MODEL, B_NOPE), kr, z(D_MODEL, LANES - B_NOPE - B_ROPE)], axis=1)
    kr_rot = jnp.concatenate([z(D_MODEL, B_NOPE), -kr[:, half:], kr[:, :half], z(D_MODEL, LANES - B_NOPE - B_ROPE)], axis=1)
    uq = p['mla_w_uq'][l].reshape(B_Q_LORA, B_HEADS, B_NOPE + B_ROPE)
    padq = z(B_Q_LORA, B_HEADS, LANES - B_NOPE - B_ROPE)
    uq_pad = jnp.concatenate([uq, padq], axis=-1).reshape(B_Q_LORA, B_HEADS * LANES)
    uq_rot = jnp.concatenate([z(B_Q_LORA, B_HEADS, B_NOPE), -uq[..., B_NOPE + half:], uq[..., B_NOPE:B_NOPE + half], padq],
                             axis=-1).reshape(B_Q_LORA, B_HEADS * LANES)
    ukv = p['mla_w_ukv'][l].reshape(B_KV_LORA, B_HEADS, B_NOPE + B_V)
    zk = z(B_KV_LORA, B_HEADS, HALF)
    uk_pad = jnp.concatenate([ukv[..., :B_NOPE], zk], axis=-1).reshape(B_KV_LORA, B_HEADS * LANES)
    uv_pad = jnp.concatenate([ukv[..., B_NOPE:], zk], axis=-1).reshape(B_KV_LORA, B_HEADS * LANES)

    g_out = p['mix_out_norm_g'][l]
    w_out = p['w_out'][l]
    a_rows = np.concatenate([np.arange(HEAD_DIM) + HEAD_DIM * h for h in A_HEAD_ORDER])
    w_out = jnp.concatenate([w_out[:A_WIDTH][a_rows], w_out[A_WIDTH:]], axis=0)
    g_out_a = g_out[:A_WIDTH][a_rows]
    row = lambda v: v.reshape(1, -1).astype(F32)
    return {
        'g_mix': row(p['norm_mix_g'][l]),
        'wa': jnp.concatenate([qa, ka, va], axis=1).astype(BF16),
        'wc': jnp.concatenate([qc * scale, kc, vc], axis=1).astype(BF16),
        'wb': jnp.concatenate([cq, ckv], axis=1).astype(BF16),
        'wkr': jnp.concatenate([kr_pad, kr_rot], axis=1).astype(BF16),
        'g_q': row(p['mla_q_norm_g'][l]),
        'g_kv': row(p['mla_kv_norm_g'][l]),
        'wuq': jnp.concatenate([uq_pad, uq_rot], axis=1).astype(BF16),
        'wukv': jnp.concatenate([uk_pad, uv_pad], axis=1).astype(BF16),
        'sink': p['sink_a'][l][np.array(A_HEAD_ORDER)].astype(F32),
        'g_out_a': row(g_out_a),
        'g_out_b': row(g_out[A_WIDTH:A_WIDTH + B_WIDTH]),
        'g_out_c': row(g_out[A_WIDTH + B_WIDTH:]),
        'w_out': w_out.astype(BF16),
        'g_x': row(p['norm_x_g'][l]),
        'g_mem': row(p['norm_mem_g'][l]),
        'w_xq': p['w_xq'][l].astype(BF16),
        'w_xkv': p['w_xkv'][l].astype(BF16),
        'w_xo': p['w_xo'][l].astype(BF16),
        'g_ffn': row(p['norm_ffn_g'][l]),
    }


def _prep_ffn(w_gu, w_down):
    dff = w_down.shape[-2]
    return w_gu[..., :dff].astype(BF16), w_gu[..., dff:].astype(BF16), w_down.astype(BF16)


def _rope_tables(L):
    half = B_ROPE // 2
    pos = jnp.arange(L, dtype=F32)
    inv = ROPE_THETA ** (-jnp.arange(half, dtype=F32) / half)
    ang = pos[:, None] * inv[None, :]
    cos2 = jnp.tile(jnp.cos(ang), (1, 2))
    sin2 = jnp.tile(jnp.sin(ang), (1, 2))
    pad = jnp.zeros((L, LANES - B_NOPE - B_ROPE), F32)
    qscale = (B_NOPE + B_ROPE) ** -0.5 * LOG2E
    cq = jnp.concatenate([jnp.ones((L, B_NOPE), F32), cos2, pad], axis=1) * qscale
    sq = jnp.concatenate([jnp.zeros((L, B_NOPE), F32), sin2, pad], axis=1) * qscale
    ck = jnp.concatenate([jnp.zeros((L, B_NOPE), F32), cos2, pad], axis=1)
    sk = jnp.concatenate([jnp.zeros((L, B_NOPE), F32), sin2, pad], axis=1)
    return cq, sq, ck, sk


TOKEN_TILE = 512
EXPERT_TILE = 512
FFN_TILE = 1024
MOVE_TILE = 1024
N_PAIRS_USED = N_EXPERTS * (N_EXPERTS - 1) // 2


def _mixer_and_cross(x2, mem, lw, bsz, L):
    T = bsz * L
    tm = min(TOKEN_TILE, L)
    tabs = _rope_tables(L)
    slopes_a = _alibi_slopes(A_Q_HEADS)[list(A_HEAD_ORDER)]
    slopes_c = _alibi_slopes(C_HEADS)
    (qa, ka, va), qkv_c, (qb, kb, vb) = _in_proj(x2, lw, tabs, bsz, L, tm)
    as_seq = lambda t: t.reshape(bsz, 1, L, t.shape[-1])
    ya = _banded_attention(as_seq(qa), as_seq(ka), as_seq(va), slopes_a, W=A_WINDOW, n_kg=1, G=2, sink=lw['sink'])
    ya = ya.reshape(T, A_WIDTH)
    yb = _latent_attention(qb, kb, vb, bsz, L)
    ycs, lses = [], []
    for (w, r), (qc, kc, vc) in zip(DILATED_PAIRS, qkv_c):
        o, lse = _banded_attention(qc, kc, vc, slopes_c, W=w // (2 * r), n_kg=C_HEADS // 2, G=1, want_lse=True)
        ycs.append(o)
        lses.append(jnp.transpose(lse, (0, 2, 1, 3)).reshape(T, LSE_LANES))
    mem2 = mem.reshape(-1, D_MODEL)
    kvm = _norm_matmul(mem2, lw['g_mem'], lw['w_xkv'], min(512, mem2.shape[0]))
    return _mix_out(x2, ya, yb, ycs, lses, kvm.reshape(bsz, mem.shape[1], -1), lw, L, tm)


def _routed_swiglu(xs2, g, wr, wg, wu, wd, final_g, final_norm):
    total = sum(x.shape[0] for x in xs2)
    et = min(EXPERT_TILE, total)
    n_tiles = total // et + N_PAIRS_USED
    counts = jnp.zeros((1, LANES), F32)
    routes = []
    for x2 in xs2:
        route, counts = _router(x2, g, wr, counts, min(TOKEN_TILE, x2.shape[0]))
        routes.append(route)
    start, tile_a, tile_b, n_valid = _plan_pairs(counts, n_tiles, et)
    sorted_rows = jnp.zeros((n_tiles * et * ROW_SUBLANES, LANES), F32)
    for x2, route in zip(xs2, routes):
        sorted_rows = _dispatch(x2, route, start, sorted_rows, min(MOVE_TILE, x2.shape[0]))
    y_sorted = _pair_experts(sorted_rows, g, wr, wg, wu, wd, final_g, tile_a, tile_b, n_valid, et, final_norm)
    return [_unpermute(y_sorted, route, start, x2.shape[0], min(MOVE_TILE, x2.shape[0]))
            for x2, route in zip(xs2, routes)]


def _encoder(groups, layers, ffn, moe, routers, final_g):
    shapes = [x.shape for x, _ in groups]
    xs2 = [x.reshape(-1, D_MODEL) for x, _ in groups]
    depth = len(layers)
    normed = False
    for l, lw in enumerate(layers):
        xs2 = [_mixer_and_cross(x2, mem, lw, shp[0], shp[1]) for x2, (_, mem), shp in zip(xs2, groups, shapes)]
        if l % 2 == 0:
            wg, wu, wd = ffn[l // 2]
            xs2 = [_swiglu(x2, lw['g_ffn'], wg, wu, wd, min(FFN_TILE, x2.shape[0])) for x2 in xs2]
        else:
            wg, wu, wd = moe[l // 2]
            normed = l == depth - 1
            xs2 = _routed_swiglu(xs2, lw['g_ffn'], routers[l // 2], wg, wu, wd, final_g, final_norm=normed)
    assert normed, "the final rmsnorm is fused into the last layer's routed SwiGLU"
    return tuple(x2.reshape(shp) for x2, shp in zip(xs2, shapes))


def kernel(x_prompt, x_sample, mem_prompt, mem_sample, norm_mix_g, w_in, sink_a, mla_q_norm_g, mla_kv_norm_g, mla_w_uq, mla_w_ukv, mix_out_norm_g, w_out, norm_x_g, norm_mem_g, w_xq, w_xkv, w_xo, norm_ffn_g, ffn_w_gu, ffn_w_down, moe_router, moe_w_gu, moe_w_down, final_norm_g):
    p = dict(norm_mix_g=norm_mix_g, w_in=w_in, sink_a=sink_a, mla_q_norm_g=mla_q_norm_g, mla_kv_norm_g=mla_kv_norm_g,
             mla_w_uq=mla_w_uq, mla_w_ukv=mla_w_ukv, mix_out_norm_g=mix_out_norm_g, w_out=w_out, norm_x_g=norm_x_g,
             norm_mem_g=norm_mem_g, w_xq=w_xq, w_xkv=w_xkv, w_xo=w_xo, norm_ffn_g=norm_ffn_g)
    depth = w_in.shape[0]
    layers = [_prep_layer(l, p) for l in range(depth)]
    ffn = [_prep_ffn(ffn_w_gu[i], ffn_w_down[i]) for i in range(ffn_w_gu.shape[0])]
    moe = [_prep_ffn(moe_w_gu[i], moe_w_down[i]) for i in range(moe_w_gu.shape[0])]
    routers = [jnp.pad(moe_router[i].astype(F32), ((0, 0), (0, LANES - N_EXPERTS))) for i in range(moe_router.shape[0])]
    final_g = final_norm_g.reshape(1, -1).astype(F32)
    return _encoder([(x_prompt, mem_prompt), (x_sample, mem_sample)], layers, ffn, moe, routers, final_g)
```

```python
import functools
import math

import numpy as np
import jax
import jax.numpy as jnp
from jax import lax
from jax.experimental import pallas as pl
from jax.experimental.pallas import tpu as pltpu

D_MODEL = 1024
HEAD_DIM = 64
A_Q_HEADS = 4
A_KV_HEADS = 2
A_WINDOW = 128
B_HEADS = 6
B_NOPE = 64
B_ROPE = 32
B_V = 64
B_Q_LORA = 384
B_KV_LORA = 256
ROPE_THETA = 10000.0
C_HEADS = 6
DILATED_PAIRS = ((128, 1), (512, 4), (2048, 16))
STRIDES = tuple(r for _, r in DILATED_PAIRS)
A_WIDTH = A_Q_HEADS * HEAD_DIM
B_WIDTH = B_HEADS * B_V
C_WIDTH = C_HEADS * HEAD_DIM
IN_SIZES = (A_WIDTH, A_KV_HEADS * HEAD_DIM, A_KV_HEADS * HEAD_DIM, B_Q_LORA, B_KV_LORA, B_ROPE,
            C_WIDTH, C_WIDTH, C_WIDTH)
X_HEADS = 4
X_HEAD_DIM = 128
N_EXPERTS = 8
TOP_K = 2
EPS = 1e-6
NEG = -1e30

LANES = 128
HALF = LANES // 2
LSE_LANES = 8
VMEM_LIMIT = 56 * 1024 * 1024
LOG2E = math.log2(math.e)

BF16 = jnp.bfloat16
F32 = jnp.float32

A_HEAD_ORDER = (0, 2, 1, 3)


def _params(*sem):
    return pltpu.CompilerParams(dimension_semantics=sem, vmem_limit_bytes=VMEM_LIMIT)


def _rms(x, g):
    return x * lax.rsqrt(jnp.mean(x * x, axis=-1, keepdims=True) + EPS) * g


def _dot(a, b):
    return jnp.dot(a, b, preferred_element_type=F32)


def _dot_nt(a, b):
    return lax.dot_general(a, b, (((1,), (1,)), ((), ())), preferred_element_type=F32)


def _const_spec(shape):
    n = len(shape)
    return pl.BlockSpec(shape, lambda *_: (0,) * n)


def _phase_spec(r, tm, nl, width):
    return pl.BlockSpec((None, r, tm // r, width), lambda i: (i // nl, 0, i % nl, 0))


def _in_kernel(*refs):
    it = iter(refs)
    x_ref, g_ref, wa_ref, wc_ref, wb_ref, wkr_ref, gq_ref, gkv_ref, wuq_ref, wukv_ref = [next(it) for _ in range(10)]
    cq_ref, sq_ref, ck_ref, sk_ref = [next(it) for _ in range(4)]
    qa_ref, ka_ref, va_ref = next(it), next(it), next(it)
    c_refs = [(next(it), next(it), next(it)) for _ in STRIDES]
    qb_ref, kb_ref, vb_ref = next(it), next(it), next(it)
    zc_scr = next(it)
    tm = x_ref.shape[0]

    h = _rms(x_ref[...], g_ref[...]).astype(BF16)
    za = _dot(h, wa_ref[...])
    qa_ref[...] = za[:, :A_WIDTH].astype(BF16)
    ka_ref[...] = za[:, A_WIDTH:A_WIDTH + LANES].astype(BF16)
    va_ref[...] = za[:, A_WIDTH + LANES:].astype(BF16)

    zc = _dot(h, wc_ref[...])
    ngrp = C_WIDTH // LANES
    for g in range(3 * ngrp):
        zc_scr[g] = zc[:, g * LANES:(g + 1) * LANES]
    for r, qkv_refs in zip(STRIDES, c_refs):
        for j in range(r):
            for n, ref in enumerate(qkv_refs):
                ref[j] = jnp.concatenate(
                    [zc_scr[n * ngrp + g, pl.ds(j, tm // r, stride=r), :] for g in range(ngrp)], axis=-1).astype(BF16)

    zb = _dot(h, wb_ref[...])
    hq = _rms(zb[:, :B_Q_LORA], gq_ref[...]).astype(BF16)
    hkv = _rms(zb[:, B_Q_LORA:], gkv_ref[...]).astype(BF16)
    zq = _dot(hq, wuq_ref[...])
    zkv = _dot(hkv, wukv_ref[...])
    zkr = _dot(h, wkr_ref[...])
    kr = zkr[:, :LANES] * ck_ref[...] + zkr[:, LANES:] * sk_ref[...]
    cq = cq_ref[...]
    sq = sq_ref[...]
    lane = lax.broadcasted_iota(jnp.int32, (1, LANES), 1)
    nb = B_HEADS * LANES
    for hd in range(B_HEADS):
        lo, hi = hd * LANES, (hd + 1) * LANES
        qb_ref[hd] = (zq[:, lo:hi] * cq + zq[:, nb + lo:nb + hi] * sq).astype(BF16)
        kb_ref[hd] = (zkv[:, lo:hi] + kr).astype(BF16)
        ones = (lane == B_V).astype(F32)
        vb_ref[hd] = (zkv[:, nb + lo:nb + hi] + ones).T.astype(BF16)


def _in_proj(x2, lw, tabs, bsz, L, tm):
    T = x2.shape[0]
    nl = L // tm
    row = lambda w: pl.BlockSpec((tm, w), lambda i: (i, 0))
    tab = pl.BlockSpec((tm, LANES), lambda i: (i % nl, 0))
    hm = pl.BlockSpec((B_HEADS, tm, LANES), lambda i: (0, i, 0))
    weights = (lw['g_mix'], lw['wa'], lw['wc'], lw['wb'], lw['wkr'], lw['g_q'], lw['g_kv'], lw['wuq'], lw['wukv'])
    out_specs = [row(A_WIDTH), row(LANES), row(LANES)]
    out_shape = [jax.ShapeDtypeStruct((T, w), BF16) for w in (A_WIDTH, LANES, LANES)]
    for r in STRIDES:
        assert tm % (16 * r) == 0, (tm, r)
        out_specs += [_phase_spec(r, tm, nl, C_WIDTH)] * 3
        out_shape += [jax.ShapeDtypeStruct((bsz, r, L // r, C_WIDTH), BF16)] * 3
    out_specs += [hm, hm, pl.BlockSpec((B_HEADS, LANES, tm), lambda i: (0, 0, i))]
    out_shape += [jax.ShapeDtypeStruct((B_HEADS, T, LANES), BF16)] * 2
    out_shape += [jax.ShapeDtypeStruct((B_HEADS, LANES, T), BF16)]
    outs = pl.pallas_call(
        _in_kernel,
        grid=(T // tm,),
        in_specs=[row(D_MODEL)] + [_const_spec(w.shape) for w in weights] + [tab] * 4,
        out_specs=out_specs,
        out_shape=out_shape,
        scratch_shapes=[pltpu.VMEM((3 * C_WIDTH // LANES, tm, LANES), F32)],
        compiler_params=_params("parallel"),
        name="in_proj",
    )(x2, *weights, *tabs)
    qkv_c = [outs[3 + 3 * n:6 + 3 * n] for n in range(len(STRIDES))]
    return outs[:3], qkv_c, outs[-3:]


def _banded_kernel(*refs, W, Q, R, n_kg, G, has_sink, want_lse, nchunks):
    it = iter(refs)
    q_ref = next(it)
    kp_ref, km_ref, kn_ref = next(it), next(it), next(it)
    vp_ref, vm_ref, vn_ref = next(it), next(it), next(it)
    bias_ref = next(it)
    sink_ref = next(it) if has_sink else None
    o_ref = next(it)
    lse_ref = next(it) if want_lse else None
    kfull, vfull = next(it), next(it)

    c = pl.program_id(2)
    P = q_ref.shape[0]
    kfull[:, 0:W] = kp_ref[...]
    kfull[:, W:W + R] = km_ref[...]
    kfull[:, W + R:] = kn_ref[...]
    vfull[:, 0:W] = vp_ref[...]
    vfull[:, W:W + R] = vm_ref[...]
    vfull[:, W + R:] = vn_ref[...]

    nsub = R // Q
    win = Q + 2 * W
    lo = lax.broadcasted_iota(jnp.int32, (Q, LANES), 1) < HALF
    col = lax.broadcasted_iota(jnp.int32, (2 * Q, win), 1)
    top = lax.broadcasted_iota(jnp.int32, (2 * Q, 1), 0) < Q
    lane8 = lax.broadcasted_iota(jnp.int32, (Q, LSE_LANES), 1)
    before_start = jnp.logical_and(c == 0, col < W)
    after_end = jnp.logical_and(c == nchunks - 1, col >= Q + W)
    units = [(ph, i, kg, g) for ph in range(P) for i in range(nsub) for kg in range(n_kg) for g in range(G)]

    scores = []
    for ph, i, kg, g in units:
        qg = kg * G + g
        qblk = q_ref[ph, i * Q:(i + 1) * Q, qg * LANES:(qg + 1) * LANES]
        zero = jnp.zeros_like(qblk)
        qm = jnp.concatenate([jnp.where(lo, qblk, zero), jnp.where(lo, zero, qblk)], axis=0)
        kwin = kfull[ph, i * Q:i * Q + win, kg * LANES:(kg + 1) * LANES]
        s = _dot_nt(qm, kwin) + bias_ref[qg]
        if i == 0:
            s = jnp.where(before_start, NEG, s)
        if i == nsub - 1:
            s = jnp.where(after_end, NEG, s)
        scores.append(s)

    probs = []
    for (ph, i, kg, g), s in zip(units, scores):
        qg = kg * G + g
        m = jnp.max(s, axis=-1, keepdims=True)
        if has_sink:
            sk = jnp.where(top, sink_ref[2 * qg], sink_ref[2 * qg + 1])
            m = jnp.maximum(m, sk)
        p = jnp.exp(s - m)
        den = jnp.sum(p, axis=-1, keepdims=True)
        if has_sink:
            den = den + jnp.exp(sk - m)
        probs.append((p.astype(BF16), den, m))

    lse_tiles = {(ph, i): jnp.zeros((Q, LSE_LANES), F32) for ph in range(P) for i in range(nsub)}
    for (ph, i, kg, g), (p, den, m) in zip(units, probs):
        qg = kg * G + g
        vwin = vfull[ph, i * Q:i * Q + win, kg * LANES:(kg + 1) * LANES]
        o = _dot(p, vwin) / den
        o_ref[ph, i * Q:(i + 1) * Q, qg * LANES:(qg + 1) * LANES] = jnp.where(lo, o[:Q], o[Q:]).astype(BF16)
        if want_lse:
            lse = m + jnp.log(den)
            tile = jnp.where(lane8 == 2 * qg, lse[:Q], lse_tiles[ph, i])
            lse_tiles[ph, i] = jnp.where(lane8 == 2 * qg + 1, lse[Q:], tile)
    if want_lse:
        for (ph, i), tile in lse_tiles.items():
            lse_ref[ph, i * Q:(i + 1) * Q, :] = tile


def _band_bias(slopes, step, W, Q):
    row = np.arange(Q)[:, None]
    col = np.arange(Q + 2 * W)[None, :]
    dist = np.abs(row + W - col)
    bias = -np.asarray(slopes, np.float32)[:, None, None] * (step * dist).astype(np.float32)[None]
    bias = np.where(dist[None] <= W, bias, np.float32(NEG)).astype(np.float32)
    return jnp.asarray(bias.reshape(len(slopes) // 2, 2 * Q, Q + 2 * W))


def _alibi_slopes(n):
    return (2.0 ** (-8.0 * np.arange(1, n + 1, dtype=np.float32) / n)).astype(np.float32)


BAND_ROWS = 512


def _band_tiles(Ls, W):
    R = min(Ls, BAND_ROWS)
    Q = min(R, 128)
    assert Ls % R == 0 and R % Q == 0 and R % W == 0 and Ls % W == 0, (Ls, R, Q, W)
    return R, Q


def _banded_attention(q, k, v, slopes, *, W, n_kg, G, sink=None, want_lse=False):
    bsz, r, Ls, Cq = q.shape
    Ck = k.shape[-1]
    R, Q = _band_tiles(Ls, W)
    nchunks = Ls // R
    nblk = Ls // W
    per = R // W
    P = math.gcd(r, max(1, BAND_ROWS // R))
    bias = _band_bias(slopes, r, W, Q)
    q_spec = pl.BlockSpec((None, P, R, Cq), lambda b, j, c: (b, j, c, 0))
    main = pl.BlockSpec((None, P, R, Ck), lambda b, j, c: (b, j, c, 0))
    prev = pl.BlockSpec((None, P, W, Ck), lambda b, j, c: (b, j, jnp.maximum(c * per - 1, 0), 0))
    nxt = pl.BlockSpec((None, P, W, Ck), lambda b, j, c: (b, j, jnp.minimum((c + 1) * per, nblk - 1), 0))
    in_specs = [q_spec, prev, main, nxt, prev, main, nxt, _const_spec(bias.shape)]
    args = [q, k, k, k, v, v, v, bias]
    if sink is not None:
        in_specs.append(pl.BlockSpec(memory_space=pltpu.SMEM))
        args.append(sink)
    out_specs = [q_spec]
    out_shape = [jax.ShapeDtypeStruct(q.shape, BF16)]
    if want_lse:
        out_specs.append(pl.BlockSpec((None, P, R, LSE_LANES), lambda b, j, c: (b, j, c, 0)))
        out_shape.append(jax.ShapeDtypeStruct((bsz, r, Ls, LSE_LANES), F32))
    kern = functools.partial(_banded_kernel, W=W, Q=Q, R=R, n_kg=n_kg, G=G, has_sink=sink is not None,
                             want_lse=want_lse, nchunks=nchunks)
    outs = pl.pallas_call(
        kern,
        grid=(bsz, r // P, nchunks),
        in_specs=in_specs,
        out_specs=out_specs,
        out_shape=out_shape,
        scratch_shapes=[pltpu.VMEM((P, R + 2 * W, Ck), BF16), pltpu.VMEM((P, R + 2 * W, Ck), BF16)],
        compiler_params=_params("parallel", "parallel", "parallel"),
        name=f"banded_w{W}_r{r}",
    )(*args)
    return (outs[0], outs[1]) if want_lse else outs[0]


VT_ROWS = 80
FLASH_STRIP = 256
FLASH_UNROLL = 4


def _flash_kernel(q_ref, k_ref, vt_ref, o_ref, *, tk, nk):
    tq = q_ref.shape[1]
    ns = tq // FLASH_STRIP
    chains = [(h, c) for h in range(2) for c in range(ns)]
    qs = [q_ref[h, c * FLASH_STRIP:(c + 1) * FLASH_STRIP, :] for h, c in chains]

    def body(t, carry):
        off = pl.multiple_of(t * tk, tk)
        kt = [k_ref[h, pl.ds(off, tk), :] for h in range(2)]
        vt = [vt_ref[h, :, pl.ds(off, tk)] for h in range(2)]
        ss = [_dot_nt(kt[h], qs[n]) for n, (h, _) in enumerate(chains)]
        ms = [jnp.maximum(carry[2 * n], jnp.max(ss[n], axis=0, keepdims=True)) for n in range(len(chains))]
        ps = [jnp.exp2(ss[n] - ms[n]).astype(BF16) for n in range(len(chains))]
        out = []
        for n, (h, _) in enumerate(chains):
            alpha = jnp.exp2(carry[2 * n] - ms[n])
            out += [ms[n], carry[2 * n + 1] * alpha + _dot(vt[h], ps[n])]
        return tuple(out)

    m0 = jnp.full((1, FLASH_STRIP), NEG, F32)
    acc0 = jnp.zeros((VT_ROWS, FLASH_STRIP), F32)
    res = lax.fori_loop(0, nk, body, (m0, acc0) * len(chains), unroll=FLASH_UNROLL)
    accs = [jnp.concatenate([res[2 * (h * ns + c) + 1] for c in range(ns)], axis=1) for h in range(2)]
    outs = [acc[:B_V] / acc[B_V:B_V + 1] for acc in accs]
    o_ref[...] = jnp.concatenate(outs, axis=0).T.astype(BF16)


def _flash_tiles(L):
    tq = min(L, 1024)
    tk = min(L, 512)
    assert L % tq == 0 and L % tk == 0
    return tq, tk


def _latent_attention(qb, kb, vt, bsz, L):
    tq, tk = _flash_tiles(L)
    q4 = qb.reshape(B_HEADS, bsz, L, LANES)
    k4 = kb.reshape(B_HEADS, bsz, L, LANES)
    k_spec = pl.BlockSpec((2, None, L, LANES), lambda b, hp, i: (hp, b, 0, 0))
    vt_spec = pl.BlockSpec((2, VT_ROWS, L), lambda b, hp, i: (hp, 0, b))
    out = pl.pallas_call(
        functools.partial(_flash_kernel, tk=tk, nk=L // tk),
        grid=(bsz, B_HEADS // 2, L // tq),
        in_specs=[pl.BlockSpec((2, None, tq, LANES), lambda b, hp, i: (hp, b, i, 0)), k_spec, vt_spec],
        out_specs=pl.BlockSpec((None, tq, LANES), lambda b, hp, i: (b, i, hp)),
        out_shape=jax.ShapeDtypeStruct((bsz, L, B_WIDTH), BF16),
        compiler_params=_params("parallel", "parallel", "parallel"),
        name="latent_flash",
    )(q4, k4, vt)
    return out.reshape(bsz * L, B_WIDTH)


def _mix_out_kernel(*refs):
    it = iter(refs)
    x_ref, ya_ref, yb_ref = next(it), next(it), next(it)
    c_refs = [next(it) for _ in STRIDES]
    l_refs = [next(it) for _ in STRIDES]
    ga_ref, gb_ref, gc_ref, w_ref = next(it), next(it), next(it), next(it)
    kv_ref, gx_ref, wq_ref, wo_ref = next(it), next(it), next(it), next(it)
    o_ref = next(it)
    c_scr = next(it)
    tm = x_ref.shape[0]
    ngrp = C_WIDTH // LANES

    for n, r in enumerate(STRIDES):
        for j in range(r):
            blk = c_refs[n][j].astype(F32)
            for g in range(ngrp):
                c_scr[n * ngrp + g, pl.ds(j, tm // r, stride=r), :] = blk[:, g * LANES:(g + 1) * LANES]

    ls = [l_ref[...] for l_ref in l_refs]
    mx = functools.reduce(jnp.maximum, ls)
    es = [jnp.exp(l - mx) for l in ls]
    tot = functools.reduce(lambda a, b: a + b, es)
    head_of_lane = lax.broadcasted_iota(jnp.int32, (LSE_LANES, C_WIDTH), 1) // HEAD_DIM
    spread = (head_of_lane == lax.broadcasted_iota(jnp.int32, (LSE_LANES, C_WIDTH), 0)).astype(BF16)
    wide = []
    for e in es:
        w = e / tot
        hi = w.astype(BF16)
        lo = (w - hi.astype(F32)).astype(BF16)
        wide.append(_dot(hi, spread) + _dot(lo, spread))
    groups = []
    for kg in range(ngrp):
        acc = jnp.zeros((tm, LANES), F32)
        for n, w in enumerate(wide):
            acc = acc + c_scr[n * ngrp + kg] * w[:, kg * LANES:(kg + 1) * LANES]
        groups.append(acc)
    yc = jnp.concatenate(groups, axis=-1)
    y = jnp.concatenate([
        _rms(ya_ref[...].astype(F32), ga_ref[...]).astype(BF16),
        _rms(yb_ref[...].astype(F32), gb_ref[...]).astype(BF16),
        _rms(yc, gc_ref[...]).astype(BF16)], axis=-1)
    x = x_ref[...] + _dot(y, w_ref[...])
    o_ref[...] = _cross_attention(x, kv_ref, gx_ref, wq_ref, wo_ref)


def _mix_out(x2, ya, yb, ycs, lses, kvm, lw, L, tm):
    T = x2.shape[0]
    nl = L // tm
    row = lambda w: pl.BlockSpec((tm, w), lambda i: (i, 0))
    weights = (lw['g_out_a'], lw['g_out_b'], lw['g_out_c'], lw['w_out'])
    x_weights = (lw['g_x'], lw['w_xq'], lw['w_xo'])
    n = len(STRIDES)
    return pl.pallas_call(
        _mix_out_kernel,
        grid=(T // tm,),
        in_specs=([row(D_MODEL), row(A_WIDTH), row(B_WIDTH)]
                  + [_phase_spec(r, tm, nl, C_WIDTH) for r in STRIDES]
                  + [row(LSE_LANES)] * n
                  + [_const_spec(w.shape) for w in weights]
                  + [pl.BlockSpec((None,) + kvm.shape[1:], lambda i: (i // nl, 0, 0))]
                  + [_const_spec(w.shape) for w in x_weights]),
        out_specs=row(D_MODEL),
        out_shape=jax.ShapeDtypeStruct((T, D_MODEL), F32),
        scratch_shapes=[pltpu.VMEM((n * C_WIDTH // LANES, tm, LANES), F32)],
        compiler_params=_params("parallel"),
        name="mix_out_cross",
    )(x2, ya, yb, *ycs, *lses, *weights, kvm, *x_weights)


def _norm_matmul_kernel(x_ref, g_ref, w_ref, o_ref):
    o_ref[...] = _dot(_rms(x_ref[...], g_ref[...]).astype(BF16), w_ref[...]).astype(o_ref.dtype)


def _norm_matmul(x2, g, w, tm):
    T, K = x2.shape
    N = w.shape[1]
    return pl.pallas_call(
        _norm_matmul_kernel,
        grid=(T // tm,),
        in_specs=[pl.BlockSpec((tm, K), lambda i: (i, 0)), _const_spec(g.shape), _const_spec(w.shape)],
        out_specs=pl.BlockSpec((tm, N), lambda i: (i, 0)),
        out_shape=jax.ShapeDtypeStruct((T, N), BF16),
        compiler_params=_params("parallel"),
        name="norm_matmul",
    )(x2, g, w)


def _cross_attention(x, kv_ref, g_ref, wq_ref, wo_ref):
    h = _rms(x, g_ref[...]).astype(BF16)
    q = (_dot(h, wq_ref[...]) * (X_HEAD_DIM ** -0.5)).astype(BF16)
    xd = X_HEADS * X_HEAD_DIM
    outs = []
    for hd in range(X_HEADS):
        sl = slice(hd * X_HEAD_DIM, (hd + 1) * X_HEAD_DIM)
        k = kv_ref[:, sl]
        v = kv_ref[:, xd + hd * X_HEAD_DIM:xd + (hd + 1) * X_HEAD_DIM]
        s = _dot_nt(q[:, sl], k)
        p = jnp.exp(s - jnp.max(s, axis=-1, keepdims=True))
        den = jnp.sum(p, axis=-1, keepdims=True)
        outs.append((_dot(p.astype(BF16), v) / den).astype(BF16))
    return x + _dot(jnp.concatenate(outs, axis=-1), wo_ref[...])


def _silu_mul(gate, up):
    return gate * jax.nn.sigmoid(gate) * up


def _swiglu_kernel(x_ref, g_ref, wg_ref, wu_ref, wd_ref, o_ref, h_scr, acc_scr):
    j = pl.program_id(1)

    @pl.when(j == 0)
    def _():
        h_scr[...] = _rms(x_ref[...], g_ref[...]).astype(BF16)
        acc_scr[...] = jnp.zeros_like(acc_scr)

    h = h_scr[...]
    act = _silu_mul(_dot(h, wg_ref[...]), _dot(h, wu_ref[...])).astype(BF16)
    acc_scr[...] += _dot(act, wd_ref[...])

    @pl.when(j == pl.num_programs(1) - 1)
    def _():
        o_ref[...] = x_ref[...] + acc_scr[...]


def _ff_chunk(dff, limit):
    for c in (1792, 1408, 1024, 512, 256, 128):
        if c <= limit and dff % c == 0:
            return c
    raise ValueError(dff)


def _swiglu(x2, g, wg, wu, wd, tm):
    T = x2.shape[0]
    dff = wg.shape[1]
    ck = _ff_chunk(dff, 1408)
    return pl.pallas_call(
        _swiglu_kernel,
        grid=(T // tm, dff // ck),
        in_specs=[pl.BlockSpec((tm, D_MODEL), lambda i, j: (i, 0)),
                  _const_spec(g.shape),
                  pl.BlockSpec((D_MODEL, ck), lambda i, j: (0, j)),
                  pl.BlockSpec((D_MODEL, ck), lambda i, j: (0, j)),
                  pl.BlockSpec((ck, D_MODEL), lambda i, j: (j, 0))],
        out_specs=pl.BlockSpec((tm, D_MODEL), lambda i, j: (i, 0)),
        out_shape=jax.ShapeDtypeStruct((T, D_MODEL), F32),
        scratch_shapes=[pltpu.VMEM((tm, D_MODEL), BF16), pltpu.VMEM((tm, D_MODEL), F32)],
        compiler_params=_params("parallel", "arbitrary"),
        name="swiglu_dense",
    )(x2, g, wg, wu, wd)


def _router_logits(h, wr):
    return jnp.dot(h, wr, preferred_element_type=F32, precision=lax.Precision.HIGHEST)


def _router_kernel(x_ref, g_ref, wr_ref, tri_ref, c0_ref, route_ref, cnt_ref, run_scr):
    @pl.when(pl.program_id(0) == 0)
    def _():
        run_scr[...] = c0_ref[...]

    logits = _router_logits(_rms(x_ref[...], g_ref[...]), wr_ref[...])
    tm = logits.shape[0]
    lane = lax.broadcasted_iota(jnp.int32, (tm, LANES), 1)
    logits = jnp.where(lane < N_EXPERTS, logits, NEG)
    v1 = jnp.max(logits, axis=-1, keepdims=True)
    i1 = jnp.min(jnp.where(logits == v1, lane, LANES), axis=-1, keepdims=True)
    rest = jnp.where(lane == i1, NEG, logits)
    v2 = jnp.max(rest, axis=-1, keepdims=True)
    i2 = jnp.min(jnp.where(rest == v2, lane, LANES), axis=-1, keepdims=True)
    pid = jnp.minimum(i1, i2) * N_EXPERTS + jnp.maximum(i1, i2)
    onehot = (lane == pid).astype(F32)
    before = _dot(tri_ref[...], onehot.astype(BF16)) + run_scr[...]
    rank = jnp.sum(onehot * before, axis=-1, keepdims=True)
    run_scr[...] += jnp.sum(onehot, axis=0, keepdims=True)
    cnt_ref[...] = run_scr[...]
    lane8 = lax.broadcasted_iota(jnp.int32, (tm, LSE_LANES), 1)
    route_ref[...] = jnp.where(lane8 == 0, pid, jnp.where(lane8 == 1, rank.astype(jnp.int32), 0))


def _router(x2, g, wr, counts0, tm):
    T = x2.shape[0]
    tri = (np.arange(tm)[:, None] > np.arange(tm)[None, :]).astype(np.float32)
    tri = jnp.asarray(tri, BF16)
    return pl.pallas_call(
        _router_kernel,
        grid=(T // tm,),
        in_specs=[pl.BlockSpec((tm, D_MODEL), lambda i: (i, 0)), _const_spec(g.shape), _const_spec(wr.shape),
                  _const_spec(tri.shape), _const_spec(counts0.shape)],
        out_specs=[pl.BlockSpec((tm, LSE_LANES), lambda i: (i, 0)), _const_spec(counts0.shape)],
        out_shape=[jax.ShapeDtypeStruct((T, LSE_LANES), jnp.int32), jax.ShapeDtypeStruct(counts0.shape, F32)],
        scratch_shapes=[pltpu.VMEM(counts0.shape, F32)],
        compiler_params=_params("arbitrary"),
        name="router",
    )(x2, g, wr, tri, counts0)


ROW_SUBLANES = D_MODEL // LANES


def _slab_to_rows(ref, n):
    return jnp.concatenate([ref[pl.ds(c, n, stride=ROW_SUBLANES), :] for c in range(ROW_SUBLANES)], axis=-1)


def _rows_to_slab(ref, rows, n):
    for c in range(ROW_SUBLANES):
        ref[pl.ds(c, n, stride=ROW_SUBLANES), :] = rows[:, c * LANES:(c + 1) * LANES]


def _slab(ref, row):
    return ref.at[pl.ds(pl.multiple_of(row * ROW_SUBLANES, ROW_SUBLANES), ROW_SUBLANES)]


MOVE_UNROLL = 8


def _move_rows(n, copy_of_row, copy_of_all):
    def issue(r, _):
        copy_of_row(r).start()
        return 0

    lax.fori_loop(0, n, issue, 0, unroll=MOVE_UNROLL)
    copy_of_all().wait()


def _dispatch_kernel(slot_ref, x_ref, prev_ref, o_ref, slab, sem, *, tm):
    del prev_ref
    _rows_to_slab(slab, x_ref[...], tm)

    def copy_of_row(r):
        slot = slot_ref[0, r]
        return pltpu.make_async_copy(_slab(slab, r), _slab(o_ref, slot), sem)

    def copy_of_all():
        return pltpu.make_async_copy(slab, o_ref.at[pl.ds(0, tm * ROW_SUBLANES)], sem)

    _move_rows(tm, copy_of_row, copy_of_all)


def _slots_kernel(route_ref, start_ref, o_ref):
    route = route_ref[...]
    tm = route.shape[0]
    lane = lax.broadcasted_iota(jnp.int32, (tm, LANES), 1)
    first = jnp.sum(jnp.where(lane == route[:, 0:1], start_ref[...], 0), axis=-1, keepdims=True)
    o_ref[...] = jnp.broadcast_to(first + route[:, 1:2], o_ref.shape)


def _slots(route, start, tm):
    T = route.shape[0]
    slots = pl.pallas_call(
        _slots_kernel,
        grid=(T // tm,),
        in_specs=[pl.BlockSpec((tm, LSE_LANES), lambda i: (i, 0)), _const_spec(start.shape)],
        out_specs=pl.BlockSpec((tm, LSE_LANES), lambda i: (i, 0)),
        out_shape=jax.ShapeDtypeStruct((T, LSE_LANES), jnp.int32),
        compiler_params=_params("parallel"),
        name="slots",
    )(route, start)
    return slots[:, 0].reshape(T // tm, 1, tm)


def _dispatch(x2, slots, sorted_rows, tm):
    T = x2.shape[0]
    return pl.pallas_call(
        functools.partial(_dispatch_kernel, tm=tm),
        grid=(T // tm,),
        in_specs=[pl.BlockSpec((None, 1, tm), lambda i: (i, 0, 0), memory_space=pltpu.SMEM),
                  pl.BlockSpec((tm, D_MODEL), lambda i: (i, 0)),
                  pl.BlockSpec(memory_space=pl.ANY)],
        out_specs=pl.BlockSpec(memory_space=pl.ANY),
        out_shape=jax.ShapeDtypeStruct(sorted_rows.shape, sorted_rows.dtype),
        scratch_shapes=[pltpu.VMEM((tm * ROW_SUBLANES, LANES), F32), pltpu.SemaphoreType.DMA],
        input_output_aliases={2: 0},
        compiler_params=pltpu.CompilerParams(dimension_semantics=("arbitrary",), vmem_limit_bytes=VMEM_LIMIT,
                                             has_side_effects=True),
        name="dispatch",
    )(slots, x2, sorted_rows)


def _unpermute_kernel(slot_ref, y_ref, o_ref, slab, sem, *, tm):
    def copy_of_row(r):
        slot = slot_ref[0, r]
        return pltpu.make_async_copy(_slab(y_ref, slot), _slab(slab, r), sem)

    def copy_of_all():
        return pltpu.make_async_copy(y_ref.at[pl.ds(0, tm * ROW_SUBLANES)], slab, sem)

    _move_rows(tm, copy_of_row, copy_of_all)
    o_ref[...] = _slab_to_rows(slab, tm)


def _unpermute(y_sorted, slots, T, tm):
    return pl.pallas_call(
        functools.partial(_unpermute_kernel, tm=tm),
        grid=(T // tm,),
        in_specs=[pl.BlockSpec((None, 1, tm), lambda i: (i, 0, 0), memory_space=pltpu.SMEM),
                  pl.BlockSpec(memory_space=pl.ANY)],
        out_specs=pl.BlockSpec((tm, D_MODEL), lambda i: (i, 0)),
        out_shape=jax.ShapeDtypeStruct((T, D_MODEL), F32),
        scratch_shapes=[pltpu.VMEM((tm * ROW_SUBLANES, LANES), F32), pltpu.SemaphoreType.DMA],
        compiler_params=_params("arbitrary"),
        name="unpermute",
    )(slots, y_sorted)


def _pair_experts_kernel(ta_ref, tb_ref, nv_ref, xs_ref, g_ref, wr_ref, wg_ref, wu_ref, wd_ref, gf_ref, o_ref,
                         x_scr, h_scr, gate_scr, acc_scr, *, nj, final_norm):
    i = pl.program_id(0)
    j = pl.program_id(1)
    tm = x_scr.shape[0]

    @pl.when(i < nv_ref[0])
    def _():
        @pl.when(j == 0)
        def _():
            x = _slab_to_rows(xs_ref, tm)
            x_scr[...] = x
            h = _rms(x, g_ref[...])
            h_scr[...] = h.astype(BF16)
            logits = _router_logits(h, wr_ref[...])
            lane = lax.broadcasted_iota(jnp.int32, logits.shape, 1)
            la = jnp.sum(jnp.where(lane == ta_ref[i], logits, 0.0), axis=-1, keepdims=True)
            lb = jnp.sum(jnp.where(lane == tb_ref[i], logits, 0.0), axis=-1, keepdims=True)
            mx = jnp.maximum(la, lb)
            ea, eb = jnp.exp(la - mx), jnp.exp(lb - mx)
            gate_scr[0] = ea / (ea + eb)
            gate_scr[1] = eb / (ea + eb)
            acc_scr[...] = jnp.zeros_like(acc_scr)

        h = h_scr[...]
        gate = jnp.where(j < nj, gate_scr[0], gate_scr[1])
        act = (_silu_mul(_dot(h, wg_ref[...]), _dot(h, wu_ref[...])) * gate).astype(BF16)
        acc_scr[...] += _dot(act, wd_ref[...])

        @pl.when(j == 2 * nj - 1)
        def _():
            out = x_scr[...] + acc_scr[...]
            if final_norm:
                out = _rms(out, gf_ref[...])
            _rows_to_slab(o_ref, out, tm)

    @pl.when(jnp.logical_and(i >= nv_ref[0], j == 2 * nj - 1))
    def _():
        o_ref[...] = jnp.zeros_like(o_ref)


def _pair_experts(xs, g, wr, wg, wu, wd, g_final, tile_a, tile_b, n_valid, tm, final_norm):
    rows = xs.shape[0] // ROW_SUBLANES
    dff = wg.shape[2]
    ck = _ff_chunk(dff, 1792)
    nj = dff // ck

    def expert(i, j, ta, tb):
        return jnp.where(j < nj, ta[i], tb[i])

    def chunk(i, j, nv):
        return jnp.where(i < nv[0], j % nj, nj - 1)

    slab_spec = pl.BlockSpec((tm * ROW_SUBLANES, LANES), lambda i, j, ta, tb, nv: (i, 0))
    const = lambda a: pl.BlockSpec(a.shape, lambda i, j, ta, tb, nv: (0,) * a.ndim)
    grid_spec = pltpu.PrefetchScalarGridSpec(
        num_scalar_prefetch=3,
        grid=(rows // tm, 2 * nj),
        in_specs=[slab_spec, const(g), const(wr),
                  pl.BlockSpec((None, D_MODEL, ck), lambda i, j, ta, tb, nv: (expert(i, j, ta, tb), 0, chunk(i, j, nv))),
                  pl.BlockSpec((None, D_MODEL, ck), lambda i, j, ta, tb, nv: (expert(i, j, ta, tb), 0, chunk(i, j, nv))),
                  pl.BlockSpec((None, ck, D_MODEL), lambda i, j, ta, tb, nv: (expert(i, j, ta, tb), chunk(i, j, nv), 0)),
                  const(g_final)],
        out_specs=slab_spec,
        scratch_shapes=[pltpu.VMEM((tm, D_MODEL), F32), pltpu.VMEM((tm, D_MODEL), BF16),
                        pltpu.VMEM((2, tm, 1), F32), pltpu.VMEM((tm, D_MODEL), F32)],
    )
    return pl.pallas_call(
        functools.partial(_pair_experts_kernel, nj=nj, final_norm=final_norm),
        grid_spec=grid_spec,
        out_shape=jax.ShapeDtypeStruct(xs.shape, F32),
        compiler_params=_params("parallel", "arbitrary"),
        name="pair_experts",
    )(tile_a, tile_b, n_valid, xs, g, wr, wg, wu, wd, g_final)


def _plan_pairs(counts, n_tiles, tm):
    counts = counts.reshape(-1).astype(jnp.int32)
    tiles = (counts + tm - 1) // tm
    tile_end = jnp.cumsum(tiles)
    start = (tile_end - tiles) * tm
    n_valid = tile_end[-1:]
    tile_ids = jnp.minimum(jnp.arange(n_tiles), n_valid[0] - 1)
    tile_pid = jnp.sum(tile_ids[:, None] >= tile_end[None, :], axis=-1)
    tile_a = (tile_pid // N_EXPERTS).astype(jnp.int32)
    tile_b = (tile_pid % N_EXPERTS).astype(jnp.int32)
    return start.astype(jnp.int32), tile_a, tile_b, n_valid.astype(jnp.int32)


def _prep_layer(l, p):
    offs = np.cumsum((0,) + IN_SIZES)
    w_in = p['w_in'][l]
    qa, ka, va, cq, ckv, kr, qc, kc, vc = [w_in[:, int(offs[i]):int(offs[i + 1])] for i in range(9)]
    scale = HEAD_DIM ** -0.5
    order = list(A_HEAD_ORDER)
    qa = qa.reshape(D_MODEL, A_Q_HEADS, HEAD_DIM)[:, order, :].reshape(D_MODEL, A_WIDTH) * scale
    half = B_ROPE // 2
    z = lambda *s: jnp.zeros(s, F32)
    kr_pad = jnp.concatenate([z(D_MODEL, B_NOPE), kr, z(D_MODEL, LANES - B_NOPE - B_ROPE)], axis=1)
    kr_rot = jnp.concatenate([z(D_MODEL, B_NOPE), -kr[:, half:], kr[:, :half], z(D_MODEL, LANES - B_NOPE - B_ROPE)], axis=1)
    uq = p['mla_w_uq'][l].reshape(B_Q_LORA, B_HEADS, B_NOPE + B_ROPE)
    padq = z(B_Q_LORA, B_HEADS, LANES - B_NOPE - B_ROPE)
    uq_pad = jnp.concatenate([uq, padq], axis=-1).reshape(B_Q_LORA, B_HEADS * LANES)
    uq_rot = jnp.concatenate([z(B_Q_LORA, B_HEADS, B_NOPE), -uq[..., B_NOPE + half:], uq[..., B_NOPE:B_NOPE + half], padq],
                             axis=-1).reshape(B_Q_LORA, B_HEADS * LANES)
    ukv = p['mla_w_ukv'][l].reshape(B_KV_LORA, B_HEADS, B_NOPE + B_V)
    zk = z(B_KV_LORA, B_HEADS, HALF)
    uk_pad = jnp.concatenate([ukv[..., :B_NOPE], zk], axis=-1).reshape(B_KV_LORA, B_HEADS * LANES)
    uv_pad = jnp.concatenate([ukv[..., B_NOPE:], zk], axis=-1).reshape(B_KV_LORA, B_HEADS * LANES)

    g_out = p['mix_out_norm_g'][l]
    w_out = p['w_out'][l]
    a_rows = np.concatenate([np.arange(HEAD_DIM) + HEAD_DIM * h for h in A_HEAD_ORDER])
    w_out = jnp.concatenate([w_out[:A_WIDTH][a_rows], w_out[A_WIDTH:]], axis=0)
    g_out_a = g_out[:A_WIDTH][a_rows]
    row = lambda v: v.reshape(1, -1).astype(F32)
    return {
        'g_mix': row(p['norm_mix_g'][l]),
        'wa': jnp.concatenate([qa, ka, va], axis=1).astype(BF16),
        'wc': jnp.concatenate([qc * scale, kc, vc], axis=1).astype(BF16),
        'wb': jnp.concatenate([cq, ckv], axis=1).astype(BF16),
        'wkr': jnp.concatenate([kr_pad, kr_rot], axis=1).astype(BF16),
        'g_q': row(p['mla_q_norm_g'][l]),
        'g_kv': row(p['mla_kv_norm_g'][l]),
        'wuq': jnp.concatenate([uq_pad, uq_rot], axis=1).astype(BF16),
        'wukv': jnp.concatenate([uk_pad, uv_pad], axis=1).astype(BF16),
        'sink': p['sink_a'][l][np.array(A_HEAD_ORDER)].astype(F32),
        'g_out_a': row(g_out_a),
        'g_out_b': row(g_out[A_WIDTH:A_WIDTH + B_WIDTH]),
        'g_out_c': row(g_out[A_WIDTH + B_WIDTH:]),
        'w_out': w_out.astype(BF16),
        'g_x': row(p['norm_x_g'][l]),
        'g_mem': row(p['norm_mem_g'][l]),
        'w_xq': p['w_xq'][l].astype(BF16),
        'w_xkv': p['w_xkv'][l].astype(BF16),
        'w_xo': p['w_xo'][l].astype(BF16),
        'g_ffn': row(p['norm_ffn_g'][l]),
    }


def _prep_ffn(w_gu, w_down):
    dff = w_down.shape[-2]
    return w_gu[..., :dff].astype(BF16), w_gu[..., dff:].astype(BF16), w_down.astype(BF16)


def _rope_tables(L):
    half = B_ROPE // 2
    pos = jnp.arange(L, dtype=F32)
    inv = ROPE_THETA ** (-jnp.arange(half, dtype=F32) / half)
    ang = pos[:, None] * inv[None, :]
    cos2 = jnp.tile(jnp.cos(ang), (1, 2))
    sin2 = jnp.tile(jnp.sin(ang), (1, 2))
    pad = jnp.zeros((L, LANES - B_NOPE - B_ROPE), F32)
    qscale = (B_NOPE + B_ROPE) ** -0.5 * LOG2E
    cq = jnp.concatenate([jnp.ones((L, B_NOPE), F32), cos2, pad], axis=1) * qscale
    sq = jnp.concatenate([jnp.zeros((L, B_NOPE), F32), sin2, pad], axis=1) * qscale
    ck = jnp.concatenate([jnp.zeros((L, B_NOPE), F32), cos2, pad], axis=1)
    sk = jnp.concatenate([jnp.zeros((L, B_NOPE), F32), sin2, pad], axis=1)
    return cq, sq, ck, sk


TOKEN_TILE = 512
EXPERT_TILE = 512
FFN_TILE = 1024
MOVE_TILE = 1024
N_PAIRS_USED = N_EXPERTS * (N_EXPERTS - 1) // 2


def _mixer_and_cross(x2, mem, lw, bsz, L):
    T = bsz * L
    tm = min(TOKEN_TILE, L)
    tabs = _rope_tables(L)
    slopes_a = _alibi_slopes(A_Q_HEADS)[list(A_HEAD_ORDER)]
    slopes_c = _alibi_slopes(C_HEADS)
    (qa, ka, va), qkv_c, (qb, kb, vb) = _in_proj(x2, lw, tabs, bsz, L, tm)
    as_seq = lambda t: t.reshape(bsz, 1, L, t.shape[-1])
    ya = _banded_attention(as_seq(qa), as_seq(ka), as_seq(va), slopes_a, W=A_WINDOW, n_kg=1, G=2, sink=lw['sink'])
    ya = ya.reshape(T, A_WIDTH)
    yb = _latent_attention(qb, kb, vb, bsz, L)
    ycs, lses = [], []
    for (w, r), (qc, kc, vc) in zip(DILATED_PAIRS, qkv_c):
        o, lse = _banded_attention(qc, kc, vc, slopes_c, W=w // (2 * r), n_kg=C_HEADS // 2, G=1, want_lse=True)
        ycs.append(o)
        lses.append(jnp.transpose(lse, (0, 2, 1, 3)).reshape(T, LSE_LANES))
    mem2 = mem.reshape(-1, D_MODEL)
    kvm = _norm_matmul(mem2, lw['g_mem'], lw['w_xkv'], min(512, mem2.shape[0]))
    return _mix_out(x2, ya, yb, ycs, lses, kvm.reshape(bsz, mem.shape[1], -1), lw, L, tm)


def _routed_swiglu(xs2, g, wr, wg, wu, wd, final_g, final_norm):
    total = sum(x.shape[0] for x in xs2)
    et = min(EXPERT_TILE, total)
    n_tiles = total // et + N_PAIRS_USED
    counts = jnp.zeros((1, LANES), F32)
    routes = []
    for x2 in xs2:
        route, counts = _router(x2, g, wr, counts, min(TOKEN_TILE, x2.shape[0]))
        routes.append(route)
    start, tile_a, tile_b, n_valid = _plan_pairs(counts, n_tiles, et)
    move = [min(MOVE_TILE, x2.shape[0]) for x2 in xs2]
    slots = [_slots(route, start.reshape(1, LANES), mt) for route, mt in zip(routes, move)]
    sorted_rows = jnp.zeros((n_tiles * et * ROW_SUBLANES, LANES), F32)
    for x2, sl, mt in zip(xs2, slots, move):
        sorted_rows = _dispatch(x2, sl, sorted_rows, mt)
    y_sorted = _pair_experts(sorted_rows, g, wr, wg, wu, wd, final_g, tile_a, tile_b, n_valid, et, final_norm)
    return [_unpermute(y_sorted, sl, x2.shape[0], mt) for x2, sl, mt in zip(xs2, slots, move)]


def _encoder(groups, layers, ffn, moe, routers, final_g):
    shapes = [x.shape for x, _ in groups]
    xs2 = [x.reshape(-1, D_MODEL) for x, _ in groups]
    depth = len(layers)
    normed = False
    for l, lw in enumerate(layers):
        xs2 = [_mixer_and_cross(x2, mem, lw, shp[0], shp[1]) for x2, (_, mem), shp in zip(xs2, groups, shapes)]
        if l % 2 == 0:
            wg, wu, wd = ffn[l // 2]
            xs2 = [_swiglu(x2, lw['g_ffn'], wg, wu, wd, min(FFN_TILE, x2.shape[0])) for x2 in xs2]
        else:
            wg, wu, wd = moe[l // 2]
            normed = l == depth - 1
            xs2 = _routed_swiglu(xs2, lw['g_ffn'], routers[l // 2], wg, wu, wd, final_g, final_norm=normed)
    assert normed, "the final rmsnorm is fused into the last layer's routed SwiGLU"
    return tuple(x2.reshape(shp) for x2, shp in zip(xs2, shapes))


def kernel(x_prompt, x_sample, mem_prompt, mem_sample, norm_mix_g, w_in, sink_a, mla_q_norm_g, mla_kv_norm_g, mla_w_uq, mla_w_ukv, mix_out_norm_g, w_out, norm_x_g, norm_mem_g, w_xq, w_xkv, w_xo, norm_ffn_g, ffn_w_gu, ffn_w_down, moe_router, moe_w_gu, moe_w_down, final_norm_g):
    p = dict(norm_mix_g=norm_mix_g, w_in=w_in, sink_a=sink_a, mla_q_norm_g=mla_q_norm_g, mla_kv_norm_g=mla_kv_norm_g,
             mla_w_uq=mla_w_uq, mla_w_ukv=mla_w_ukv, mix_out_norm_g=mix_out_norm_g, w_out=w_out, norm_x_g=norm_x_g,
             norm_mem_g=norm_mem_g, w_xq=w_xq, w_xkv=w_xkv, w_xo=w_xo, norm_ffn_g=norm_ffn_g)
    depth = w_in.shape[0]
    layers = [_prep_layer(l, p) for l in range(depth)]
    ffn = [_prep_ffn(ffn_w_gu[i], ffn_w_down[i]) for i in range(ffn_w_gu.shape[0])]
    moe = [_prep_ffn(moe_w_gu[i], moe_w_down[i]) for i in range(moe_w_gu.shape[0])]
    routers = [jnp.pad(moe_router[i].astype(F32), ((0, 0), (0, LANES - N_EXPERTS))) for i in range(moe_router.shape[0])]
    final_g = final_norm_g.reshape(1, -1).astype(F32)
    return _encoder([(x_prompt, mem_prompt), (x_sample, mem_sample)], layers, ffn, moe, routers, final_g)
```

```python
import functools
import math

import numpy as np
import jax
import jax.numpy as jnp
from jax import lax
from jax.experimental import pallas as pl
from jax.experimental.pallas import tpu as pltpu

D_MODEL = 1024
HEAD_DIM = 64
A_Q_HEADS = 4
A_KV_HEADS = 2
A_WINDOW = 128
B_HEADS = 6
B_NOPE = 64
B_ROPE = 32
B_V = 64
B_Q_LORA = 384
B_KV_LORA = 256
ROPE_THETA = 10000.0
C_HEADS = 6
DILATED_PAIRS = ((128, 1), (512, 4), (2048, 16))
STRIDES = tuple(r for _, r in DILATED_PAIRS)
A_WIDTH = A_Q_HEADS * HEAD_DIM
B_WIDTH = B_HEADS * B_V
C_WIDTH = C_HEADS * HEAD_DIM
IN_SIZES = (A_WIDTH, A_KV_HEADS * HEAD_DIM, A_KV_HEADS * HEAD_DIM, B_Q_LORA, B_KV_LORA, B_ROPE,
            C_WIDTH, C_WIDTH, C_WIDTH)
X_HEADS = 4
X_HEAD_DIM = 128
N_EXPERTS = 8
TOP_K = 2
EPS = 1e-6
NEG = -1e30

LANES = 128
HALF = LANES // 2
LSE_LANES = 8
VMEM_LIMIT = 56 * 1024 * 1024
LOG2E = math.log2(math.e)

BF16 = jnp.bfloat16
F32 = jnp.float32

A_HEAD_ORDER = (0, 2, 1, 3)


def _params(*sem):
    return pltpu.CompilerParams(dimension_semantics=sem, vmem_limit_bytes=VMEM_LIMIT)


def _rms(x, g):
    return x * lax.rsqrt(jnp.mean(x * x, axis=-1, keepdims=True) + EPS) * g


def _dot(a, b):
    return jnp.dot(a, b, preferred_element_type=F32)


def _dot_nt(a, b):
    return lax.dot_general(a, b, (((1,), (1,)), ((), ())), preferred_element_type=F32)


def _const_spec(shape):
    n = len(shape)
    return pl.BlockSpec(shape, lambda *_: (0,) * n)


def _phase_spec(r, tm, nl, width):
    return pl.BlockSpec((None, r, tm // r, width), lambda i: (i // nl, 0, i % nl, 0))


def _in_kernel(*refs):
    it = iter(refs)
    x_ref, g_ref, wa_ref, wc_ref, wb_ref, wkr_ref, gq_ref, gkv_ref, wuq_ref, wukv_ref = [next(it) for _ in range(10)]
    cq_ref, sq_ref, ck_ref, sk_ref = [next(it) for _ in range(4)]
    qa_ref, ka_ref, va_ref = next(it), next(it), next(it)
    c_refs = [(next(it), next(it), next(it)) for _ in STRIDES]
    qb_ref, kb_ref, vb_ref = next(it), next(it), next(it)
    zc_scr = next(it)
    tm = x_ref.shape[0]

    h = _rms(x_ref[...], g_ref[...]).astype(BF16)
    za = _dot(h, wa_ref[...])
    qa_ref[...] = za[:, :A_WIDTH].astype(BF16)
    ka_ref[...] = za[:, A_WIDTH:A_WIDTH + LANES].astype(BF16)
    va_ref[...] = za[:, A_WIDTH + LANES:].astype(BF16)

    zc = _dot(h, wc_ref[...])
    ngrp = C_WIDTH // LANES
    for g in range(3 * ngrp):
        zc_scr[g] = zc[:, g * LANES:(g + 1) * LANES]
    for r, qkv_refs in zip(STRIDES, c_refs):
        for j in range(r):
            for n, ref in enumerate(qkv_refs):
                ref[j] = jnp.concatenate(
                    [zc_scr[n * ngrp + g, pl.ds(j, tm // r, stride=r), :] for g in range(ngrp)], axis=-1).astype(BF16)

    zb = _dot(h, wb_ref[...])
    hq = _rms(zb[:, :B_Q_LORA], gq_ref[...]).astype(BF16)
    hkv = _rms(zb[:, B_Q_LORA:], gkv_ref[...]).astype(BF16)
    zq = _dot(hq, wuq_ref[...])
    zkv = _dot(hkv, wukv_ref[...])
    zkr = _dot(h, wkr_ref[...])
    kr = zkr[:, :LANES] * ck_ref[...] + zkr[:, LANES:] * sk_ref[...]
    cq = cq_ref[...]
    sq = sq_ref[...]
    lane = lax.broadcasted_iota(jnp.int32, (1, LANES), 1)
    nb = B_HEADS * LANES
    for hd in range(B_HEADS):
        lo, hi = hd * LANES, (hd + 1) * LANES
        qb_ref[hd] = (zq[:, lo:hi] * cq + zq[:, nb + lo:nb + hi] * sq).astype(BF16)
        kb_ref[hd] = (zkv[:, lo:hi] + kr).astype(BF16)
        ones = (lane == B_V).astype(F32)
        vb_ref[hd] = (zkv[:, nb + lo:nb + hi] + ones).T.astype(BF16)


def _in_proj(x2, lw, tabs, bsz, L, tm):
    T = x2.shape[0]
    nl = L // tm
    row = lambda w: pl.BlockSpec((tm, w), lambda i: (i, 0))
    tab = pl.BlockSpec((tm, LANES), lambda i: (i % nl, 0))
    hm = pl.BlockSpec((B_HEADS, tm, LANES), lambda i: (0, i, 0))
    weights = (lw['g_mix'], lw['wa'], lw['wc'], lw['wb'], lw['wkr'], lw['g_q'], lw['g_kv'], lw['wuq'], lw['wukv'])
    out_specs = [row(A_WIDTH), row(LANES), row(LANES)]
    out_shape = [jax.ShapeDtypeStruct((T, w), BF16) for w in (A_WIDTH, LANES, LANES)]
    for r in STRIDES:
        assert tm % (16 * r) == 0, (tm, r)
        out_specs += [_phase_spec(r, tm, nl, C_WIDTH)] * 3
        out_shape += [jax.ShapeDtypeStruct((bsz, r, L // r, C_WIDTH), BF16)] * 3
    out_specs += [hm, hm, pl.BlockSpec((B_HEADS, LANES, tm), lambda i: (0, 0, i))]
    out_shape += [jax.ShapeDtypeStruct((B_HEADS, T, LANES), BF16)] * 2
    out_shape += [jax.ShapeDtypeStruct((B_HEADS, LANES, T), BF16)]
    outs = pl.pallas_call(
        _in_kernel,
        grid=(T // tm,),
        in_specs=[row(D_MODEL)] + [_const_spec(w.shape) for w in weights] + [tab] * 4,
        out_specs=out_specs,
        out_shape=out_shape,
        scratch_shapes=[pltpu.VMEM((3 * C_WIDTH // LANES, tm, LANES), F32)],
        compiler_params=_params("parallel"),
        name="in_proj",
    )(x2, *weights, *tabs)
    qkv_c = [outs[3 + 3 * n:6 + 3 * n] for n in range(len(STRIDES))]
    return outs[:3], qkv_c, outs[-3:]


def _banded_kernel(*refs, W, Q, R, n_kg, G, has_sink, want_lse, nchunks):
    it = iter(refs)
    q_ref = next(it)
    kp_ref, km_ref, kn_ref = next(it), next(it), next(it)
    vp_ref, vm_ref, vn_ref = next(it), next(it), next(it)
    bias_ref = next(it)
    sink_ref = next(it) if has_sink else None
    o_ref = next(it)
    lse_ref = next(it) if want_lse else None
    kfull, vfull = next(it), next(it)

    c = pl.program_id(2)
    P = q_ref.shape[0]
    kfull[:, 0:W] = kp_ref[...]
    kfull[:, W:W + R] = km_ref[...]
    kfull[:, W + R:] = kn_ref[...]
    vfull[:, 0:W] = vp_ref[...]
    vfull[:, W:W + R] = vm_ref[...]
    vfull[:, W + R:] = vn_ref[...]

    nsub = R // Q
    win = Q + 2 * W
    lo = lax.broadcasted_iota(jnp.int32, (Q, LANES), 1) < HALF
    col = lax.broadcasted_iota(jnp.int32, (2 * Q, win), 1)
    top = lax.broadcasted_iota(jnp.int32, (2 * Q, 1), 0) < Q
    lane8 = lax.broadcasted_iota(jnp.int32, (Q, LSE_LANES), 1)
    before_start = jnp.logical_and(c == 0, col < W)
    after_end = jnp.logical_and(c == nchunks - 1, col >= Q + W)
    units = [(ph, i, kg, g) for ph in range(P) for i in range(nsub) for kg in range(n_kg) for g in range(G)]

    scores = []
    for ph, i, kg, g in units:
        qg = kg * G + g
        qblk = q_ref[ph, i * Q:(i + 1) * Q, qg * LANES:(qg + 1) * LANES]
        zero = jnp.zeros_like(qblk)
        qm = jnp.concatenate([jnp.where(lo, qblk, zero), jnp.where(lo, zero, qblk)], axis=0)
        kwin = kfull[ph, i * Q:i * Q + win, kg * LANES:(kg + 1) * LANES]
        s = _dot_nt(qm, kwin) + bias_ref[qg]
        if i == 0:
            s = jnp.where(before_start, NEG, s)
        if i == nsub - 1:
            s = jnp.where(after_end, NEG, s)
        scores.append(s)

    probs = []
    for (ph, i, kg, g), s in zip(units, scores):
        qg = kg * G + g
        m = jnp.max(s, axis=-1, keepdims=True)
        if has_sink:
            sk = jnp.where(top, sink_ref[2 * qg], sink_ref[2 * qg + 1])
            m = jnp.maximum(m, sk)
        p = jnp.exp(s - m)
        den = jnp.sum(p, axis=-1, keepdims=True)
        if has_sink:
            den = den + jnp.exp(sk - m)
        probs.append((p.astype(BF16), den, m))

    lse_tiles = {(ph, i): jnp.zeros((Q, LSE_LANES), F32) for ph in range(P) for i in range(nsub)}
    for (ph, i, kg, g), (p, den, m) in zip(units, probs):
        qg = kg * G + g
        vwin = vfull[ph, i * Q:i * Q + win, kg * LANES:(kg + 1) * LANES]
        o = _dot(p, vwin) / den
        o_ref[ph, i * Q:(i + 1) * Q, qg * LANES:(qg + 1) * LANES] = jnp.where(lo, o[:Q], o[Q:]).astype(BF16)
        if want_lse:
            lse = m + jnp.log(den)
            tile = jnp.where(lane8 == 2 * qg, lse[:Q], lse_tiles[ph, i])
            lse_tiles[ph, i] = jnp.where(lane8 == 2 * qg + 1, lse[Q:], tile)
    if want_lse:
        for (ph, i), tile in lse_tiles.items():
            lse_ref[ph, i * Q:(i + 1) * Q, :] = tile


def _band_bias(slopes, step, W, Q):
    row = np.arange(Q)[:, None]
    col = np.arange(Q + 2 * W)[None, :]
    dist = np.abs(row + W - col)
    bias = -np.asarray(slopes, np.float32)[:, None, None] * (step * dist).astype(np.float32)[None]
    bias = np.where(dist[None] <= W, bias, np.float32(NEG)).astype(np.float32)
    return jnp.asarray(bias.reshape(len(slopes) // 2, 2 * Q, Q + 2 * W))


def _alibi_slopes(n):
    return (2.0 ** (-8.0 * np.arange(1, n + 1, dtype=np.float32) / n)).astype(np.float32)


BAND_ROWS = 512


def _band_tiles(Ls, W):
    R = min(Ls, BAND_ROWS)
    Q = min(R, 128)
    assert Ls % R == 0 and R % Q == 0 and R % W == 0 and Ls % W == 0, (Ls, R, Q, W)
    return R, Q


def _banded_attention(q, k, v, slopes, *, W, n_kg, G, sink=None, want_lse=False):
    bsz, r, Ls, Cq = q.shape
    Ck = k.shape[-1]
    R, Q = _band_tiles(Ls, W)
    nchunks = Ls // R
    nblk = Ls // W
    per = R // W
    P = math.gcd(r, max(1, BAND_ROWS // R))
    bias = _band_bias(slopes, r, W, Q)
    q_spec = pl.BlockSpec((None, P, R, Cq), lambda b, j, c: (b, j, c, 0))
    main = pl.BlockSpec((None, P, R, Ck), lambda b, j, c: (b, j, c, 0))
    prev = pl.BlockSpec((None, P, W, Ck), lambda b, j, c: (b, j, jnp.maximum(c * per - 1, 0), 0))
    nxt = pl.BlockSpec((None, P, W, Ck), lambda b, j, c: (b, j, jnp.minimum((c + 1) * per, nblk - 1), 0))
    in_specs = [q_spec, prev, main, nxt, prev, main, nxt, _const_spec(bias.shape)]
    args = [q, k, k, k, v, v, v, bias]
    if sink is not None:
        in_specs.append(pl.BlockSpec(memory_space=pltpu.SMEM))
        args.append(sink)
    out_specs = [q_spec]
    out_shape = [jax.ShapeDtypeStruct(q.shape, BF16)]
    if want_lse:
        out_specs.append(pl.BlockSpec((None, P, R, LSE_LANES), lambda b, j, c: (b, j, c, 0)))
        out_shape.append(jax.ShapeDtypeStruct((bsz, r, Ls, LSE_LANES), F32))
    kern = functools.partial(_banded_kernel, W=W, Q=Q, R=R, n_kg=n_kg, G=G, has_sink=sink is not None,
                             want_lse=want_lse, nchunks=nchunks)
    outs = pl.pallas_call(
        kern,
        grid=(bsz, r // P, nchunks),
        in_specs=in_specs,
        out_specs=out_specs,
        out_shape=out_shape,
        scratch_shapes=[pltpu.VMEM((P, R + 2 * W, Ck), BF16), pltpu.VMEM((P, R + 2 * W, Ck), BF16)],
        compiler_params=_params("parallel", "parallel", "parallel"),
        name=f"banded_w{W}_r{r}",
    )(*args)
    return (outs[0], outs[1]) if want_lse else outs[0]


VT_ROWS = 80
FLASH_STRIP = 256
FLASH_UNROLL = 4


def _flash_kernel(q_ref, k_ref, vt_ref, o_ref, *, tk, nk):
    tq = q_ref.shape[1]
    ns = tq // FLASH_STRIP
    chains = [(h, c) for h in range(2) for c in range(ns)]
    qs = [q_ref[h, c * FLASH_STRIP:(c + 1) * FLASH_STRIP, :] for h, c in chains]

    def body(t, carry):
        off = pl.multiple_of(t * tk, tk)
        kt = [k_ref[h, pl.ds(off, tk), :] for h in range(2)]
        vt = [vt_ref[h, :, pl.ds(off, tk)] for h in range(2)]
        ss = [_dot_nt(kt[h], qs[n]) for n, (h, _) in enumerate(chains)]
        ms = [jnp.maximum(carry[2 * n], jnp.max(ss[n], axis=0, keepdims=True)) for n in range(len(chains))]
        ps = [jnp.exp2(ss[n] - ms[n]).astype(BF16) for n in range(len(chains))]
        out = []
        for n, (h, _) in enumerate(chains):
            alpha = jnp.exp2(carry[2 * n] - ms[n])
            out += [ms[n], carry[2 * n + 1] * alpha + _dot(vt[h], ps[n])]
        return tuple(out)

    m0 = jnp.full((1, FLASH_STRIP), NEG, F32)
    acc0 = jnp.zeros((VT_ROWS, FLASH_STRIP), F32)
    res = lax.fori_loop(0, nk, body, (m0, acc0) * len(chains), unroll=FLASH_UNROLL)
    accs = [jnp.concatenate([res[2 * (h * ns + c) + 1] for c in range(ns)], axis=1) for h in range(2)]
    outs = [acc[:B_V] / acc[B_V:B_V + 1] for acc in accs]
    o_ref[...] = jnp.concatenate(outs, axis=0).T.astype(BF16)


def _flash_tiles(L):
    tq = min(L, 1024)
    tk = min(L, 512)
    assert L % tq == 0 and L % tk == 0
    return tq, tk


def _latent_attention(qb, kb, vt, bsz, L):
    tq, tk = _flash_tiles(L)
    q4 = qb.reshape(B_HEADS, bsz, L, LANES)
    k4 = kb.reshape(B_HEADS, bsz, L, LANES)
    k_spec = pl.BlockSpec((2, None, L, LANES), lambda b, hp, i: (hp, b, 0, 0))
    vt_spec = pl.BlockSpec((2, VT_ROWS, L), lambda b, hp, i: (hp, 0, b))
    out = pl.pallas_call(
        functools.partial(_flash_kernel, tk=tk, nk=L // tk),
        grid=(bsz, B_HEADS // 2, L // tq),
        in_specs=[pl.BlockSpec((2, None, tq, LANES), lambda b, hp, i: (hp, b, i, 0)), k_spec, vt_spec],
        out_specs=pl.BlockSpec((None, tq, LANES), lambda b, hp, i: (b, i, hp)),
        out_shape=jax.ShapeDtypeStruct((bsz, L, B_WIDTH), BF16),
        compiler_params=_params("parallel", "parallel", "parallel"),
        name="latent_flash",
    )(q4, k4, vt)
    return out.reshape(bsz * L, B_WIDTH)


def _mix_out_kernel(*refs):
    it = iter(refs)
    x_ref, ya_ref, yb_ref = next(it), next(it), next(it)
    c_refs = [next(it) for _ in STRIDES]
    l_refs = [next(it) for _ in STRIDES]
    ga_ref, gb_ref, gc_ref, w_ref = next(it), next(it), next(it), next(it)
    kv_ref, gx_ref, wq_ref, wo_ref = next(it), next(it), next(it), next(it)
    o_ref = next(it)
    c_scr = next(it)
    tm = x_ref.shape[0]
    ngrp = C_WIDTH // LANES

    for n, r in enumerate(STRIDES):
        for j in range(r):
            blk = c_refs[n][j].astype(F32)
            for g in range(ngrp):
                c_scr[n * ngrp + g, pl.ds(j, tm // r, stride=r), :] = blk[:, g * LANES:(g + 1) * LANES]

    ls = [l_ref[...] for l_ref in l_refs]
    mx = functools.reduce(jnp.maximum, ls)
    es = [jnp.exp(l - mx) for l in ls]
    tot = functools.reduce(lambda a, b: a + b, es)
    head_of_lane = lax.broadcasted_iota(jnp.int32, (LSE_LANES, C_WIDTH), 1) // HEAD_DIM
    spread = (head_of_lane == lax.broadcasted_iota(jnp.int32, (LSE_LANES, C_WIDTH), 0)).astype(BF16)
    wide = []
    for e in es:
        w = e / tot
        hi = w.astype(BF16)
        lo = (w - hi.astype(F32)).astype(BF16)
        wide.append(_dot(hi, spread) + _dot(lo, spread))
    groups = []
    for kg in range(ngrp):
        acc = jnp.zeros((tm, LANES), F32)
        for n, w in enumerate(wide):
            acc = acc + c_scr[n * ngrp + kg] * w[:, kg * LANES:(kg + 1) * LANES]
        groups.append(acc)
    yc = jnp.concatenate(groups, axis=-1)
    y = jnp.concatenate([
        _rms(ya_ref[...].astype(F32), ga_ref[...]).astype(BF16),
        _rms(yb_ref[...].astype(F32), gb_ref[...]).astype(BF16),
        _rms(yc, gc_ref[...]).astype(BF16)], axis=-1)
    x = x_ref[...] + _dot(y, w_ref[...])
    o_ref[...] = _cross_attention(x, kv_ref, gx_ref, wq_ref, wo_ref)


def _mix_out(x2, ya, yb, ycs, lses, kvm, lw, L, tm):
    T = x2.shape[0]
    nl = L // tm
    row = lambda w: pl.BlockSpec((tm, w), lambda i: (i, 0))
    weights = (lw['g_out_a'], lw['g_out_b'], lw['g_out_c'], lw['w_out'])
    x_weights = (lw['g_x'], lw['w_xq'], lw['w_xo'])
    n = len(STRIDES)
    return pl.pallas_call(
        _mix_out_kernel,
        grid=(T // tm,),
        in_specs=([row(D_MODEL), row(A_WIDTH), row(B_WIDTH)]
                  + [_phase_spec(r, tm, nl, C_WIDTH) for r in STRIDES]
                  + [row(LSE_LANES)] * n
                  + [_const_spec(w.shape) for w in weights]
                  + [pl.BlockSpec((None,) + kvm.shape[1:], lambda i: (i // nl, 0, 0))]
                  + [_const_spec(w.shape) for w in x_weights]),
        out_specs=row(D_MODEL),
        out_shape=jax.ShapeDtypeStruct((T, D_MODEL), F32),
        scratch_shapes=[pltpu.VMEM((n * C_WIDTH // LANES, tm, LANES), F32)],
        compiler_params=_params("parallel"),
        name="mix_out_cross",
    )(x2, ya, yb, *ycs, *lses, *weights, kvm, *x_weights)


def _norm_matmul_kernel(x_ref, g_ref, w_ref, o_ref):
    o_ref[...] = _dot(_rms(x_ref[...], g_ref[...]).astype(BF16), w_ref[...]).astype(o_ref.dtype)


def _norm_matmul(x2, g, w, tm):
    T, K = x2.shape
    N = w.shape[1]
    return pl.pallas_call(
        _norm_matmul_kernel,
        grid=(T // tm,),
        in_specs=[pl.BlockSpec((tm, K), lambda i: (i, 0)), _const_spec(g.shape), _const_spec(w.shape)],
        out_specs=pl.BlockSpec((tm, N), lambda i: (i, 0)),
        out_shape=jax.ShapeDtypeStruct((T, N), BF16),
        compiler_params=_params("parallel"),
        name="norm_matmul",
    )(x2, g, w)


def _cross_attention(x, kv_ref, g_ref, wq_ref, wo_ref):
    h = _rms(x, g_ref[...]).astype(BF16)
    q = (_dot(h, wq_ref[...]) * (X_HEAD_DIM ** -0.5)).astype(BF16)
    xd = X_HEADS * X_HEAD_DIM
    outs = []
    for hd in range(X_HEADS):
        sl = slice(hd * X_HEAD_DIM, (hd + 1) * X_HEAD_DIM)
        k = kv_ref[:, sl]
        v = kv_ref[:, xd + hd * X_HEAD_DIM:xd + (hd + 1) * X_HEAD_DIM]
        s = _dot_nt(q[:, sl], k)
        p = jnp.exp(s - jnp.max(s, axis=-1, keepdims=True))
        den = jnp.sum(p, axis=-1, keepdims=True)
        outs.append((_dot(p.astype(BF16), v) / den).astype(BF16))
    return x + _dot(jnp.concatenate(outs, axis=-1), wo_ref[...])


def _silu_mul(gate, up):
    return gate * jax.nn.sigmoid(gate) * up


def _swiglu_kernel(x_ref, g_ref, wg_ref, wu_ref, wd_ref, o_ref, h_scr, acc_scr):
    j = pl.program_id(1)

    @pl.when(j == 0)
    def _():
        h_scr[...] = _rms(x_ref[...], g_ref[...]).astype(BF16)
        acc_scr[...] = jnp.zeros_like(acc_scr)

    h = h_scr[...]
    act = _silu_mul(_dot(h, wg_ref[...]), _dot(h, wu_ref[...])).astype(BF16)
    acc_scr[...] += _dot(act, wd_ref[...])

    @pl.when(j == pl.num_programs(1) - 1)
    def _():
        o_ref[...] = x_ref[...] + acc_scr[...]


def _ff_chunk(dff, limit):
    for c in (1792, 1408, 1024, 512, 256, 128):
        if c <= limit and dff % c == 0:
            return c
    raise ValueError(dff)


def _swiglu(x2, g, wg, wu, wd, tm):
    T = x2.shape[0]
    dff = wg.shape[1]
    ck = _ff_chunk(dff, 1408)
    return pl.pallas_call(
        _swiglu_kernel,
        grid=(T // tm, dff // ck),
        in_specs=[pl.BlockSpec((tm, D_MODEL), lambda i, j: (i, 0)),
                  _const_spec(g.shape),
                  pl.BlockSpec((D_MODEL, ck), lambda i, j: (0, j)),
                  pl.BlockSpec((D_MODEL, ck), lambda i, j: (0, j)),
                  pl.BlockSpec((ck, D_MODEL), lambda i, j: (j, 0))],
        out_specs=pl.BlockSpec((tm, D_MODEL), lambda i, j: (i, 0)),
        out_shape=jax.ShapeDtypeStruct((T, D_MODEL), F32),
        scratch_shapes=[pltpu.VMEM((tm, D_MODEL), BF16), pltpu.VMEM((tm, D_MODEL), F32)],
        compiler_params=_params("parallel", "arbitrary"),
        name="swiglu_dense",
    )(x2, g, wg, wu, wd)


def _router_logits(h, wr):
    return jnp.dot(h, wr, preferred_element_type=F32, precision=lax.Precision.HIGHEST)


def _router_kernel(x_ref, g_ref, wr_ref, tri_ref, c0_ref, route_ref, cnt_ref, run_scr):
    @pl.when(pl.program_id(0) == 0)
    def _():
        run_scr[...] = c0_ref[...]

    logits = _router_logits(_rms(x_ref[...], g_ref[...]), wr_ref[...])
    tm = logits.shape[0]
    lane = lax.broadcasted_iota(jnp.int32, (tm, LANES), 1)
    logits = jnp.where(lane < N_EXPERTS, logits, NEG)
    v1 = jnp.max(logits, axis=-1, keepdims=True)
    i1 = jnp.min(jnp.where(logits == v1, lane, LANES), axis=-1, keepdims=True)
    rest = jnp.where(lane == i1, NEG, logits)
    v2 = jnp.max(rest, axis=-1, keepdims=True)
    i2 = jnp.min(jnp.where(rest == v2, lane, LANES), axis=-1, keepdims=True)
    pid = jnp.minimum(i1, i2) * N_EXPERTS + jnp.maximum(i1, i2)
    onehot = (lane == pid).astype(F32)
    before = _dot(tri_ref[...], onehot.astype(BF16)) + run_scr[...]
    rank = jnp.sum(onehot * before, axis=-1, keepdims=True)
    run_scr[...] += jnp.sum(onehot, axis=0, keepdims=True)
    cnt_ref[...] = run_scr[...]
    lane8 = lax.broadcasted_iota(jnp.int32, (tm, LSE_LANES), 1)
    route_ref[...] = jnp.where(lane8 == 0, pid, jnp.where(lane8 == 1, rank.astype(jnp.int32), 0))


def _router(x2, g, wr, counts0, tm):
    T = x2.shape[0]
    tri = (np.arange(tm)[:, None] > np.arange(tm)[None, :]).astype(np.float32)
    tri = jnp.asarray(tri, BF16)
    return pl.pallas_call(
        _router_kernel,
        grid=(T // tm,),
        in_specs=[pl.BlockSpec((tm, D_MODEL), lambda i: (i, 0)), _const_spec(g.shape), _const_spec(wr.shape),
                  _const_spec(tri.shape), _const_spec(counts0.shape)],
        out_specs=[pl.BlockSpec((tm, LSE_LANES), lambda i: (i, 0)), _const_spec(counts0.shape)],
        out_shape=[jax.ShapeDtypeStruct((T, LSE_LANES), jnp.int32), jax.ShapeDtypeStruct(counts0.shape, F32)],
        scratch_shapes=[pltpu.VMEM(counts0.shape, F32)],
        compiler_params=_params("arbitrary"),
        name="router",
    )(x2, g, wr, tri, counts0)


ROW_SUBLANES = D_MODEL // LANES


def _slab_to_rows(ref, n):
    return jnp.concatenate([ref[pl.ds(c, n, stride=ROW_SUBLANES), :] for c in range(ROW_SUBLANES)], axis=-1)


def _rows_to_slab(ref, rows, n):
    for c in range(ROW_SUBLANES):
        ref[pl.ds(c, n, stride=ROW_SUBLANES), :] = rows[:, c * LANES:(c + 1) * LANES]


def _slab(ref, row):
    return ref.at[pl.ds(pl.multiple_of(row * ROW_SUBLANES, ROW_SUBLANES), ROW_SUBLANES)]


MOVE_UNROLL = 8


def _move_rows(n, copy_of_row, copy_of_all):
    def issue(r, _):
        copy_of_row(r).start()
        return 0

    lax.fori_loop(0, n, issue, 0, unroll=MOVE_UNROLL)
    copy_of_all().wait()


def _dispatch_kernel(slot_ref, x_ref, prev_ref, o_ref, slab, sem, *, tm):
    del prev_ref
    _rows_to_slab(slab, x_ref[...], tm)

    def copy_of_row(r):
        slot = slot_ref[0, r]
        return pltpu.make_async_copy(_slab(slab, r), _slab(o_ref, slot), sem)

    def copy_of_all():
        return pltpu.make_async_copy(slab, o_ref.at[pl.ds(0, tm * ROW_SUBLANES)], sem)

    _move_rows(tm, copy_of_row, copy_of_all)


def _slots_kernel(route_ref, start_ref, o_ref):
    route = route_ref[...]
    tm = route.shape[0]
    lane = lax.broadcasted_iota(jnp.int32, (tm, LANES), 1)
    first = jnp.sum(jnp.where(lane == route[:, 0:1], start_ref[...], 0), axis=-1, keepdims=True)
    o_ref[...] = jnp.broadcast_to(first + route[:, 1:2], o_ref.shape)


def _slots(route, start, tm):
    T = route.shape[0]
    slots = pl.pallas_call(
        _slots_kernel,
        grid=(T // tm,),
        in_specs=[pl.BlockSpec((tm, LSE_LANES), lambda i: (i, 0)), _const_spec(start.shape)],
        out_specs=pl.BlockSpec((tm, LSE_LANES), lambda i: (i, 0)),
        out_shape=jax.ShapeDtypeStruct((T, LSE_LANES), jnp.int32),
        compiler_params=_params("parallel"),
        name="slots",
    )(route, start)
    return slots[:, 0].reshape(T // tm, 1, tm)


def _dispatch(x2, slots, sorted_rows, tm):
    T = x2.shape[0]
    return pl.pallas_call(
        functools.partial(_dispatch_kernel, tm=tm),
        grid=(T // tm,),
        in_specs=[pl.BlockSpec((None, 1, tm), lambda i: (i, 0, 0), memory_space=pltpu.SMEM),
                  pl.BlockSpec((tm, D_MODEL), lambda i: (i, 0)),
                  pl.BlockSpec(memory_space=pl.ANY)],
        out_specs=pl.BlockSpec(memory_space=pl.ANY),
        out_shape=jax.ShapeDtypeStruct(sorted_rows.shape, sorted_rows.dtype),
        scratch_shapes=[pltpu.VMEM((tm * ROW_SUBLANES, LANES), F32), pltpu.SemaphoreType.DMA],
        input_output_aliases={2: 0},
        compiler_params=pltpu.CompilerParams(dimension_semantics=("arbitrary",), vmem_limit_bytes=VMEM_LIMIT,
                                             has_side_effects=True),
        name="dispatch",
    )(slots, x2, sorted_rows)


def _unpermute_kernel(slot_ref, y_ref, o_ref, slab, sem, *, tm):
    def copy_of_row(r):
        slot = slot_ref[0, r]
        return pltpu.make_async_copy(_slab(y_ref, slot), _slab(slab, r), sem)

    def copy_of_all():
        return pltpu.make_async_copy(y_ref.at[pl.ds(0, tm * ROW_SUBLANES)], slab, sem)

    _move_rows(tm, copy_of_row, copy_of_all)
    o_ref[...] = _slab_to_rows(slab, tm)


def _unpermute(y_sorted, slots, T, tm):
    return pl.pallas_call(
        functools.partial(_unpermute_kernel, tm=tm),
        grid=(T // tm,),
        in_specs=[pl.BlockSpec((None, 1, tm), lambda i: (i, 0, 0), memory_space=pltpu.SMEM),
                  pl.BlockSpec(memory_space=pl.ANY)],
        out_specs=pl.BlockSpec((tm, D_MODEL), lambda i: (i, 0)),
        out_shape=jax.ShapeDtypeStruct((T, D_MODEL), F32),
        scratch_shapes=[pltpu.VMEM((tm * ROW_SUBLANES, LANES), F32), pltpu.SemaphoreType.DMA],
        compiler_params=_params("arbitrary"),
        name="unpermute",
    )(slots, y_sorted)


def _pair_experts_kernel(ta_ref, tb_ref, nv_ref, xs_ref, g_ref, wr_ref, wg_ref, wu_ref, wd_ref, gf_ref, o_ref,
                         x_scr, h_scr, gate_scr, acc_scr, *, nj, final_norm):
    i = pl.program_id(0)
    j = pl.program_id(1)
    tm = x_scr.shape[0]

    @pl.when(i < nv_ref[0])
    def _():
        @pl.when(j == 0)
        def _():
            x = _slab_to_rows(xs_ref, tm)
            x_scr[...] = x
            h = _rms(x, g_ref[...])
            h_scr[...] = h.astype(BF16)
            la = jnp.sum(h * wr_ref[pl.ds(ta_ref[i], 1), :], axis=-1, keepdims=True)
            lb = jnp.sum(h * wr_ref[pl.ds(tb_ref[i], 1), :], axis=-1, keepdims=True)
            mx = jnp.maximum(la, lb)
            ea, eb = jnp.exp(la - mx), jnp.exp(lb - mx)
            gate_scr[0] = ea / (ea + eb)
            gate_scr[1] = eb / (ea + eb)
            acc_scr[...] = jnp.zeros_like(acc_scr)

        h = h_scr[...]
        gate = jnp.where(j < nj, gate_scr[0], gate_scr[1])
        act = (_silu_mul(_dot(h, wg_ref[...]), _dot(h, wu_ref[...])) * gate).astype(BF16)
        acc_scr[...] += _dot(act, wd_ref[...])

        @pl.when(j == 2 * nj - 1)
        def _():
            out = x_scr[...] + acc_scr[...]
            if final_norm:
                out = _rms(out, gf_ref[...])
            _rows_to_slab(o_ref, out, tm)

    @pl.when(jnp.logical_and(i >= nv_ref[0], j == 2 * nj - 1))
    def _():
        o_ref[...] = jnp.zeros_like(o_ref)


def _pair_experts(xs, g, wr, wg, wu, wd, g_final, tile_a, tile_b, n_valid, tm, final_norm):
    rows = xs.shape[0] // ROW_SUBLANES
    dff = wg.shape[2]
    ck = _ff_chunk(dff, 1792)
    nj = dff // ck

    def expert(i, j, ta, tb):
        return jnp.where(j < nj, ta[i], tb[i])

    def chunk(i, j, nv):
        return jnp.where(i < nv[0], j % nj, nj - 1)

    slab_spec = pl.BlockSpec((tm * ROW_SUBLANES, LANES), lambda i, j, ta, tb, nv: (i, 0))
    const = lambda a: pl.BlockSpec(a.shape, lambda i, j, ta, tb, nv: (0,) * a.ndim)
    grid_spec = pltpu.PrefetchScalarGridSpec(
        num_scalar_prefetch=3,
        grid=(rows // tm, 2 * nj),
        in_specs=[slab_spec, const(g), const(wr),
                  pl.BlockSpec((None, D_MODEL, ck), lambda i, j, ta, tb, nv: (expert(i, j, ta, tb), 0, chunk(i, j, nv))),
                  pl.BlockSpec((None, D_MODEL, ck), lambda i, j, ta, tb, nv: (expert(i, j, ta, tb), 0, chunk(i, j, nv))),
                  pl.BlockSpec((None, ck, D_MODEL), lambda i, j, ta, tb, nv: (expert(i, j, ta, tb), chunk(i, j, nv), 0)),
                  const(g_final)],
        out_specs=slab_spec,
        scratch_shapes=[pltpu.VMEM((tm, D_MODEL), F32), pltpu.VMEM((tm, D_MODEL), BF16),
                        pltpu.VMEM((2, tm, 1), F32), pltpu.VMEM((tm, D_MODEL), F32)],
    )
    return pl.pallas_call(
        functools.partial(_pair_experts_kernel, nj=nj, final_norm=final_norm),
        grid_spec=grid_spec,
        out_shape=jax.ShapeDtypeStruct(xs.shape, F32),
        compiler_params=_params("parallel", "arbitrary"),
        name="pair_experts",
    )(tile_a, tile_b, n_valid, xs, g, wr, wg, wu, wd, g_final)


def _plan_pairs(counts, n_tiles, tm):
    counts = counts.reshape(-1).astype(jnp.int32)
    tiles = (counts + tm - 1) // tm
    tile_end = jnp.cumsum(tiles)
    start = (tile_end - tiles) * tm
    n_valid = tile_end[-1:]
    tile_ids = jnp.minimum(jnp.arange(n_tiles), n_valid[0] - 1)
    tile_pid = jnp.sum(tile_ids[:, None] >= tile_end[None, :], axis=-1)
    tile_a = (tile_pid // N_EXPERTS).astype(jnp.int32)
    tile_b = (tile_pid % N_EXPERTS).astype(jnp.int32)
    return start.astype(jnp.int32), tile_a, tile_b, n_valid.astype(jnp.int32)


def _prep_layer(l, p):
    offs = np.cumsum((0,) + IN_SIZES)
    w_in = p['w_in'][l]
    qa, ka, va, cq, ckv, kr, qc, kc, vc = [w_in[:, int(offs[i]):int(offs[i + 1])] for i in range(9)]
    scale = HEAD_DIM ** -0.5
    order = list(A_HEAD_ORDER)
    qa = qa.reshape(D_MODEL, A_Q_HEADS, HEAD_DIM)[:, order, :].reshape(D_MODEL, A_WIDTH) * scale
    half = B_ROPE // 2
    z = lambda *s: jnp.zeros(s, F32)
    kr_pad = jnp.concatenate([z(D_MODEL, B_NOPE), kr, z(D_MODEL, LANES - B_NOPE - B_ROPE)], axis=1)
    kr_rot = jnp.concatenate([z(D_MODEL, B_NOPE), -kr[:, half:], kr[:, :half], z(D_MODEL, LANES - B_NOPE - B_ROPE)], axis=1)
    uq = p['mla_w_uq'][l].reshape(B_Q_LORA, B_HEADS, B_NOPE + B_ROPE)
    padq = z(B_Q_LORA, B_HEADS, LANES - B_NOPE - B_ROPE)
    uq_pad = jnp.concatenate([uq, padq], axis=-1).reshape(B_Q_LORA, B_HEADS * LANES)
    uq_rot = jnp.concatenate([z(B_Q_LORA, B_HEADS, B_NOPE), -uq[..., B_NOPE + half:], uq[..., B_NOPE:B_NOPE + half], padq],
                             axis=-1).reshape(B_Q_LORA, B_HEADS * LANES)
    ukv = p['mla_w_ukv'][l].reshape(B_KV_LORA, B_HEADS, B_NOPE + B_V)
    zk = z(B_KV_LORA, B_HEADS, HALF)
    uk_pad = jnp.concatenate([ukv[..., :B_NOPE], zk], axis=-1).reshape(B_KV_LORA, B_HEADS * LANES)
    uv_pad = jnp.concatenate([ukv[..., B_NOPE:], zk], axis=-1).reshape(B_KV_LORA, B_HEADS * LANES)

    g_out = p['mix_out_norm_g'][l]
    w_out = p['w_out'][l]
    a_rows = np.concatenate([np.arange(HEAD_DIM) + HEAD_DIM * h for h in A_HEAD_ORDER])
    w_out = jnp.concatenate([w_out[:A_WIDTH][a_rows], w_out[A_WIDTH:]], axis=0)
    g_out_a = g_out[:A_WIDTH][a_rows]
    row = lambda v: v.reshape(1, -1).astype(F32)
    return {
        'g_mix': row(p['norm_mix_g'][l]),
        'wa': jnp.concatenate([qa, ka, va], axis=1).astype(BF16),
        'wc': jnp.concatenate([qc * scale, kc, vc], axis=1).astype(BF16),
        'wb': jnp.concatenate([cq, ckv], axis=1).astype(BF16),
        'wkr': jnp.concatenate([kr_pad, kr_rot], axis=1).astype(BF16),
        'g_q': row(p['mla_q_norm_g'][l]),
        'g_kv': row(p['mla_kv_norm_g'][l]),
        'wuq': jnp.concatenate([uq_pad, uq_rot], axis=1).astype(BF16),
        'wukv': jnp.concatenate([uk_pad, uv_pad], axis=1).astype(BF16),
        'sink': p['sink_a'][l][np.array(A_HEAD_ORDER)].astype(F32),
        'g_out_a': row(g_out_a),
        'g_out_b': row(g_out[A_WIDTH:A_WIDTH + B_WIDTH]),
        'g_out_c': row(g_out[A_WIDTH + B_WIDTH:]),
        'w_out': w_out.astype(BF16),
        'g_x': row(p['norm_x_g'][l]),
        'g_mem': row(p['norm_mem_g'][l]),
        'w_xq': p['w_xq'][l].astype(BF16),
        'w_xkv': p['w_xkv'][l].astype(BF16),
        'w_xo': p['w_xo'][l].astype(BF16),
        'g_ffn': row(p['norm_ffn_g'][l]),
    }


def _prep_ffn(w_gu, w_down):
    dff = w_down.shape[-2]
    return w_gu[..., :dff].astype(BF16), w_gu[..., dff:].astype(BF16), w_down.astype(BF16)


def _rope_tables(L):
    half = B_ROPE // 2
    pos = jnp.arange(L, dtype=F32)
    inv = ROPE_THETA ** (-jnp.arange(half, dtype=F32) / half)
    ang = pos[:, None] * inv[None, :]
    cos2 = jnp.tile(jnp.cos(ang), (1, 2))
    sin2 = jnp.tile(jnp.sin(ang), (1, 2))
    pad = jnp.zeros((L, LANES - B_NOPE - B_ROPE), F32)
    qscale = (B_NOPE + B_ROPE) ** -0.5 * LOG2E
    cq = jnp.concatenate([jnp.ones((L, B_NOPE), F32), cos2, pad], axis=1) * qscale
    sq = jnp.concatenate([jnp.zeros((L, B_NOPE), F32), sin2, pad], axis=1) * qscale
    ck = jnp.concatenate([jnp.zeros((L, B_NOPE), F32), cos2, pad], axis=1)
    sk = jnp.concatenate([jnp.zeros((L, B_NOPE), F32), sin2, pad], axis=1)
    return cq, sq, ck, sk


TOKEN_TILE = 512
EXPERT_TILE = 512
FFN_TILE = 1024
MOVE_TILE = 1024
N_PAIRS_USED = N_EXPERTS * (N_EXPERTS - 1) // 2


def _mixer_and_cross(x2, mem, lw, bsz, L):
    T = bsz * L
    tm = min(TOKEN_TILE, L)
    tabs = _rope_tables(L)
    slopes_a = _alibi_slopes(A_Q_HEADS)[list(A_HEAD_ORDER)]
    slopes_c = _alibi_slopes(C_HEADS)
    (qa, ka, va), qkv_c, (qb, kb, vb) = _in_proj(x2, lw, tabs, bsz, L, tm)
    as_seq = lambda t: t.reshape(bsz, 1, L, t.shape[-1])
    ya = _banded_attention(as_seq(qa), as_seq(ka), as_seq(va), slopes_a, W=A_WINDOW, n_kg=1, G=2, sink=lw['sink'])
    ya = ya.reshape(T, A_WIDTH)
    yb = _latent_attention(qb, kb, vb, bsz, L)
    ycs, lses = [], []
    for (w, r), (qc, kc, vc) in zip(DILATED_PAIRS, qkv_c):
        o, lse = _banded_attention(qc, kc, vc, slopes_c, W=w // (2 * r), n_kg=C_HEADS // 2, G=1, want_lse=True)
        ycs.append(o)
        lses.append(jnp.transpose(lse, (0, 2, 1, 3)).reshape(T, LSE_LANES))
    mem2 = mem.reshape(-1, D_MODEL)
    kvm = _norm_matmul(mem2, lw['g_mem'], lw['w_xkv'], min(512, mem2.shape[0]))
    return _mix_out(x2, ya, yb, ycs, lses, kvm.reshape(bsz, mem.shape[1], -1), lw, L, tm)


def _routed_swiglu(xs2, g, wr, wg, wu, wd, final_g, final_norm):
    total = sum(x.shape[0] for x in xs2)
    et = min(EXPERT_TILE, total)
    n_tiles = total // et + N_PAIRS_USED
    counts = jnp.zeros((1, LANES), F32)
    routes = []
    for x2 in xs2:
        route, counts = _router(x2, g, wr, counts, min(TOKEN_TILE, x2.shape[0]))
        routes.append(route)
    start, tile_a, tile_b, n_valid = _plan_pairs(counts, n_tiles, et)
    move = [min(MOVE_TILE, x2.shape[0]) for x2 in xs2]
    slots = [_slots(route, start.reshape(1, LANES), mt) for route, mt in zip(routes, move)]
    sorted_rows = jnp.zeros((n_tiles * et * ROW_SUBLANES, LANES), F32)
    for x2, sl, mt in zip(xs2, slots, move):
        sorted_rows = _dispatch(x2, sl, sorted_rows, mt)
    wr_rows = wr[:, :N_EXPERTS].T
    y_sorted = _pair_experts(sorted_rows, g, wr_rows, wg, wu, wd, final_g, tile_a, tile_b, n_valid, et, final_norm)
    return [_unpermute(y_sorted, sl, x2.shape[0], mt) for x2, sl, mt in zip(xs2, slots, move)]


def _encoder(groups, layers, ffn, moe, routers, final_g):
    shapes = [x.shape for x, _ in groups]
    xs2 = [x.reshape(-1, D_MODEL) for x, _ in groups]
    depth = len(layers)
    normed = False
    for l, lw in enumerate(layers):
        xs2 = [_mixer_and_cross(x2, mem, lw, shp[0], shp[1]) for x2, (_, mem), shp in zip(xs2, groups, shapes)]
        if l % 2 == 0:
            wg, wu, wd = ffn[l // 2]
            xs2 = [_swiglu(x2, lw['g_ffn'], wg, wu, wd, min(FFN_TILE, x2.shape[0])) for x2 in xs2]
        else:
            wg, wu, wd = moe[l // 2]
            normed = l == depth - 1
            xs2 = _routed_swiglu(xs2, lw['g_ffn'], routers[l // 2], wg, wu, wd, final_g, final_norm=normed)
    assert normed, "the final rmsnorm is fused into the last layer's routed SwiGLU"
    return tuple(x2.reshape(shp) for x2, shp in zip(xs2, shapes))


def kernel(x_prompt, x_sample, mem_prompt, mem_sample, norm_mix_g, w_in, sink_a, mla_q_norm_g, mla_kv_norm_g, mla_w_uq, mla_w_ukv, mix_out_norm_g, w_out, norm_x_g, norm_mem_g, w_xq, w_xkv, w_xo, norm_ffn_g, ffn_w_gu, ffn_w_down, moe_router, moe_w_gu, moe_w_down, final_norm_g):
    p = dict(norm_mix_g=norm_mix_g, w_in=w_in, sink_a=sink_a, mla_q_norm_g=mla_q_norm_g, mla_kv_norm_g=mla_kv_norm_g,
             mla_w_uq=mla_w_uq, mla_w_ukv=mla_w_ukv, mix_out_norm_g=mix_out_norm_g, w_out=w_out, norm_x_g=norm_x_g,
             norm_mem_g=norm_mem_g, w_xq=w_xq, w_xkv=w_xkv, w_xo=w_xo, norm_ffn_g=norm_ffn_g)
    depth = w_in.shape[0]
    layers = [_prep_layer(l, p) for l in range(depth)]
    ffn = [_prep_ffn(ffn_w_gu[i], ffn_w_down[i]) for i in range(ffn_w_gu.shape[0])]
    moe = [_prep_ffn(moe_w_gu[i], moe_w_down[i]) for i in range(moe_w_gu.shape[0])]
    routers = [jnp.pad(moe_router[i].astype(F32), ((0, 0), (0, LANES - N_EXPERTS))) for i in range(moe_router.shape[0])]
    final_g = final_norm_g.reshape(1, -1).astype(F32)
    return _encoder([(x_prompt, mem_prompt), (x_sample, mem_sample)], layers, ffn, moe, routers, final_g)
```

```python
import functools
import math

import numpy as np
import jax
import jax.numpy as jnp
from jax import lax
from jax.experimental import pallas as pl
from jax.experimental.pallas import tpu as pltpu

D_MODEL = 1024
HEAD_DIM = 64
A_Q_HEADS = 4
A_KV_HEADS = 2
A_WINDOW = 128
B_HEADS = 6
B_NOPE = 64
B_ROPE = 32
B_V = 64
B_Q_LORA = 384
B_KV_LORA = 256
ROPE_THETA = 10000.0
C_HEADS = 6
DILATED_PAIRS = ((128, 1), (512, 4), (2048, 16))
STRIDES = tuple(r for _, r in DILATED_PAIRS)
A_WIDTH = A_Q_HEADS * HEAD_DIM
B_WIDTH = B_HEADS * B_V
C_WIDTH = C_HEADS * HEAD_DIM
IN_SIZES = (A_WIDTH, A_KV_HEADS * HEAD_DIM, A_KV_HEADS * HEAD_DIM, B_Q_LORA, B_KV_LORA, B_ROPE,
            C_WIDTH, C_WIDTH, C_WIDTH)
X_HEADS = 4
X_HEAD_DIM = 128
N_EXPERTS = 8
EPS = 1e-6
NEG = -1e30

LANES = 128
HALF = LANES // 2
LSE_LANES = 8
VMEM_LIMIT = 56 * 1024 * 1024
LOG2E = math.log2(math.e)

BF16 = jnp.bfloat16
F32 = jnp.float32

A_HEAD_ORDER = (0, 2, 1, 3)


def _params(*sem):
    return pltpu.CompilerParams(dimension_semantics=sem, vmem_limit_bytes=VMEM_LIMIT)


def _rms(x, g):
    return x * lax.rsqrt(jnp.mean(x * x, axis=-1, keepdims=True) + EPS) * g


def _dot(a, b):
    return jnp.dot(a, b, preferred_element_type=F32)


def _dot_nt(a, b):
    return lax.dot_general(a, b, (((1,), (1,)), ((), ())), preferred_element_type=F32)


def _const_spec(shape):
    n = len(shape)
    return pl.BlockSpec(shape, lambda *_: (0,) * n)


def _phase_spec(r, tm, nl, width):
    return pl.BlockSpec((None, r, tm // r, width), lambda i: (i // nl, 0, i % nl, 0))


def _in_kernel(*refs):
    it = iter(refs)
    x_ref, g_ref, wa_ref, wc_ref, wb_ref, wkr_ref, gq_ref, gkv_ref, wuq_ref, wukv_ref = [next(it) for _ in range(10)]
    cq_ref, sq_ref, ck_ref, sk_ref = [next(it) for _ in range(4)]
    qa_ref, ka_ref, va_ref = next(it), next(it), next(it)
    c_refs = [(next(it), next(it), next(it)) for _ in STRIDES]
    qb_ref, kb_ref, vb_ref = next(it), next(it), next(it)
    zc_scr = next(it)
    tm = x_ref.shape[0]

    h = _rms(x_ref[...], g_ref[...]).astype(BF16)
    za = _dot(h, wa_ref[...])
    qa_ref[...] = za[:, :A_WIDTH].astype(BF16)
    ka_ref[...] = za[:, A_WIDTH:A_WIDTH + LANES].astype(BF16)
    va_ref[...] = za[:, A_WIDTH + LANES:].astype(BF16)

    zc = _dot(h, wc_ref[...])
    ngrp = C_WIDTH // LANES
    for g in range(3 * ngrp):
        zc_scr[g] = zc[:, g * LANES:(g + 1) * LANES]
    for r, qkv_refs in zip(STRIDES, c_refs):
        for j in range(r):
            for n, ref in enumerate(qkv_refs):
                ref[j] = jnp.concatenate(
                    [zc_scr[n * ngrp + g, pl.ds(j, tm // r, stride=r), :] for g in range(ngrp)], axis=-1).astype(BF16)

    zb = _dot(h, wb_ref[...])
    hq = _rms(zb[:, :B_Q_LORA], gq_ref[...]).astype(BF16)
    hkv = _rms(zb[:, B_Q_LORA:], gkv_ref[...]).astype(BF16)
    zq = _dot(hq, wuq_ref[...])
    zkv = _dot(hkv, wukv_ref[...])
    zkr = _dot(h, wkr_ref[...])
    kr = zkr[:, :LANES] * ck_ref[...] + zkr[:, LANES:] * sk_ref[...]
    cq = cq_ref[...]
    sq = sq_ref[...]
    lane = lax.broadcasted_iota(jnp.int32, (1, LANES), 1)
    nb = B_HEADS * LANES
    for hd in range(B_HEADS):
        lo, hi = hd * LANES, (hd + 1) * LANES
        qb_ref[hd] = (zq[:, lo:hi] * cq + zq[:, nb + lo:nb + hi] * sq).astype(BF16)
        kb_ref[hd] = (zkv[:, lo:hi] + kr).astype(BF16)
        ones = (lane == B_V).astype(F32)
        vb_ref[hd] = (zkv[:, nb + lo:nb + hi] + ones).T.astype(BF16)


def _in_proj(x2, lw, tabs, bsz, L, tm):
    T = x2.shape[0]
    nl = L // tm
    row = lambda w: pl.BlockSpec((tm, w), lambda i: (i, 0))
    tab = pl.BlockSpec((tm, LANES), lambda i: (i % nl, 0))
    hm = pl.BlockSpec((B_HEADS, tm, LANES), lambda i: (0, i, 0))
    weights = (lw['g_mix'], lw['wa'], lw['wc'], lw['wb'], lw['wkr'], lw['g_q'], lw['g_kv'], lw['wuq'], lw['wukv'])
    out_specs = [row(A_WIDTH), row(LANES), row(LANES)]
    out_shape = [jax.ShapeDtypeStruct((T, w), BF16) for w in (A_WIDTH, LANES, LANES)]
    for r in STRIDES:
        assert tm % (16 * r) == 0, (tm, r)
        out_specs += [_phase_spec(r, tm, nl, C_WIDTH)] * 3
        out_shape += [jax.ShapeDtypeStruct((bsz, r, L // r, C_WIDTH), BF16)] * 3
    out_specs += [hm, hm, pl.BlockSpec((B_HEADS, LANES, tm), lambda i: (0, 0, i))]
    out_shape += [jax.ShapeDtypeStruct((B_HEADS, T, LANES), BF16)] * 2
    out_shape += [jax.ShapeDtypeStruct((B_HEADS, LANES, T), BF16)]
    outs = pl.pallas_call(
        _in_kernel,
        grid=(T // tm,),
        in_specs=[row(D_MODEL)] + [_const_spec(w.shape) for w in weights] + [tab] * 4,
        out_specs=out_specs,
        out_shape=out_shape,
        scratch_shapes=[pltpu.VMEM((3 * C_WIDTH // LANES, tm, LANES), F32)],
        compiler_params=_params("parallel"),
        name="in_proj",
    )(x2, *weights, *tabs)
    qkv_c = [outs[3 + 3 * n:6 + 3 * n] for n in range(len(STRIDES))]
    return outs[:3], qkv_c, outs[-3:]


def _banded_kernel(*refs, W, Q, R, n_kg, G, has_sink, want_lse, nchunks):
    it = iter(refs)
    q_ref = next(it)
    kp_ref, km_ref, kn_ref = next(it), next(it), next(it)
    vp_ref, vm_ref, vn_ref = next(it), next(it), next(it)
    bias_ref = next(it)
    sink_ref = next(it) if has_sink else None
    o_ref = next(it)
    lse_ref = next(it) if want_lse else None
    kfull, vfull = next(it), next(it)

    c = pl.program_id(2)
    P = q_ref.shape[0]
    kfull[:, 0:W] = kp_ref[...]
    kfull[:, W:W + R] = km_ref[...]
    kfull[:, W + R:] = kn_ref[...]
    vfull[:, 0:W] = vp_ref[...]
    vfull[:, W:W + R] = vm_ref[...]
    vfull[:, W + R:] = vn_ref[...]

    nsub = R // Q
    win = Q + 2 * W
    lo = lax.broadcasted_iota(jnp.int32, (Q, LANES), 1) < HALF
    col = lax.broadcasted_iota(jnp.int32, (2 * Q, win), 1)
    top = lax.broadcasted_iota(jnp.int32, (2 * Q, 1), 0) < Q
    lane8 = lax.broadcasted_iota(jnp.int32, (Q, LSE_LANES), 1)
    before_start = jnp.logical_and(c == 0, col < W)
    after_end = jnp.logical_and(c == nchunks - 1, col >= Q + W)
    units = [(ph, i, kg, g) for ph in range(P) for i in range(nsub) for kg in range(n_kg) for g in range(G)]

    scores = []
    for ph, i, kg, g in units:
        qg = kg * G + g
        qblk = q_ref[ph, i * Q:(i + 1) * Q, qg * LANES:(qg + 1) * LANES]
        zero = jnp.zeros_like(qblk)
        qm = jnp.concatenate([jnp.where(lo, qblk, zero), jnp.where(lo, zero, qblk)], axis=0)
        kwin = kfull[ph, i * Q:i * Q + win, kg * LANES:(kg + 1) * LANES]
        s = _dot_nt(qm, kwin) + bias_ref[qg]
        if i == 0:
            s = jnp.where(before_start, NEG, s)
        if i == nsub - 1:
            s = jnp.where(after_end, NEG, s)
        scores.append(s)

    probs = []
    for (ph, i, kg, g), s in zip(units, scores):
        qg = kg * G + g
        m = jnp.max(s, axis=-1, keepdims=True)
        if has_sink:
            sk = jnp.where(top, sink_ref[2 * qg], sink_ref[2 * qg + 1])
            m = jnp.maximum(m, sk)
        p = jnp.exp(s - m)
        den = jnp.sum(p, axis=-1, keepdims=True)
        if has_sink:
            den = den + jnp.exp(sk - m)
        probs.append((p.astype(BF16), den, m))

    lse_tiles = {(ph, i): jnp.zeros((Q, LSE_LANES), F32) for ph in range(P) for i in range(nsub)}
    for (ph, i, kg, g), (p, den, m) in zip(units, probs):
        qg = kg * G + g
        vwin = vfull[ph, i * Q:i * Q + win, kg * LANES:(kg + 1) * LANES]
        o = _dot(p, vwin) / den
        o_ref[ph, i * Q:(i + 1) * Q, qg * LANES:(qg + 1) * LANES] = jnp.where(lo, o[:Q], o[Q:]).astype(BF16)
        if want_lse:
            lse = m + jnp.log(den)
            tile = jnp.where(lane8 == 2 * qg, lse[:Q], lse_tiles[ph, i])
            lse_tiles[ph, i] = jnp.where(lane8 == 2 * qg + 1, lse[Q:], tile)
    if want_lse:
        for (ph, i), tile in lse_tiles.items():
            lse_ref[ph, i * Q:(i + 1) * Q, :] = tile


def _band_bias(slopes, step, W, Q):
    row = np.arange(Q)[:, None]
    col = np.arange(Q + 2 * W)[None, :]
    dist = np.abs(row + W - col)
    bias = -np.asarray(slopes, np.float32)[:, None, None] * (step * dist).astype(np.float32)[None]
    bias = np.where(dist[None] <= W, bias, np.float32(NEG)).astype(np.float32)
    return jnp.asarray(bias.reshape(len(slopes) // 2, 2 * Q, Q + 2 * W))


def _alibi_slopes(n):
    return (2.0 ** (-8.0 * np.arange(1, n + 1, dtype=np.float32) / n)).astype(np.float32)


BAND_ROWS = 512


def _band_tiles(Ls, W):
    R = min(Ls, BAND_ROWS)
    Q = min(R, 128)
    assert Ls % R == 0 and R % Q == 0 and R % W == 0 and Ls % W == 0, (Ls, R, Q, W)
    return R, Q


def _banded_attention(q, k, v, slopes, *, W, n_kg, G, sink=None, want_lse=False):
    bsz, r, Ls, Cq = q.shape
    Ck = k.shape[-1]
    R, Q = _band_tiles(Ls, W)
    nchunks = Ls // R
    nblk = Ls // W
    per = R // W
    P = math.gcd(r, max(1, BAND_ROWS // R))
    bias = _band_bias(slopes, r, W, Q)
    q_spec = pl.BlockSpec((None, P, R, Cq), lambda b, j, c: (b, j, c, 0))
    main = pl.BlockSpec((None, P, R, Ck), lambda b, j, c: (b, j, c, 0))
    prev = pl.BlockSpec((None, P, W, Ck), lambda b, j, c: (b, j, jnp.maximum(c * per - 1, 0), 0))
    nxt = pl.BlockSpec((None, P, W, Ck), lambda b, j, c: (b, j, jnp.minimum((c + 1) * per, nblk - 1), 0))
    in_specs = [q_spec, prev, main, nxt, prev, main, nxt, _const_spec(bias.shape)]
    args = [q, k, k, k, v, v, v, bias]
    if sink is not None:
        in_specs.append(pl.BlockSpec(memory_space=pltpu.SMEM))
        args.append(sink)
    out_specs = [q_spec]
    out_shape = [jax.ShapeDtypeStruct(q.shape, BF16)]
    if want_lse:
        out_specs.append(pl.BlockSpec((None, P, R, LSE_LANES), lambda b, j, c: (b, j, c, 0)))
        out_shape.append(jax.ShapeDtypeStruct((bsz, r, Ls, LSE_LANES), F32))
    kern = functools.partial(_banded_kernel, W=W, Q=Q, R=R, n_kg=n_kg, G=G, has_sink=sink is not None,
                             want_lse=want_lse, nchunks=nchunks)
    outs = pl.pallas_call(
        kern,
        grid=(bsz, r // P, nchunks),
        in_specs=in_specs,
        out_specs=out_specs,
        out_shape=out_shape,
        scratch_shapes=[pltpu.VMEM((P, R + 2 * W, Ck), BF16), pltpu.VMEM((P, R + 2 * W, Ck), BF16)],
        compiler_params=_params("parallel", "parallel", "parallel"),
        name=f"banded_w{W}_r{r}",
    )(*args)
    return (outs[0], outs[1]) if want_lse else outs[0]


VT_ROWS = 80
FLASH_STRIP = 256
FLASH_UNROLL = 4


def _flash_kernel(q_ref, k_ref, vt_ref, o_ref, *, tk, nk):
    tq = q_ref.shape[1]
    ns = tq // FLASH_STRIP
    chains = [(h, c) for h in range(2) for c in range(ns)]
    qs = [q_ref[h, c * FLASH_STRIP:(c + 1) * FLASH_STRIP, :] for h, c in chains]

    def body(t, carry):
        off = pl.multiple_of(t * tk, tk)
        kt = [k_ref[h, pl.ds(off, tk), :] for h in range(2)]
        vt = [vt_ref[h, :, pl.ds(off, tk)] for h in range(2)]
        ss = [_dot_nt(kt[h], qs[n]) for n, (h, _) in enumerate(chains)]
        ms = [jnp.maximum(carry[2 * n], jnp.max(ss[n], axis=0, keepdims=True)) for n in range(len(chains))]
        ps = [jnp.exp2(ss[n] - ms[n]).astype(BF16) for n in range(len(chains))]
        out = []
        for n, (h, _) in enumerate(chains):
            alpha = jnp.exp2(carry[2 * n] - ms[n])
            out += [ms[n], carry[2 * n + 1] * alpha + _dot(vt[h], ps[n])]
        return tuple(out)

    m0 = jnp.full((1, FLASH_STRIP), NEG, F32)
    acc0 = jnp.zeros((VT_ROWS, FLASH_STRIP), F32)
    res = lax.fori_loop(0, nk, body, (m0, acc0) * len(chains), unroll=FLASH_UNROLL)
    accs = [jnp.concatenate([res[2 * (h * ns + c) + 1] for c in range(ns)], axis=1) for h in range(2)]
    outs = [acc[:B_V] / acc[B_V:B_V + 1] for acc in accs]
    o_ref[...] = jnp.concatenate(outs, axis=0).T.astype(BF16)


def _flash_tiles(L):
    tq = min(L, 1024)
    tk = min(L, 512)
    assert L % tq == 0 and L % tk == 0
    return tq, tk


def _latent_attention(qb, kb, vt, bsz, L):
    tq, tk = _flash_tiles(L)
    q4 = qb.reshape(B_HEADS, bsz, L, LANES)
    k4 = kb.reshape(B_HEADS, bsz, L, LANES)
    k_spec = pl.BlockSpec((2, None, L, LANES), lambda b, hp, i: (hp, b, 0, 0))
    vt_spec = pl.BlockSpec((2, VT_ROWS, L), lambda b, hp, i: (hp, 0, b))
    out = pl.pallas_call(
        functools.partial(_flash_kernel, tk=tk, nk=L // tk),
        grid=(bsz, B_HEADS // 2, L // tq),
        in_specs=[pl.BlockSpec((2, None, tq, LANES), lambda b, hp, i: (hp, b, i, 0)), k_spec, vt_spec],
        out_specs=pl.BlockSpec((None, tq, LANES), lambda b, hp, i: (b, i, hp)),
        out_shape=jax.ShapeDtypeStruct((bsz, L, B_WIDTH), BF16),
        compiler_params=_params("parallel", "parallel", "parallel"),
        name="latent_flash",
    )(q4, k4, vt)
    return out.reshape(bsz * L, B_WIDTH)


def _mix_out_kernel(*refs):
    it = iter(refs)
    x_ref, ya_ref, yb_ref = next(it), next(it), next(it)
    c_refs = [next(it) for _ in STRIDES]
    l_refs = [next(it) for _ in STRIDES]
    ga_ref, gb_ref, gc_ref, w_ref = next(it), next(it), next(it), next(it)
    kv_ref, gx_ref, wq_ref, wo_ref = next(it), next(it), next(it), next(it)
    o_ref = next(it)
    c_scr = next(it)
    tm = x_ref.shape[0]
    ngrp = C_WIDTH // LANES

    for n, r in enumerate(STRIDES):
        for j in range(r):
            blk = c_refs[n][j].astype(F32)
            for g in range(ngrp):
                c_scr[n * ngrp + g, pl.ds(j, tm // r, stride=r), :] = blk[:, g * LANES:(g + 1) * LANES]

    ls = [l_ref[...] for l_ref in l_refs]
    mx = functools.reduce(jnp.maximum, ls)
    es = [jnp.exp(l - mx) for l in ls]
    tot = functools.reduce(lambda a, b: a + b, es)
    head_of_lane = lax.broadcasted_iota(jnp.int32, (LSE_LANES, C_WIDTH), 1) // HEAD_DIM
    spread = (head_of_lane == lax.broadcasted_iota(jnp.int32, (LSE_LANES, C_WIDTH), 0)).astype(BF16)
    wide = []
    for e in es:
        w = e / tot
        hi = w.astype(BF16)
        lo = (w - hi.astype(F32)).astype(BF16)
        wide.append(_dot(hi, spread) + _dot(lo, spread))
    groups = []
    for kg in range(ngrp):
        acc = jnp.zeros((tm, LANES), F32)
        for n, w in enumerate(wide):
            acc = acc + c_scr[n * ngrp + kg] * w[:, kg * LANES:(kg + 1) * LANES]
        groups.append(acc)
    yc = jnp.concatenate(groups, axis=-1)
    y = jnp.concatenate([
        _rms(ya_ref[...].astype(F32), ga_ref[...]).astype(BF16),
        _rms(yb_ref[...].astype(F32), gb_ref[...]).astype(BF16),
        _rms(yc, gc_ref[...]).astype(BF16)], axis=-1)
    x = x_ref[...] + _dot(y, w_ref[...])
    o_ref[...] = _cross_attention(x, kv_ref, gx_ref, wq_ref, wo_ref)


def _mix_out(x2, ya, yb, ycs, lses, kvm, lw, L, tm):
    T = x2.shape[0]
    nl = L // tm
    row = lambda w: pl.BlockSpec((tm, w), lambda i: (i, 0))
    weights = (lw['g_out_a'], lw['g_out_b'], lw['g_out_c'], lw['w_out'])
    x_weights = (lw['g_x'], lw['w_xq'], lw['w_xo'])
    n = len(STRIDES)
    return pl.pallas_call(
        _mix_out_kernel,
        grid=(T // tm,),
        in_specs=([row(D_MODEL), row(A_WIDTH), row(B_WIDTH)]
                  + [_phase_spec(r, tm, nl, C_WIDTH) for r in STRIDES]
                  + [row(LSE_LANES)] * n
                  + [_const_spec(w.shape) for w in weights]
                  + [pl.BlockSpec((None,) + kvm.shape[1:], lambda i: (i // nl, 0, 0))]
                  + [_const_spec(w.shape) for w in x_weights]),
        out_specs=row(D_MODEL),
        out_shape=jax.ShapeDtypeStruct((T, D_MODEL), F32),
        scratch_shapes=[pltpu.VMEM((n * C_WIDTH // LANES, tm, LANES), F32)],
        compiler_params=_params("parallel"),
        name="mix_out_cross",
    )(x2, ya, yb, *ycs, *lses, *weights, kvm, *x_weights)


def _norm_matmul_kernel(x_ref, g_ref, w_ref, o_ref):
    o_ref[...] = _dot(_rms(x_ref[...], g_ref[...]).astype(BF16), w_ref[...]).astype(o_ref.dtype)


def _norm_matmul(x2, g, w, tm):
    T, K = x2.shape
    N = w.shape[1]
    return pl.pallas_call(
        _norm_matmul_kernel,
        grid=(T // tm,),
        in_specs=[pl.BlockSpec((tm, K), lambda i: (i, 0)), _const_spec(g.shape), _const_spec(w.shape)],
        out_specs=pl.BlockSpec((tm, N), lambda i: (i, 0)),
        out_shape=jax.ShapeDtypeStruct((T, N), BF16),
        compiler_params=_params("parallel"),
        name="norm_matmul",
    )(x2, g, w)


def _cross_attention(x, kv_ref, g_ref, wq_ref, wo_ref):
    h = _rms(x, g_ref[...]).astype(BF16)
    q = (_dot(h, wq_ref[...]) * (X_HEAD_DIM ** -0.5)).astype(BF16)
    xd = X_HEADS * X_HEAD_DIM
    outs = []
    for hd in range(X_HEADS):
        sl = slice(hd * X_HEAD_DIM, (hd + 1) * X_HEAD_DIM)
        k = kv_ref[:, sl]
        v = kv_ref[:, xd + hd * X_HEAD_DIM:xd + (hd + 1) * X_HEAD_DIM]
        s = _dot_nt(q[:, sl], k)
        p = jnp.exp(s - jnp.max(s, axis=-1, keepdims=True))
        den = jnp.sum(p, axis=-1, keepdims=True)
        outs.append((_dot(p.astype(BF16), v) / den).astype(BF16))
    return x + _dot(jnp.concatenate(outs, axis=-1), wo_ref[...])


def _silu_mul(gate, up):
    return gate * jax.nn.sigmoid(gate) * up


def _swiglu_kernel(x_ref, g_ref, wg_ref, wu_ref, wd_ref, o_ref, h_scr, acc_scr):
    j = pl.program_id(1)

    @pl.when(j == 0)
    def _():
        h_scr[...] = _rms(x_ref[...], g_ref[...]).astype(BF16)
        acc_scr[...] = jnp.zeros_like(acc_scr)

    h = h_scr[...]
    act = _silu_mul(_dot(h, wg_ref[...]), _dot(h, wu_ref[...])).astype(BF16)
    acc_scr[...] += _dot(act, wd_ref[...])

    @pl.when(j == pl.num_programs(1) - 1)
    def _():
        o_ref[...] = x_ref[...] + acc_scr[...]


def _ff_chunk(dff, limit):
    for c in (1792, 1408, 1024, 512, 256, 128):
        if c <= limit and dff % c == 0:
            return c
    raise ValueError(dff)


def _swiglu(x2, g, wg, wu, wd, tm):
    T = x2.shape[0]
    dff = wg.shape[1]
    ck = _ff_chunk(dff, 1408)
    return pl.pallas_call(
        _swiglu_kernel,
        grid=(T // tm, dff // ck),
        in_specs=[pl.BlockSpec((tm, D_MODEL), lambda i, j: (i, 0)),
                  _const_spec(g.shape),
                  pl.BlockSpec((D_MODEL, ck), lambda i, j: (0, j)),
                  pl.BlockSpec((D_MODEL, ck), lambda i, j: (0, j)),
                  pl.BlockSpec((ck, D_MODEL), lambda i, j: (j, 0))],
        out_specs=pl.BlockSpec((tm, D_MODEL), lambda i, j: (i, 0)),
        out_shape=jax.ShapeDtypeStruct((T, D_MODEL), F32),
        scratch_shapes=[pltpu.VMEM((tm, D_MODEL), BF16), pltpu.VMEM((tm, D_MODEL), F32)],
        compiler_params=_params("parallel", "arbitrary"),
        name="swiglu_dense",
    )(x2, g, wg, wu, wd)


def _router_logits(h, wr):
    return jnp.dot(h, wr, preferred_element_type=F32, precision=lax.Precision.HIGHEST)


def _router_kernel(x_ref, g_ref, wr_ref, tri_ref, c0_ref, route_ref, cnt_ref, run_scr):
    @pl.when(pl.program_id(0) == 0)
    def _():
        run_scr[...] = c0_ref[...]

    logits = _router_logits(_rms(x_ref[...], g_ref[...]), wr_ref[...])
    tm = logits.shape[0]
    lane = lax.broadcasted_iota(jnp.int32, (tm, LANES), 1)
    logits = jnp.where(lane < N_EXPERTS, logits, NEG)
    v1 = jnp.max(logits, axis=-1, keepdims=True)
    i1 = jnp.min(jnp.where(logits == v1, lane, LANES), axis=-1, keepdims=True)
    rest = jnp.where(lane == i1, NEG, logits)
    v2 = jnp.max(rest, axis=-1, keepdims=True)
    i2 = jnp.min(jnp.where(rest == v2, lane, LANES), axis=-1, keepdims=True)
    pid = jnp.minimum(i1, i2) * N_EXPERTS + jnp.maximum(i1, i2)
    onehot = (lane == pid).astype(F32)
    before = _dot(tri_ref[...], onehot.astype(BF16)) + run_scr[...]
    rank = jnp.sum(onehot * before, axis=-1, keepdims=True)
    run_scr[...] += jnp.sum(onehot, axis=0, keepdims=True)
    cnt_ref[...] = run_scr[...]
    lane8 = lax.broadcasted_iota(jnp.int32, (tm, LSE_LANES), 1)
    route_ref[...] = jnp.where(lane8 == 0, pid, jnp.where(lane8 == 1, rank.astype(jnp.int32), 0))


def _router(x2, g, wr, counts0, tm):
    T = x2.shape[0]
    tri = (np.arange(tm)[:, None] > np.arange(tm)[None, :]).astype(np.float32)
    tri = jnp.asarray(tri, BF16)
    return pl.pallas_call(
        _router_kernel,
        grid=(T // tm,),
        in_specs=[pl.BlockSpec((tm, D_MODEL), lambda i: (i, 0)), _const_spec(g.shape), _const_spec(wr.shape),
                  _const_spec(tri.shape), _const_spec(counts0.shape)],
        out_specs=[pl.BlockSpec((tm, LSE_LANES), lambda i: (i, 0)), _const_spec(counts0.shape)],
        out_shape=[jax.ShapeDtypeStruct((T, LSE_LANES), jnp.int32), jax.ShapeDtypeStruct(counts0.shape, F32)],
        scratch_shapes=[pltpu.VMEM(counts0.shape, F32)],
        compiler_params=_params("arbitrary"),
        name="router",
    )(x2, g, wr, tri, counts0)


ROW_SUBLANES = D_MODEL // LANES


def _slab_to_rows(ref, n):
    return jnp.concatenate([ref[pl.ds(c, n, stride=ROW_SUBLANES), :] for c in range(ROW_SUBLANES)], axis=-1)


def _rows_to_slab(ref, rows, n):
    for c in range(ROW_SUBLANES):
        ref[pl.ds(c, n, stride=ROW_SUBLANES), :] = rows[:, c * LANES:(c + 1) * LANES]


def _slab(ref, row):
    return ref.at[pl.ds(pl.multiple_of(row * ROW_SUBLANES, ROW_SUBLANES), ROW_SUBLANES)]


MOVE_UNROLL = 8


def _move_rows(n, copy_of_row, copy_of_all):
    def issue(r, _):
        copy_of_row(r).start()
        return 0

    lax.fori_loop(0, n, issue, 0, unroll=MOVE_UNROLL)
    copy_of_all().wait()


def _dispatch_kernel(slot_ref, x_ref, prev_ref, o_ref, slab, sem, *, tm):
    del prev_ref
    _rows_to_slab(slab, x_ref[...], tm)

    def copy_of_row(r):
        slot = slot_ref[0, r]
        return pltpu.make_async_copy(_slab(slab, r), _slab(o_ref, slot), sem)

    def copy_of_all():
        return pltpu.make_async_copy(slab, o_ref.at[pl.ds(0, tm * ROW_SUBLANES)], sem)

    _move_rows(tm, copy_of_row, copy_of_all)


def _slots_kernel(route_ref, start_ref, o_ref):
    route = route_ref[...]
    tm = route.shape[0]
    lane = lax.broadcasted_iota(jnp.int32, (tm, LANES), 1)
    first = jnp.sum(jnp.where(lane == route[:, 0:1], start_ref[...], 0), axis=-1, keepdims=True)
    o_ref[...] = jnp.broadcast_to(first + route[:, 1:2], o_ref.shape)


def _slots(route, start, tm):
    T = route.shape[0]
    slots = pl.pallas_call(
        _slots_kernel,
        grid=(T // tm,),
        in_specs=[pl.BlockSpec((tm, LSE_LANES), lambda i: (i, 0)), _const_spec(start.shape)],
        out_specs=pl.BlockSpec((tm, LSE_LANES), lambda i: (i, 0)),
        out_shape=jax.ShapeDtypeStruct((T, LSE_LANES), jnp.int32),
        compiler_params=_params("parallel"),
        name="slots",
    )(route, start)
    return slots[:, 0].reshape(T // tm, 1, tm)


def _dispatch(x2, slots, sorted_rows, tm):
    T = x2.shape[0]
    return pl.pallas_call(
        functools.partial(_dispatch_kernel, tm=tm),
        grid=(T // tm,),
        in_specs=[pl.BlockSpec((None, 1, tm), lambda i: (i, 0, 0), memory_space=pltpu.SMEM),
                  pl.BlockSpec((tm, D_MODEL), lambda i: (i, 0)),
                  pl.BlockSpec(memory_space=pl.ANY)],
        out_specs=pl.BlockSpec(memory_space=pl.ANY),
        out_shape=jax.ShapeDtypeStruct(sorted_rows.shape, sorted_rows.dtype),
        scratch_shapes=[pltpu.VMEM((tm * ROW_SUBLANES, LANES), F32), pltpu.SemaphoreType.DMA],
        input_output_aliases={2: 0},
        compiler_params=pltpu.CompilerParams(dimension_semantics=("arbitrary",), vmem_limit_bytes=VMEM_LIMIT,
                                             has_side_effects=True),
        name="dispatch",
    )(slots, x2, sorted_rows)


def _unpermute_kernel(slot_ref, y_ref, o_ref, slab, sem, *, tm):
    def copy_of_row(r):
        slot = slot_ref[0, r]
        return pltpu.make_async_copy(_slab(y_ref, slot), _slab(slab, r), sem)

    def copy_of_all():
        return pltpu.make_async_copy(y_ref.at[pl.ds(0, tm * ROW_SUBLANES)], slab, sem)

    _move_rows(tm, copy_of_row, copy_of_all)
    o_ref[...] = _slab_to_rows(slab, tm)


def _unpermute(y_sorted, slots, T, tm):
    return pl.pallas_call(
        functools.partial(_unpermute_kernel, tm=tm),
        grid=(T // tm,),
        in_specs=[pl.BlockSpec((None, 1, tm), lambda i: (i, 0, 0), memory_space=pltpu.SMEM),
                  pl.BlockSpec(memory_space=pl.ANY)],
        out_specs=pl.BlockSpec((tm, D_MODEL), lambda i: (i, 0)),
        out_shape=jax.ShapeDtypeStruct((T, D_MODEL), F32),
        scratch_shapes=[pltpu.VMEM((tm * ROW_SUBLANES, LANES), F32), pltpu.SemaphoreType.DMA],
        compiler_params=_params("arbitrary"),
        name="unpermute",
    )(slots, y_sorted)


def _pair_experts_kernel(ta_ref, tb_ref, nv_ref, xs_ref, g_ref, wr_ref, wg_ref, wu_ref, wd_ref, gf_ref, o_ref,
                         x_scr, h_scr, gate_scr, acc_scr, *, nj, final_norm):
    i = pl.program_id(0)
    j = pl.program_id(1)
    tm = x_scr.shape[0]

    @pl.when(i < nv_ref[0])
    def _():
        @pl.when(j == 0)
        def _():
            x = _slab_to_rows(xs_ref, tm)
            x_scr[...] = x
            h = _rms(x, g_ref[...])
            h_scr[...] = h.astype(BF16)
            la = jnp.sum(h * wr_ref[pl.ds(ta_ref[i], 1), :], axis=-1, keepdims=True)
            lb = jnp.sum(h * wr_ref[pl.ds(tb_ref[i], 1), :], axis=-1, keepdims=True)
            mx = jnp.maximum(la, lb)
            ea, eb = jnp.exp(la - mx), jnp.exp(lb - mx)
            gate_scr[0] = ea / (ea + eb)
            gate_scr[1] = eb / (ea + eb)
            acc_scr[...] = jnp.zeros_like(acc_scr)

        h = h_scr[...]
        gate = jnp.where(j < nj, gate_scr[0], gate_scr[1])
        act = (_silu_mul(_dot(h, wg_ref[...]), _dot(h, wu_ref[...])) * gate).astype(BF16)
        acc_scr[...] += _dot(act, wd_ref[...])

        @pl.when(j == 2 * nj - 1)
        def _():
            out = x_scr[...] + acc_scr[...]
            if final_norm:
                out = _rms(out, gf_ref[...])
            _rows_to_slab(o_ref, out, tm)

    @pl.when(jnp.logical_and(i >= nv_ref[0], j == 2 * nj - 1))
    def _():
        o_ref[...] = jnp.zeros_like(o_ref)


def _pair_experts(xs, g, wr, wg, wu, wd, g_final, tile_a, tile_b, n_valid, tm, final_norm):
    rows = xs.shape[0] // ROW_SUBLANES
    dff = wg.shape[2]
    ck = _ff_chunk(dff, 1792)
    nj = dff // ck

    def expert(i, j, ta, tb):
        return jnp.where(j < nj, ta[i], tb[i])

    def chunk(i, j, nv):
        return jnp.where(i < nv[0], j % nj, nj - 1)

    slab_spec = pl.BlockSpec((tm * ROW_SUBLANES, LANES), lambda i, j, ta, tb, nv: (i, 0))
    const = lambda a: pl.BlockSpec(a.shape, lambda i, j, ta, tb, nv: (0,) * a.ndim)
    grid_spec = pltpu.PrefetchScalarGridSpec(
        num_scalar_prefetch=3,
        grid=(rows // tm, 2 * nj),
        in_specs=[slab_spec, const(g), const(wr),
                  pl.BlockSpec((None, D_MODEL, ck), lambda i, j, ta, tb, nv: (expert(i, j, ta, tb), 0, chunk(i, j, nv))),
                  pl.BlockSpec((None, D_MODEL, ck), lambda i, j, ta, tb, nv: (expert(i, j, ta, tb), 0, chunk(i, j, nv))),
                  pl.BlockSpec((None, ck, D_MODEL), lambda i, j, ta, tb, nv: (expert(i, j, ta, tb), chunk(i, j, nv), 0)),
                  const(g_final)],
        out_specs=slab_spec,
        scratch_shapes=[pltpu.VMEM((tm, D_MODEL), F32), pltpu.VMEM((tm, D_MODEL), BF16),
                        pltpu.VMEM((2, tm, 1), F32), pltpu.VMEM((tm, D_MODEL), F32)],
    )
    return pl.pallas_call(
        functools.partial(_pair_experts_kernel, nj=nj, final_norm=final_norm),
        grid_spec=grid_spec,
        out_shape=jax.ShapeDtypeStruct(xs.shape, F32),
        compiler_params=_params("parallel", "arbitrary"),
        name="pair_experts",
    )(tile_a, tile_b, n_valid, xs, g, wr, wg, wu, wd, g_final)


def _plan_pairs(counts, n_tiles, tm):
    counts = counts.reshape(-1).astype(jnp.int32)
    tiles = (counts + tm - 1) // tm
    tile_end = jnp.cumsum(tiles)
    start = (tile_end - tiles) * tm
    n_valid = tile_end[-1:]
    tile_ids = jnp.minimum(jnp.arange(n_tiles), n_valid[0] - 1)
    tile_pid = jnp.sum(tile_ids[:, None] >= tile_end[None, :], axis=-1)
    tile_a = (tile_pid // N_EXPERTS).astype(jnp.int32)
    tile_b = (tile_pid % N_EXPERTS).astype(jnp.int32)
    return start.astype(jnp.int32), tile_a, tile_b, n_valid.astype(jnp.int32)


def _prep_layer(l, p):
    offs = np.cumsum((0,) + IN_SIZES)
    w_in = p['w_in'][l]
    qa, ka, va, cq, ckv, kr, qc, kc, vc = [w_in[:, int(offs[i]):int(offs[i + 1])] for i in range(9)]
    scale = HEAD_DIM ** -0.5
    order = list(A_HEAD_ORDER)
    qa = qa.reshape(D_MODEL, A_Q_HEADS, HEAD_DIM)[:, order, :].reshape(D_MODEL, A_WIDTH) * scale
    half = B_ROPE // 2
    z = lambda *s: jnp.zeros(s, F32)
    kr_pad = jnp.concatenate([z(D_MODEL, B_NOPE), kr, z(D_MODEL, LANES - B_NOPE - B_ROPE)], axis=1)
    kr_rot = jnp.concatenate([z(D_MODEL, B_NOPE), -kr[:, half:], kr[:, :half], z(D_MODEL, LANES - B_NOPE - B_ROPE)], axis=1)
    uq = p['mla_w_uq'][l].reshape(B_Q_LORA, B_HEADS, B_NOPE + B_ROPE)
    padq = z(B_Q_LORA, B_HEADS, LANES - B_NOPE - B_ROPE)
    uq_pad = jnp.concatenate([uq, padq], axis=-1).reshape(B_Q_LORA, B_HEADS * LANES)
    uq_rot = jnp.concatenate([z(B_Q_LORA, B_HEADS, B_NOPE), -uq[..., B_NOPE + half:], uq[..., B_NOPE:B_NOPE + half], padq],
                             axis=-1).reshape(B_Q_LORA, B_HEADS * LANES)
    ukv = p['mla_w_ukv'][l].reshape(B_KV_LORA, B_HEADS, B_NOPE + B_V)
    zk = z(B_KV_LORA, B_HEADS, HALF)
    uk_pad = jnp.concatenate([ukv[..., :B_NOPE], zk], axis=-1).reshape(B_KV_LORA, B_HEADS * LANES)
    uv_pad = jnp.concatenate([ukv[..., B_NOPE:], zk], axis=-1).reshape(B_KV_LORA, B_HEADS * LANES)

    g_out = p['mix_out_norm_g'][l]
    w_out = p['w_out'][l]
    a_rows = np.concatenate([np.arange(HEAD_DIM) + HEAD_DIM * h for h in A_HEAD_ORDER])
    w_out = jnp.concatenate([w_out[:A_WIDTH][a_rows], w_out[A_WIDTH:]], axis=0)
    g_out_a = g_out[:A_WIDTH][a_rows]
    row = lambda v: v.reshape(1, -1).astype(F32)
    return {
        'g_mix': row(p['norm_mix_g'][l]),
        'wa': jnp.concatenate([qa, ka, va], axis=1).astype(BF16),
        'wc': jnp.concatenate([qc * scale, kc, vc], axis=1).astype(BF16),
        'wb': jnp.concatenate([cq, ckv], axis=1).astype(BF16),
        'wkr': jnp.concatenate([kr_pad, kr_rot], axis=1).astype(BF16),
        'g_q': row(p['mla_q_norm_g'][l]),
        'g_kv': row(p['mla_kv_norm_g'][l]),
        'wuq': jnp.concatenate([uq_pad, uq_rot], axis=1).astype(BF16),
        'wukv': jnp.concatenate([uk_pad, uv_pad], axis=1).astype(BF16),
        'sink': p['sink_a'][l][np.array(A_HEAD_ORDER)].astype(F32),
        'g_out_a': row(g_out_a),
        'g_out_b': row(g_out[A_WIDTH:A_WIDTH + B_WIDTH]),
        'g_out_c': row(g_out[A_WIDTH + B_WIDTH:]),
        'w_out': w_out.astype(BF16),
        'g_x': row(p['norm_x_g'][l]),
        'g_mem': row(p['norm_mem_g'][l]),
        'w_xq': p['w_xq'][l].astype(BF16),
        'w_xkv': p['w_xkv'][l].astype(BF16),
        'w_xo': p['w_xo'][l].astype(BF16),
        'g_ffn': row(p['norm_ffn_g'][l]),
    }


def _prep_ffn(w_gu, w_down):
    dff = w_down.shape[-2]
    return w_gu[..., :dff].astype(BF16), w_gu[..., dff:].astype(BF16), w_down.astype(BF16)


def _rope_tables(L):
    half = B_ROPE // 2
    pos = jnp.arange(L, dtype=F32)
    inv = ROPE_THETA ** (-jnp.arange(half, dtype=F32) / half)
    ang = pos[:, None] * inv[None, :]
    cos2 = jnp.tile(jnp.cos(ang), (1, 2))
    sin2 = jnp.tile(jnp.sin(ang), (1, 2))
    pad = jnp.zeros((L, LANES - B_NOPE - B_ROPE), F32)
    qscale = (B_NOPE + B_ROPE) ** -0.5 * LOG2E
    cq = jnp.concatenate([jnp.ones((L, B_NOPE), F32), cos2, pad], axis=1) * qscale
    sq = jnp.concatenate([jnp.zeros((L, B_NOPE), F32), sin2, pad], axis=1) * qscale
    ck = jnp.concatenate([jnp.zeros((L, B_NOPE), F32), cos2, pad], axis=1)
    sk = jnp.concatenate([jnp.zeros((L, B_NOPE), F32), sin2, pad], axis=1)
    return cq, sq, ck, sk


TOKEN_TILE = 512
EXPERT_TILE = 512
FFN_TILE = 1024
MOVE_TILE = 1024
N_PAIRS_USED = N_EXPERTS * (N_EXPERTS - 1) // 2


def _mixer_and_cross(x2, mem, lw, bsz, L):
    T = bsz * L
    tm = min(TOKEN_TILE, L)
    tabs = _rope_tables(L)
    slopes_a = _alibi_slopes(A_Q_HEADS)[list(A_HEAD_ORDER)]
    slopes_c = _alibi_slopes(C_HEADS)
    (qa, ka, va), qkv_c, (qb, kb, vb) = _in_proj(x2, lw, tabs, bsz, L, tm)
    as_seq = lambda t: t.reshape(bsz, 1, L, t.shape[-1])
    ya = _banded_attention(as_seq(qa), as_seq(ka), as_seq(va), slopes_a, W=A_WINDOW, n_kg=1, G=2, sink=lw['sink'])
    ya = ya.reshape(T, A_WIDTH)
    yb = _latent_attention(qb, kb, vb, bsz, L)
    ycs, lses = [], []
    for (w, r), (qc, kc, vc) in zip(DILATED_PAIRS, qkv_c):
        o, lse = _banded_attention(qc, kc, vc, slopes_c, W=w // (2 * r), n_kg=C_HEADS // 2, G=1, want_lse=True)
        ycs.append(o)
        lses.append(jnp.transpose(lse, (0, 2, 1, 3)).reshape(T, LSE_LANES))
    mem2 = mem.reshape(-1, D_MODEL)
    kvm = _norm_matmul(mem2, lw['g_mem'], lw['w_xkv'], min(512, mem2.shape[0]))
    return _mix_out(x2, ya, yb, ycs, lses, kvm.reshape(bsz, mem.shape[1], -1), lw, L, tm)


def _routed_swiglu(xs2, g, wr, wg, wu, wd, final_g, final_norm):
    total = sum(x.shape[0] for x in xs2)
    et = min(EXPERT_TILE, total)
    n_tiles = total // et + N_PAIRS_USED
    counts = jnp.zeros((1, LANES), F32)
    routes = []
    for x2 in xs2:
        route, counts = _router(x2, g, wr, counts, min(TOKEN_TILE, x2.shape[0]))
        routes.append(route)
    start, tile_a, tile_b, n_valid = _plan_pairs(counts, n_tiles, et)
    move = [min(MOVE_TILE, x2.shape[0]) for x2 in xs2]
    slots = [_slots(route, start.reshape(1, LANES), mt) for route, mt in zip(routes, move)]
    sorted_rows = jnp.zeros((n_tiles * et * ROW_SUBLANES, LANES), F32)
    for x2, sl, mt in zip(xs2, slots, move):
        sorted_rows = _dispatch(x2, sl, sorted_rows, mt)
    wr_rows = wr[:, :N_EXPERTS].T
    y_sorted = _pair_experts(sorted_rows, g, wr_rows, wg, wu, wd, final_g, tile_a, tile_b, n_valid, et, final_norm)
    return [_unpermute(y_sorted, sl, x2.shape[0], mt) for x2, sl, mt in zip(xs2, slots, move)]


def _encoder(groups, layers, ffn, moe, routers, final_g):
    shapes = [x.shape for x, _ in groups]
    xs2 = [x.reshape(-1, D_MODEL) for x, _ in groups]
    depth = len(layers)
    normed = False
    for l, lw in enumerate(layers):
        xs2 = [_mixer_and_cross(x2, mem, lw, shp[0], shp[1]) for x2, (_, mem), shp in zip(xs2, groups, shapes)]
        if l % 2 == 0:
            wg, wu, wd = ffn[l // 2]
            xs2 = [_swiglu(x2, lw['g_ffn'], wg, wu, wd, min(FFN_TILE, x2.shape[0])) for x2 in xs2]
        else:
            wg, wu, wd = moe[l // 2]
            normed = l == depth - 1
            xs2 = _routed_swiglu(xs2, lw['g_ffn'], routers[l // 2], wg, wu, wd, final_g, final_norm=normed)
    assert normed, "the final rmsnorm is fused into the last layer's routed SwiGLU"
    return tuple(x2.reshape(shp) for x2, shp in zip(xs2, shapes))


def kernel(x_prompt, x_sample, mem_prompt, mem_sample, norm_mix_g, w_in, sink_a, mla_q_norm_g, mla_kv_norm_g, mla_w_uq, mla_w_ukv, mix_out_norm_g, w_out, norm_x_g, norm_mem_g, w_xq, w_xkv, w_xo, norm_ffn_g, ffn_w_gu, ffn_w_down, moe_router, moe_w_gu, moe_w_down, final_norm_g):
    p = dict(norm_mix_g=norm_mix_g, w_in=w_in, sink_a=sink_a, mla_q_norm_g=mla_q_norm_g, mla_kv_norm_g=mla_kv_norm_g,
             mla_w_uq=mla_w_uq, mla_w_ukv=mla_w_ukv, mix_out_norm_g=mix_out_norm_g, w_out=w_out, norm_x_g=norm_x_g,
             norm_mem_g=norm_mem_g, w_xq=w_xq, w_xkv=w_xkv, w_xo=w_xo, norm_ffn_g=norm_ffn_g)
    depth = w_in.shape[0]
    layers = [_prep_layer(l, p) for l in range(depth)]
    ffn = [_prep_ffn(ffn_w_gu[i], ffn_w_down[i]) for i in range(ffn_w_gu.shape[0])]
    moe = [_prep_ffn(moe_w_gu[i], moe_w_down[i]) for i in range(moe_w_gu.shape[0])]
    routers = [jnp.pad(moe_router[i].astype(F32), ((0, 0), (0, LANES - N_EXPERTS))) for i in range(moe_router.shape[0])]
    final_g = final_norm_g.reshape(1, -1).astype(F32)
    return _encoder([(x_prompt, mem_prompt), (x_sample, mem_sample)], layers, ffn, moe, routers, final_g)
```

```python
import functools
import math

import numpy as np
import jax
import jax.numpy as jnp
from jax import lax
from jax.experimental import pallas as pl
from jax.experimental.pallas import tpu as pltpu

D_MODEL = 1024
HEAD_DIM = 64
A_Q_HEADS = 4
A_KV_HEADS = 2
A_WINDOW = 128
B_HEADS = 6
B_NOPE = 64
B_ROPE = 32
B_V = 64
B_Q_LORA = 384
B_KV_LORA = 256
ROPE_THETA = 10000.0
C_HEADS = 6
DILATED_PAIRS = ((128, 1), (512, 4), (2048, 16))
STRIDES = tuple(r for _, r in DILATED_PAIRS)
A_WIDTH = A_Q_HEADS * HEAD_DIM
B_WIDTH = B_HEADS * B_V
C_WIDTH = C_HEADS * HEAD_DIM
IN_SIZES = (A_WIDTH, A_KV_HEADS * HEAD_DIM, A_KV_HEADS * HEAD_DIM, B_Q_LORA, B_KV_LORA, B_ROPE,
            C_WIDTH, C_WIDTH, C_WIDTH)
X_HEADS = 4
X_HEAD_DIM = 128
N_EXPERTS = 8
EPS = 1e-6
NEG = -1e30

LANES = 128
HALF = LANES // 2
LSE_LANES = 8
VMEM_LIMIT = 56 * 1024 * 1024
LOG2E = math.log2(math.e)

BF16 = jnp.bfloat16
F32 = jnp.float32

A_HEAD_ORDER = (0, 2, 1, 3)


def _params(*sem):
    return pltpu.CompilerParams(dimension_semantics=sem, vmem_limit_bytes=VMEM_LIMIT)


def _rms(x, g):
    return x * lax.rsqrt(jnp.mean(x * x, axis=-1, keepdims=True) + EPS) * g


def _dot(a, b):
    return jnp.dot(a, b, preferred_element_type=F32)


def _dot_nt(a, b):
    return lax.dot_general(a, b, (((1,), (1,)), ((), ())), preferred_element_type=F32)


def _const_spec(shape):
    n = len(shape)
    return pl.BlockSpec(shape, lambda *_: (0,) * n)


def _phase_spec(r, tm, nl, width):
    return pl.BlockSpec((None, r, tm // r, width), lambda i: (i // nl, 0, i % nl, 0))


def _in_kernel(*refs):
    it = iter(refs)
    x_ref, g_ref, wa_ref, wc_ref, wb_ref, wkr_ref, gq_ref, gkv_ref, wuq_ref, wukv_ref = [next(it) for _ in range(10)]
    cq_ref, sq_ref, ck_ref, sk_ref = [next(it) for _ in range(4)]
    qa_ref, ka_ref, va_ref = next(it), next(it), next(it)
    c_refs = [(next(it), next(it), next(it)) for _ in STRIDES]
    qb_ref, kb_ref, vb_ref = next(it), next(it), next(it)
    zc_scr = next(it)
    tm = x_ref.shape[0]

    h = _rms(x_ref[...], g_ref[...]).astype(BF16)
    za = _dot(h, wa_ref[...])
    qa_ref[...] = za[:, :A_WIDTH].astype(BF16)
    ka_ref[...] = za[:, A_WIDTH:A_WIDTH + LANES].astype(BF16)
    va_ref[...] = za[:, A_WIDTH + LANES:].astype(BF16)

    zc = _dot(h, wc_ref[...])
    ngrp = C_WIDTH // LANES
    for g in range(3 * ngrp):
        zc_scr[g] = zc[:, g * LANES:(g + 1) * LANES]
    for r, qkv_refs in zip(STRIDES, c_refs):
        for j in range(r):
            for n, ref in enumerate(qkv_refs):
                ref[j] = jnp.concatenate(
                    [zc_scr[n * ngrp + g, pl.ds(j, tm // r, stride=r), :] for g in range(ngrp)], axis=-1).astype(BF16)

    zb = _dot(h, wb_ref[...])
    hq = _rms(zb[:, :B_Q_LORA], gq_ref[...]).astype(BF16)
    hkv = _rms(zb[:, B_Q_LORA:], gkv_ref[...]).astype(BF16)
    zq = _dot(hq, wuq_ref[...])
    zkv = _dot(hkv, wukv_ref[...])
    zkr = _dot(h, wkr_ref[...])
    kr = zkr[:, :LANES] * ck_ref[...] + zkr[:, LANES:] * sk_ref[...]
    cq = cq_ref[...]
    sq = sq_ref[...]
    lane = lax.broadcasted_iota(jnp.int32, (1, LANES), 1)
    nb = B_HEADS * LANES
    for hd in range(B_HEADS):
        lo, hi = hd * LANES, (hd + 1) * LANES
        qb_ref[hd] = (zq[:, lo:hi] * cq + zq[:, nb + lo:nb + hi] * sq).astype(BF16)
        kb_ref[hd] = (zkv[:, lo:hi] + kr).astype(BF16)
        ones = (lane == B_V).astype(F32)
        vb_ref[hd] = (zkv[:, nb + lo:nb + hi] + ones).T.astype(BF16)


def _in_proj(x2, lw, tabs, bsz, L, tm):
    T = x2.shape[0]
    nl = L // tm
    row = lambda w: pl.BlockSpec((tm, w), lambda i: (i, 0))
    tab = pl.BlockSpec((tm, LANES), lambda i: (i % nl, 0))
    hm = pl.BlockSpec((B_HEADS, tm, LANES), lambda i: (0, i, 0))
    weights = (lw['g_mix'], lw['wa'], lw['wc'], lw['wb'], lw['wkr'], lw['g_q'], lw['g_kv'], lw['wuq'], lw['wukv'])
    out_specs = [row(A_WIDTH), row(LANES), row(LANES)]
    out_shape = [jax.ShapeDtypeStruct((T, w), BF16) for w in (A_WIDTH, LANES, LANES)]
    for r in STRIDES:
        assert tm % (16 * r) == 0, (tm, r)
        out_specs += [_phase_spec(r, tm, nl, C_WIDTH)] * 3
        out_shape += [jax.ShapeDtypeStruct((bsz, r, L // r, C_WIDTH), BF16)] * 3
    out_specs += [hm, hm, pl.BlockSpec((B_HEADS, LANES, tm), lambda i: (0, 0, i))]
    out_shape += [jax.ShapeDtypeStruct((B_HEADS, T, LANES), BF16)] * 2
    out_shape += [jax.ShapeDtypeStruct((B_HEADS, LANES, T), BF16)]
    outs = pl.pallas_call(
        _in_kernel,
        grid=(T // tm,),
        in_specs=[row(D_MODEL)] + [_const_spec(w.shape) for w in weights] + [tab] * 4,
        out_specs=out_specs,
        out_shape=out_shape,
        scratch_shapes=[pltpu.VMEM((3 * C_WIDTH // LANES, tm, LANES), F32)],
        compiler_params=_params("parallel"),
        name="in_proj",
    )(x2, *weights, *tabs)
    qkv_c = [outs[3 + 3 * n:6 + 3 * n] for n in range(len(STRIDES))]
    return outs[:3], qkv_c, outs[-3:]


def _banded_kernel(*refs, W, Q, R, n_kg, G, has_sink, want_lse, nchunks):
    it = iter(refs)
    q_ref = next(it)
    kp_ref, km_ref, kn_ref = next(it), next(it), next(it)
    vp_ref, vm_ref, vn_ref = next(it), next(it), next(it)
    bias_ref = next(it)
    sink_ref = next(it) if has_sink else None
    o_ref = next(it)
    lse_ref = next(it) if want_lse else None
    kfull, vfull = next(it), next(it)

    c = pl.program_id(2)
    P = q_ref.shape[0]
    kfull[:, 0:W] = kp_ref[...]
    kfull[:, W:W + R] = km_ref[...]
    kfull[:, W + R:] = kn_ref[...]
    vfull[:, 0:W] = vp_ref[...]
    vfull[:, W:W + R] = vm_ref[...]
    vfull[:, W + R:] = vn_ref[...]

    nsub = R // Q
    win = Q + 2 * W
    lo = lax.broadcasted_iota(jnp.int32, (Q, LANES), 1) < HALF
    col = lax.broadcasted_iota(jnp.int32, (2 * Q, win), 1)
    top = lax.broadcasted_iota(jnp.int32, (2 * Q, 1), 0) < Q
    lane8 = lax.broadcasted_iota(jnp.int32, (Q, LSE_LANES), 1)
    before_start = jnp.logical_and(c == 0, col < W)
    after_end = jnp.logical_and(c == nchunks - 1, col >= Q + W)
    units = [(ph, i, kg, g) for ph in range(P) for i in range(nsub) for kg in range(n_kg) for g in range(G)]

    scores = []
    for ph, i, kg, g in units:
        qg = kg * G + g
        qblk = q_ref[ph, i * Q:(i + 1) * Q, qg * LANES:(qg + 1) * LANES]
        zero = jnp.zeros_like(qblk)
        qm = jnp.concatenate([jnp.where(lo, qblk, zero), jnp.where(lo, zero, qblk)], axis=0)
        kwin = kfull[ph, i * Q:i * Q + win, kg * LANES:(kg + 1) * LANES]
        s = _dot_nt(qm, kwin) + bias_ref[qg]
        if i == 0:
            s = jnp.where(before_start, NEG, s)
        if i == nsub - 1:
            s = jnp.where(after_end, NEG, s)
        scores.append(s)

    probs = []
    for (ph, i, kg, g), s in zip(units, scores):
        qg = kg * G + g
        m = jnp.max(s, axis=-1, keepdims=True)
        if has_sink:
            sk = jnp.where(top, sink_ref[2 * qg], sink_ref[2 * qg + 1])
            m = jnp.maximum(m, sk)
        p = jnp.exp(s - m)
        den = jnp.sum(p, axis=-1, keepdims=True)
        if has_sink:
            den = den + jnp.exp(sk - m)
        probs.append((p.astype(BF16), den, m))

    lse_tiles = {(ph, i): jnp.zeros((Q, LSE_LANES), F32) for ph in range(P) for i in range(nsub)}
    for (ph, i, kg, g), (p, den, m) in zip(units, probs):
        qg = kg * G + g
        vwin = vfull[ph, i * Q:i * Q + win, kg * LANES:(kg + 1) * LANES]
        o = _dot(p, vwin) / den
        o_ref[ph, i * Q:(i + 1) * Q, qg * LANES:(qg + 1) * LANES] = jnp.where(lo, o[:Q], o[Q:]).astype(BF16)
        if want_lse:
            lse = m + jnp.log(den)
            tile = jnp.where(lane8 == 2 * qg, lse[:Q], lse_tiles[ph, i])
            lse_tiles[ph, i] = jnp.where(lane8 == 2 * qg + 1, lse[Q:], tile)
    if want_lse:
        for (ph, i), tile in lse_tiles.items():
            lse_ref[ph, i * Q:(i + 1) * Q, :] = tile


def _band_bias(slopes, step, W, Q):
    row = np.arange(Q)[:, None]
    col = np.arange(Q + 2 * W)[None, :]
    dist = np.abs(row + W - col)
    bias = -np.asarray(slopes, np.float32)[:, None, None] * (step * dist).astype(np.float32)[None]
    bias = np.where(dist[None] <= W, bias, np.float32(NEG)).astype(np.float32)
    return jnp.asarray(bias.reshape(len(slopes) // 2, 2 * Q, Q + 2 * W))


def _alibi_slopes(n):
    return (2.0 ** (-8.0 * np.arange(1, n + 1, dtype=np.float32) / n)).astype(np.float32)


BAND_ROWS = 512


def _band_tiles(Ls, W):
    R = min(Ls, BAND_ROWS)
    Q = min(R, 128)
    assert Ls % R == 0 and R % Q == 0 and R % W == 0 and Ls % W == 0, (Ls, R, Q, W)
    return R, Q


def _banded_attention(q, k, v, slopes, *, W, n_kg, G, sink=None, want_lse=False):
    bsz, r, Ls, Cq = q.shape
    Ck = k.shape[-1]
    R, Q = _band_tiles(Ls, W)
    nchunks = Ls // R
    nblk = Ls // W
    per = R // W
    P = math.gcd(r, max(1, BAND_ROWS // R))
    bias = _band_bias(slopes, r, W, Q)
    q_spec = pl.BlockSpec((None, P, R, Cq), lambda b, j, c: (b, j, c, 0))
    main = pl.BlockSpec((None, P, R, Ck), lambda b, j, c: (b, j, c, 0))
    prev = pl.BlockSpec((None, P, W, Ck), lambda b, j, c: (b, j, jnp.maximum(c * per - 1, 0), 0))
    nxt = pl.BlockSpec((None, P, W, Ck), lambda b, j, c: (b, j, jnp.minimum((c + 1) * per, nblk - 1), 0))
    in_specs = [q_spec, prev, main, nxt, prev, main, nxt, _const_spec(bias.shape)]
    args = [q, k, k, k, v, v, v, bias]
    if sink is not None:
        in_specs.append(pl.BlockSpec(memory_space=pltpu.SMEM))
        args.append(sink)
    out_specs = [q_spec]
    out_shape = [jax.ShapeDtypeStruct(q.shape, BF16)]
    if want_lse:
        out_specs.append(pl.BlockSpec((None, P, R, LSE_LANES), lambda b, j, c: (b, j, c, 0)))
        out_shape.append(jax.ShapeDtypeStruct((bsz, r, Ls, LSE_LANES), F32))
    kern = functools.partial(_banded_kernel, W=W, Q=Q, R=R, n_kg=n_kg, G=G, has_sink=sink is not None,
                             want_lse=want_lse, nchunks=nchunks)
    outs = pl.pallas_call(
        kern,
        grid=(bsz, r // P, nchunks),
        in_specs=in_specs,
        out_specs=out_specs,
        out_shape=out_shape,
        scratch_shapes=[pltpu.VMEM((P, R + 2 * W, Ck), BF16), pltpu.VMEM((P, R + 2 * W, Ck), BF16)],
        compiler_params=_params("parallel", "parallel", "parallel"),
        name=f"banded_w{W}_r{r}",
    )(*args)
    return (outs[0], outs[1]) if want_lse else outs[0]


VT_ROWS = 80
FLASH_STRIP = 256
FLASH_UNROLL = 4


def _flash_kernel(q_ref, k_ref, vt_ref, o_ref, *, tk, nk):
    tq = q_ref.shape[1]
    ns = tq // FLASH_STRIP
    chains = [(h, c) for h in range(2) for c in range(ns)]
    qs = [q_ref[h, c * FLASH_STRIP:(c + 1) * FLASH_STRIP, :] for h, c in chains]

    def body(t, carry):
        off = pl.multiple_of(t * tk, tk)
        kt = [k_ref[h, pl.ds(off, tk), :] for h in range(2)]
        vt = [vt_ref[h, :, pl.ds(off, tk)] for h in range(2)]
        ss = [_dot_nt(kt[h], qs[n]) for n, (h, _) in enumerate(chains)]
        ms = [jnp.maximum(carry[2 * n], jnp.max(ss[n], axis=0, keepdims=True)) for n in range(len(chains))]
        ps = [jnp.exp2(ss[n] - ms[n]).astype(BF16) for n in range(len(chains))]
        out = []
        for n, (h, _) in enumerate(chains):
            alpha = jnp.exp2(carry[2 * n] - ms[n])
            out += [ms[n], carry[2 * n + 1] * alpha + _dot(vt[h], ps[n])]
        return tuple(out)

    m0 = jnp.full((1, FLASH_STRIP), NEG, F32)
    acc0 = jnp.zeros((VT_ROWS, FLASH_STRIP), F32)
    res = lax.fori_loop(0, nk, body, (m0, acc0) * len(chains), unroll=FLASH_UNROLL)
    accs = [jnp.concatenate([res[2 * (h * ns + c) + 1] for c in range(ns)], axis=1) for h in range(2)]
    outs = [acc[:B_V] / acc[B_V:B_V + 1] for acc in accs]
    o_ref[...] = jnp.concatenate(outs, axis=0).T.astype(BF16)


def _flash_tiles(L):
    tq = min(L, 1024)
    tk = min(L, 512)
    assert L % tq == 0 and L % tk == 0
    return tq, tk


def _latent_attention(qb, kb, vt, bsz, L):
    tq, tk = _flash_tiles(L)
    q4 = qb.reshape(B_HEADS, bsz, L, LANES)
    k4 = kb.reshape(B_HEADS, bsz, L, LANES)
    k_spec = pl.BlockSpec((2, None, L, LANES), lambda b, hp, i: (hp, b, 0, 0))
    vt_spec = pl.BlockSpec((2, VT_ROWS, L), lambda b, hp, i: (hp, 0, b))
    out = pl.pallas_call(
        functools.partial(_flash_kernel, tk=tk, nk=L // tk),
        grid=(bsz, B_HEADS // 2, L // tq),
        in_specs=[pl.BlockSpec((2, None, tq, LANES), lambda b, hp, i: (hp, b, i, 0)), k_spec, vt_spec],
        out_specs=pl.BlockSpec((None, tq, LANES), lambda b, hp, i: (b, i, hp)),
        out_shape=jax.ShapeDtypeStruct((bsz, L, B_WIDTH), BF16),
        compiler_params=_params("parallel", "parallel", "parallel"),
        name="latent_flash",
    )(q4, k4, vt)
    return out.reshape(bsz * L, B_WIDTH)


def _mix_out_kernel(*refs):
    it = iter(refs)
    x_ref, ya_ref, yb_ref = next(it), next(it), next(it)
    c_refs = [next(it) for _ in STRIDES]
    l_refs = [next(it) for _ in STRIDES]
    ga_ref, gb_ref, gc_ref, w_ref = next(it), next(it), next(it), next(it)
    kv_ref, gx_ref, wq_ref, wo_ref = next(it), next(it), next(it), next(it)
    o_ref = next(it)
    c_scr = next(it)
    tm = x_ref.shape[0]
    ngrp = C_WIDTH // LANES

    for n, r in enumerate(STRIDES):
        for j in range(r):
            blk = c_refs[n][j].astype(F32)
            for g in range(ngrp):
                c_scr[n * ngrp + g, pl.ds(j, tm // r, stride=r), :] = blk[:, g * LANES:(g + 1) * LANES]

    ls = [l_ref[...] for l_ref in l_refs]
    mx = functools.reduce(jnp.maximum, ls)
    es = [jnp.exp(l - mx) for l in ls]
    tot = functools.reduce(lambda a, b: a + b, es)
    head_of_lane = lax.broadcasted_iota(jnp.int32, (LSE_LANES, C_WIDTH), 1) // HEAD_DIM
    spread = (head_of_lane == lax.broadcasted_iota(jnp.int32, (LSE_LANES, C_WIDTH), 0)).astype(BF16)
    wide = []
    for e in es:
        w = e / tot
        hi = w.astype(BF16)
        lo = (w - hi.astype(F32)).astype(BF16)
        wide.append(_dot(hi, spread) + _dot(lo, spread))
    groups = []
    for kg in range(ngrp):
        acc = jnp.zeros((tm, LANES), F32)
        for n, w in enumerate(wide):
            acc = acc + c_scr[n * ngrp + kg] * w[:, kg * LANES:(kg + 1) * LANES]
        groups.append(acc)
    yc = jnp.concatenate(groups, axis=-1)
    y = jnp.concatenate([
        _rms(ya_ref[...].astype(F32), ga_ref[...]).astype(BF16),
        _rms(yb_ref[...].astype(F32), gb_ref[...]).astype(BF16),
        _rms(yc, gc_ref[...]).astype(BF16)], axis=-1)
    x = x_ref[...] + _dot(y, w_ref[...])
    o_ref[...] = _cross_attention(x, kv_ref, gx_ref, wq_ref, wo_ref)


def _mix_out(x2, ya, yb, ycs, lses, kvm, lw, L, tm):
    T = x2.shape[0]
    nl = L // tm
    row = lambda w: pl.BlockSpec((tm, w), lambda i: (i, 0))
    weights = (lw['g_out_a'], lw['g_out_b'], lw['g_out_c'], lw['w_out'])
    x_weights = (lw['g_x'], lw['w_xq'], lw['w_xo'])
    n = len(STRIDES)
    return pl.pallas_call(
        _mix_out_kernel,
        grid=(T // tm,),
        in_specs=([row(D_MODEL), row(A_WIDTH), row(B_WIDTH)]
                  + [_phase_spec(r, tm, nl, C_WIDTH) for r in STRIDES]
                  + [row(LSE_LANES)] * n
                  + [_const_spec(w.shape) for w in weights]
                  + [pl.BlockSpec((None,) + kvm.shape[1:], lambda i: (i // nl, 0, 0))]
                  + [_const_spec(w.shape) for w in x_weights]),
        out_specs=row(D_MODEL),
        out_shape=jax.ShapeDtypeStruct((T, D_MODEL), F32),
        scratch_shapes=[pltpu.VMEM((n * C_WIDTH // LANES, tm, LANES), F32)],
        compiler_params=_params("parallel"),
        name="mix_out_cross",
    )(x2, ya, yb, *ycs, *lses, *weights, kvm, *x_weights)


def _norm_matmul_kernel(x_ref, g_ref, w_ref, o_ref):
    o_ref[...] = _dot(_rms(x_ref[...], g_ref[...]).astype(BF16), w_ref[...]).astype(o_ref.dtype)


def _norm_matmul(x2, g, w, tm):
    T, K = x2.shape
    N = w.shape[1]
    return pl.pallas_call(
        _norm_matmul_kernel,
        grid=(T // tm,),
        in_specs=[pl.BlockSpec((tm, K), lambda i: (i, 0)), _const_spec(g.shape), _const_spec(w.shape)],
        out_specs=pl.BlockSpec((tm, N), lambda i: (i, 0)),
        out_shape=jax.ShapeDtypeStruct((T, N), BF16),
        compiler_params=_params("parallel"),
        name="norm_matmul",
    )(x2, g, w)


def _cross_attention(x, kv_ref, g_ref, wq_ref, wo_ref):
    h = _rms(x, g_ref[...]).astype(BF16)
    q = (_dot(h, wq_ref[...]) * (X_HEAD_DIM ** -0.5)).astype(BF16)
    xd = X_HEADS * X_HEAD_DIM
    outs = []
    for hd in range(X_HEADS):
        sl = slice(hd * X_HEAD_DIM, (hd + 1) * X_HEAD_DIM)
        k = kv_ref[:, sl]
        v = kv_ref[:, xd + hd * X_HEAD_DIM:xd + (hd + 1) * X_HEAD_DIM]
        s = _dot_nt(q[:, sl], k)
        p = jnp.exp(s - jnp.max(s, axis=-1, keepdims=True))
        den = jnp.sum(p, axis=-1, keepdims=True)
        outs.append((_dot(p.astype(BF16), v) / den).astype(BF16))
    return x + _dot(jnp.concatenate(outs, axis=-1), wo_ref[...])


def _silu_mul(gate, up):
    return gate * jax.nn.sigmoid(gate) * up


def _swiglu_kernel(x_ref, g_ref, wg_ref, wu_ref, wd_ref, o_ref, h_scr, acc_scr):
    j = pl.program_id(1)

    @pl.when(j == 0)
    def _():
        h_scr[...] = _rms(x_ref[...], g_ref[...]).astype(BF16)
        acc_scr[...] = jnp.zeros_like(acc_scr)

    h = h_scr[...]
    act = _silu_mul(_dot(h, wg_ref[...]), _dot(h, wu_ref[...])).astype(BF16)
    acc_scr[...] += _dot(act, wd_ref[...])

    @pl.when(j == pl.num_programs(1) - 1)
    def _():
        o_ref[...] = x_ref[...] + acc_scr[...]


def _ff_chunk(dff, limit):
    for c in (1792, 1408, 1024, 512, 256, 128):
        if c <= limit and dff % c == 0:
            return c
    raise ValueError(dff)


def _swiglu(x2, g, wg, wu, wd, tm):
    T = x2.shape[0]
    dff = wg.shape[1]
    ck = _ff_chunk(dff, 1408)
    return pl.pallas_call(
        _swiglu_kernel,
        grid=(T // tm, dff // ck),
        in_specs=[pl.BlockSpec((tm, D_MODEL), lambda i, j: (i, 0)),
                  _const_spec(g.shape),
                  pl.BlockSpec((D_MODEL, ck), lambda i, j: (0, j)),
                  pl.BlockSpec((D_MODEL, ck), lambda i, j: (0, j)),
                  pl.BlockSpec((ck, D_MODEL), lambda i, j: (j, 0))],
        out_specs=pl.BlockSpec((tm, D_MODEL), lambda i, j: (i, 0)),
        out_shape=jax.ShapeDtypeStruct((T, D_MODEL), F32),
        scratch_shapes=[pltpu.VMEM((tm, D_MODEL), BF16), pltpu.VMEM((tm, D_MODEL), F32)],
        compiler_params=_params("parallel", "arbitrary"),
        name="swiglu_dense",
    )(x2, g, wg, wu, wd)


def _router_logits(h, wr):
    return jnp.dot(h, wr, preferred_element_type=F32, precision=lax.Precision.HIGHEST)


def _router_kernel(x_ref, g_ref, wr_ref, tri_ref, c0_ref, route_ref, cnt_ref, run_scr):
    @pl.when(pl.program_id(0) == 0)
    def _():
        run_scr[...] = c0_ref[...]

    logits = _router_logits(_rms(x_ref[...], g_ref[...]), wr_ref[...])
    tm = logits.shape[0]
    lane = lax.broadcasted_iota(jnp.int32, (tm, LANES), 1)
    logits = jnp.where(lane < N_EXPERTS, logits, NEG)
    v1 = jnp.max(logits, axis=-1, keepdims=True)
    i1 = jnp.min(jnp.where(logits == v1, lane, LANES), axis=-1, keepdims=True)
    rest = jnp.where(lane == i1, NEG, logits)
    v2 = jnp.max(rest, axis=-1, keepdims=True)
    i2 = jnp.min(jnp.where(rest == v2, lane, LANES), axis=-1, keepdims=True)
    pid = jnp.minimum(i1, i2) * N_EXPERTS + jnp.maximum(i1, i2)
    onehot = (lane == pid).astype(F32)
    before = _dot(tri_ref[...], onehot.astype(BF16)) + run_scr[...]
    rank = jnp.sum(onehot * before, axis=-1, keepdims=True)
    run_scr[...] += jnp.sum(onehot, axis=0, keepdims=True)
    cnt_ref[...] = run_scr[...]
    lane8 = lax.broadcasted_iota(jnp.int32, (tm, LSE_LANES), 1)
    route_ref[...] = jnp.where(lane8 == 0, pid, jnp.where(lane8 == 1, rank.astype(jnp.int32), 0))


def _router(x2, g, wr, counts0, tm):
    T = x2.shape[0]
    tri = (np.arange(tm)[:, None] > np.arange(tm)[None, :]).astype(np.float32)
    tri = jnp.asarray(tri, BF16)
    return pl.pallas_call(
        _router_kernel,
        grid=(T // tm,),
        in_specs=[pl.BlockSpec((tm, D_MODEL), lambda i: (i, 0)), _const_spec(g.shape), _const_spec(wr.shape),
                  _const_spec(tri.shape), _const_spec(counts0.shape)],
        out_specs=[pl.BlockSpec((tm, LSE_LANES), lambda i: (i, 0)), _const_spec(counts0.shape)],
        out_shape=[jax.ShapeDtypeStruct((T, LSE_LANES), jnp.int32), jax.ShapeDtypeStruct(counts0.shape, F32)],
        scratch_shapes=[pltpu.VMEM(counts0.shape, F32)],
        compiler_params=_params("arbitrary"),
        name="router",
    )(x2, g, wr, tri, counts0)


ROW_SUBLANES = D_MODEL // LANES


def _slab_to_rows(ref, n):
    return jnp.concatenate([ref[pl.ds(c, n, stride=ROW_SUBLANES), :] for c in range(ROW_SUBLANES)], axis=-1)


def _rows_to_slab(ref, rows, n):
    for c in range(ROW_SUBLANES):
        ref[pl.ds(c, n, stride=ROW_SUBLANES), :] = rows[:, c * LANES:(c + 1) * LANES]


def _slab(ref, row):
    return ref.at[pl.ds(pl.multiple_of(row * ROW_SUBLANES, ROW_SUBLANES), ROW_SUBLANES)]


MOVE_UNROLL = 8


def _move_rows(n, copy_of_row, copy_of_all):
    def issue(g, _):
        for k in range(MOVE_UNROLL):
            copy_of_row(g * MOVE_UNROLL + k).start(priority=k % 2)
        return 0

    lax.fori_loop(0, n // MOVE_UNROLL, issue, 0)
    copy_of_all().wait()


def _dispatch_kernel(slot_ref, x_ref, prev_ref, o_ref, slab, sem, *, tm):
    del prev_ref
    _rows_to_slab(slab, x_ref[...], tm)

    def copy_of_row(r):
        slot = slot_ref[0, r]
        return pltpu.make_async_copy(_slab(slab, r), _slab(o_ref, slot), sem)

    def copy_of_all():
        return pltpu.make_async_copy(slab, o_ref.at[pl.ds(0, tm * ROW_SUBLANES)], sem)

    _move_rows(tm, copy_of_row, copy_of_all)


def _slots_kernel(route_ref, start_ref, o_ref):
    route = route_ref[...]
    tm = route.shape[0]
    lane = lax.broadcasted_iota(jnp.int32, (tm, LANES), 1)
    first = jnp.sum(jnp.where(lane == route[:, 0:1], start_ref[...], 0), axis=-1, keepdims=True)
    o_ref[...] = jnp.broadcast_to(first + route[:, 1:2], o_ref.shape)


def _slots(route, start, tm):
    T = route.shape[0]
    slots = pl.pallas_call(
        _slots_kernel,
        grid=(T // tm,),
        in_specs=[pl.BlockSpec((tm, LSE_LANES), lambda i: (i, 0)), _const_spec(start.shape)],
        out_specs=pl.BlockSpec((tm, LSE_LANES), lambda i: (i, 0)),
        out_shape=jax.ShapeDtypeStruct((T, LSE_LANES), jnp.int32),
        compiler_params=_params("parallel"),
        name="slots",
    )(route, start)
    return slots[:, 0].reshape(T // tm, 1, tm)


def _dispatch(x2, slots, sorted_rows, tm):
    T = x2.shape[0]
    return pl.pallas_call(
        functools.partial(_dispatch_kernel, tm=tm),
        grid=(T // tm,),
        in_specs=[pl.BlockSpec((None, 1, tm), lambda i: (i, 0, 0), memory_space=pltpu.SMEM),
                  pl.BlockSpec((tm, D_MODEL), lambda i: (i, 0)),
                  pl.BlockSpec(memory_space=pl.ANY)],
        out_specs=pl.BlockSpec(memory_space=pl.ANY),
        out_shape=jax.ShapeDtypeStruct(sorted_rows.shape, sorted_rows.dtype),
        scratch_shapes=[pltpu.VMEM((tm * ROW_SUBLANES, LANES), F32), pltpu.SemaphoreType.DMA],
        input_output_aliases={2: 0},
        compiler_params=pltpu.CompilerParams(dimension_semantics=("arbitrary",), vmem_limit_bytes=VMEM_LIMIT,
                                             has_side_effects=True),
        name="dispatch",
    )(slots, x2, sorted_rows)


def _unpermute_kernel(slot_ref, y_ref, o_ref, slab, sem, *, tm):
    def copy_of_row(r):
        slot = slot_ref[0, r]
        return pltpu.make_async_copy(_slab(y_ref, slot), _slab(slab, r), sem)

    def copy_of_all():
        return pltpu.make_async_copy(y_ref.at[pl.ds(0, tm * ROW_SUBLANES)], slab, sem)

    _move_rows(tm, copy_of_row, copy_of_all)
    o_ref[...] = _slab_to_rows(slab, tm)


def _unpermute(y_sorted, slots, T, tm):
    return pl.pallas_call(
        functools.partial(_unpermute_kernel, tm=tm),
        grid=(T // tm,),
        in_specs=[pl.BlockSpec((None, 1, tm), lambda i: (i, 0, 0), memory_space=pltpu.SMEM),
                  pl.BlockSpec(memory_space=pl.ANY)],
        out_specs=pl.BlockSpec((tm, D_MODEL), lambda i: (i, 0)),
        out_shape=jax.ShapeDtypeStruct((T, D_MODEL), F32),
        scratch_shapes=[pltpu.VMEM((tm * ROW_SUBLANES, LANES), F32), pltpu.SemaphoreType.DMA],
        compiler_params=_params("arbitrary"),
        name="unpermute",
    )(slots, y_sorted)


def _pair_experts_kernel(ta_ref, tb_ref, nv_ref, xs_ref, g_ref, wr_ref, wg_ref, wu_ref, wd_ref, gf_ref, o_ref,
                         x_scr, h_scr, gate_scr, acc_scr, *, nj, final_norm):
    i = pl.program_id(0)
    j = pl.program_id(1)
    tm = x_scr.shape[0]

    @pl.when(i < nv_ref[0])
    def _():
        @pl.when(j == 0)
        def _():
            x = _slab_to_rows(xs_ref, tm)
            x_scr[...] = x
            h = _rms(x, g_ref[...])
            h_scr[...] = h.astype(BF16)
            la = jnp.sum(h * wr_ref[pl.ds(ta_ref[i], 1), :], axis=-1, keepdims=True)
            lb = jnp.sum(h * wr_ref[pl.ds(tb_ref[i], 1), :], axis=-1, keepdims=True)
            mx = jnp.maximum(la, lb)
            ea, eb = jnp.exp(la - mx), jnp.exp(lb - mx)
            gate_scr[0] = ea / (ea + eb)
            gate_scr[1] = eb / (ea + eb)
            acc_scr[...] = jnp.zeros_like(acc_scr)

        h = h_scr[...]
        gate = jnp.where(j < nj, gate_scr[0], gate_scr[1])
        act = (_silu_mul(_dot(h, wg_ref[...]), _dot(h, wu_ref[...])) * gate).astype(BF16)
        acc_scr[...] += _dot(act, wd_ref[...])

        @pl.when(j == 2 * nj - 1)
        def _():
            out = x_scr[...] + acc_scr[...]
            if final_norm:
                out = _rms(out, gf_ref[...])
            _rows_to_slab(o_ref, out, tm)

    @pl.when(jnp.logical_and(i >= nv_ref[0], j == 2 * nj - 1))
    def _():
        o_ref[...] = jnp.zeros_like(o_ref)


def _pair_experts(xs, g, wr, wg, wu, wd, g_final, tile_a, tile_b, n_valid, tm, final_norm):
    rows = xs.shape[0] // ROW_SUBLANES
    dff = wg.shape[2]
    ck = _ff_chunk(dff, 1792)
    nj = dff // ck

    def expert(i, j, ta, tb):
        return jnp.where(j < nj, ta[i], tb[i])

    def chunk(i, j, nv):
        return jnp.where(i < nv[0], j % nj, nj - 1)

    slab_spec = pl.BlockSpec((tm * ROW_SUBLANES, LANES), lambda i, j, ta, tb, nv: (i, 0))
    const = lambda a: pl.BlockSpec(a.shape, lambda i, j, ta, tb, nv: (0,) * a.ndim)
    grid_spec = pltpu.PrefetchScalarGridSpec(
        num_scalar_prefetch=3,
        grid=(rows // tm, 2 * nj),
        in_specs=[slab_spec, const(g), const(wr),
                  pl.BlockSpec((None, D_MODEL, ck), lambda i, j, ta, tb, nv: (expert(i, j, ta, tb), 0, chunk(i, j, nv))),
                  pl.BlockSpec((None, D_MODEL, ck), lambda i, j, ta, tb, nv: (expert(i, j, ta, tb), 0, chunk(i, j, nv))),
                  pl.BlockSpec((None, ck, D_MODEL), lambda i, j, ta, tb, nv: (expert(i, j, ta, tb), chunk(i, j, nv), 0)),
                  const(g_final)],
        out_specs=slab_spec,
        scratch_shapes=[pltpu.VMEM((tm, D_MODEL), F32), pltpu.VMEM((tm, D_MODEL), BF16),
                        pltpu.VMEM((2, tm, 1), F32), pltpu.VMEM((tm, D_MODEL), F32)],
    )
    return pl.pallas_call(
        functools.partial(_pair_experts_kernel, nj=nj, final_norm=final_norm),
        grid_spec=grid_spec,
        out_shape=jax.ShapeDtypeStruct(xs.shape, F32),
        compiler_params=_params("parallel", "arbitrary"),
        name="pair_experts",
    )(tile_a, tile_b, n_valid, xs, g, wr, wg, wu, wd, g_final)


def _plan_pairs(counts, n_tiles, tm):
    counts = counts.reshape(-1).astype(jnp.int32)
    tiles = (counts + tm - 1) // tm
    tile_end = jnp.cumsum(tiles)
    start = (tile_end - tiles) * tm
    n_valid = tile_end[-1:]
    tile_ids = jnp.minimum(jnp.arange(n_tiles), n_valid[0] - 1)
    tile_pid = jnp.sum(tile_ids[:, None] >= tile_end[None, :], axis=-1)
    tile_a = (tile_pid // N_EXPERTS).astype(jnp.int32)
    tile_b = (tile_pid % N_EXPERTS).astype(jnp.int32)
    return start.astype(jnp.int32), tile_a, tile_b, n_valid.astype(jnp.int32)


def _prep_layer(l, p):
    offs = np.cumsum((0,) + IN_SIZES)
    w_in = p['w_in'][l]
    qa, ka, va, cq, ckv, kr, qc, kc, vc = [w_in[:, int(offs[i]):int(offs[i + 1])] for i in range(9)]
    scale = HEAD_DIM ** -0.5
    order = list(A_HEAD_ORDER)
    qa = qa.reshape(D_MODEL, A_Q_HEADS, HEAD_DIM)[:, order, :].reshape(D_MODEL, A_WIDTH) * scale
    half = B_ROPE // 2
    z = lambda *s: jnp.zeros(s, F32)
    kr_pad = jnp.concatenate([z(D_MODEL, B_NOPE), kr, z(D_MODEL, LANES - B_NOPE - B_ROPE)], axis=1)
    kr_rot = jnp.concatenate([z(D_MODEL, B_NOPE), -kr[:, half:], kr[:, :half], z(D_MODEL, LANES - B_NOPE - B_ROPE)], axis=1)
    uq = p['mla_w_uq'][l].reshape(B_Q_LORA, B_HEADS, B_NOPE + B_ROPE)
    padq = z(B_Q_LORA, B_HEADS, LANES - B_NOPE - B_ROPE)
    uq_pad = jnp.concatenate([uq, padq], axis=-1).reshape(B_Q_LORA, B_HEADS * LANES)
    uq_rot = jnp.concatenate([z(B_Q_LORA, B_HEADS, B_NOPE), -uq[..., B_NOPE + half:], uq[..., B_NOPE:B_NOPE + half], padq],
                             axis=-1).reshape(B_Q_LORA, B_HEADS * LANES)
    ukv = p['mla_w_ukv'][l].reshape(B_KV_LORA, B_HEADS, B_NOPE + B_V)
    zk = z(B_KV_LORA, B_HEADS, HALF)
    uk_pad = jnp.concatenate([ukv[..., :B_NOPE], zk], axis=-1).reshape(B_KV_LORA, B_HEADS * LANES)
    uv_pad = jnp.concatenate([ukv[..., B_NOPE:], zk], axis=-1).reshape(B_KV_LORA, B_HEADS * LANES)

    g_out = p['mix_out_norm_g'][l]
    w_out = p['w_out'][l]
    a_rows = np.concatenate([np.arange(HEAD_DIM) + HEAD_DIM * h for h in A_HEAD_ORDER])
    w_out = jnp.concatenate([w_out[:A_WIDTH][a_rows], w_out[A_WIDTH:]], axis=0)
    g_out_a = g_out[:A_WIDTH][a_rows]
    row = lambda v: v.reshape(1, -1).astype(F32)
    return {
        'g_mix': row(p['norm_mix_g'][l]),
        'wa': jnp.concatenate([qa, ka, va], axis=1).astype(BF16),
        'wc': jnp.concatenate([qc * scale, kc, vc], axis=1).astype(BF16),
        'wb': jnp.concatenate([cq, ckv], axis=1).astype(BF16),
        'wkr': jnp.concatenate([kr_pad, kr_rot], axis=1).astype(BF16),
        'g_q': row(p['mla_q_norm_g'][l]),
        'g_kv': row(p['mla_kv_norm_g'][l]),
        'wuq': jnp.concatenate([uq_pad, uq_rot], axis=1).astype(BF16),
        'wukv': jnp.concatenate([uk_pad, uv_pad], axis=1).astype(BF16),
        'sink': p['sink_a'][l][np.array(A_HEAD_ORDER)].astype(F32),
        'g_out_a': row(g_out_a),
        'g_out_b': row(g_out[A_WIDTH:A_WIDTH + B_WIDTH]),
        'g_out_c': row(g_out[A_WIDTH + B_WIDTH:]),
        'w_out': w_out.astype(BF16),
        'g_x': row(p['norm_x_g'][l]),
        'g_mem': row(p['norm_mem_g'][l]),
        'w_xq': p['w_xq'][l].astype(BF16),
        'w_xkv': p['w_xkv'][l].astype(BF16),
        'w_xo': p['w_xo'][l].astype(BF16),
        'g_ffn': row(p['norm_ffn_g'][l]),
    }


def _prep_ffn(w_gu, w_down):
    dff = w_down.shape[-2]
    return w_gu[..., :dff].astype(BF16), w_gu[..., dff:].astype(BF16), w_down.astype(BF16)


def _rope_tables(L):
    half = B_ROPE // 2
    pos = jnp.arange(L, dtype=F32)
    inv = ROPE_THETA ** (-jnp.arange(half, dtype=F32) / half)
    ang = pos[:, None] * inv[None, :]
    cos2 = jnp.tile(jnp.cos(ang), (1, 2))
    sin2 = jnp.tile(jnp.sin(ang), (1, 2))
    pad = jnp.zeros((L, LANES - B_NOPE - B_ROPE), F32)
    qscale = (B_NOPE + B_ROPE) ** -0.5 * LOG2E
    cq = jnp.concatenate([jnp.ones((L, B_NOPE), F32), cos2, pad], axis=1) * qscale
    sq = jnp.concatenate([jnp.zeros((L, B_NOPE), F32), sin2, pad], axis=1) * qscale
    ck = jnp.concatenate([jnp.zeros((L, B_NOPE), F32), cos2, pad], axis=1)
    sk = jnp.concatenate([jnp.zeros((L, B_NOPE), F32), sin2, pad], axis=1)
    return cq, sq, ck, sk


TOKEN_TILE = 512
EXPERT_TILE = 512
FFN_TILE = 1024
MOVE_TILE = 1024
N_PAIRS_USED = N_EXPERTS * (N_EXPERTS - 1) // 2


def _mixer_and_cross(x2, mem, lw, bsz, L):
    T = bsz * L
    tm = min(TOKEN_TILE, L)
    tabs = _rope_tables(L)
    slopes_a = _alibi_slopes(A_Q_HEADS)[list(A_HEAD_ORDER)]
    slopes_c = _alibi_slopes(C_HEADS)
    (qa, ka, va), qkv_c, (qb, kb, vb) = _in_proj(x2, lw, tabs, bsz, L, tm)
    as_seq = lambda t: t.reshape(bsz, 1, L, t.shape[-1])
    ya = _banded_attention(as_seq(qa), as_seq(ka), as_seq(va), slopes_a, W=A_WINDOW, n_kg=1, G=2, sink=lw['sink'])
    ya = ya.reshape(T, A_WIDTH)
    yb = _latent_attention(qb, kb, vb, bsz, L)
    ycs, lses = [], []
    for (w, r), (qc, kc, vc) in zip(DILATED_PAIRS, qkv_c):
        o, lse = _banded_attention(qc, kc, vc, slopes_c, W=w // (2 * r), n_kg=C_HEADS // 2, G=1, want_lse=True)
        ycs.append(o)
        lses.append(jnp.transpose(lse, (0, 2, 1, 3)).reshape(T, LSE_LANES))
    mem2 = mem.reshape(-1, D_MODEL)
    kvm = _norm_matmul(mem2, lw['g_mem'], lw['w_xkv'], min(512, mem2.shape[0]))
    return _mix_out(x2, ya, yb, ycs, lses, kvm.reshape(bsz, mem.shape[1], -1), lw, L, tm)


def _routed_swiglu(xs2, g, wr, wg, wu, wd, final_g, final_norm):
    total = sum(x.shape[0] for x in xs2)
    et = min(EXPERT_TILE, total)
    n_tiles = total // et + N_PAIRS_USED
    counts = jnp.zeros((1, LANES), F32)
    routes = []
    for x2 in xs2:
        route, counts = _router(x2, g, wr, counts, min(TOKEN_TILE, x2.shape[0]))
        routes.append(route)
    start, tile_a, tile_b, n_valid = _plan_pairs(counts, n_tiles, et)
    move = [min(MOVE_TILE, x2.shape[0]) for x2 in xs2]
    slots = [_slots(route, start.reshape(1, LANES), mt) for route, mt in zip(routes, move)]
    sorted_rows = jnp.zeros((n_tiles * et * ROW_SUBLANES, LANES), F32)
    for x2, sl, mt in zip(xs2, slots, move):
        sorted_rows = _dispatch(x2, sl, sorted_rows, mt)
    wr_rows = wr[:, :N_EXPERTS].T
    y_sorted = _pair_experts(sorted_rows, g, wr_rows, wg, wu, wd, final_g, tile_a, tile_b, n_valid, et, final_norm)
    return [_unpermute(y_sorted, sl, x2.shape[0], mt) for x2, sl, mt in zip(xs2, slots, move)]


def _encoder(groups, layers, ffn, moe, routers, final_g):
    shapes = [x.shape for x, _ in groups]
    xs2 = [x.reshape(-1, D_MODEL) for x, _ in groups]
    depth = len(layers)
    normed = False
    for l, lw in enumerate(layers):
        xs2 = [_mixer_and_cross(x2, mem, lw, shp[0], shp[1]) for x2, (_, mem), shp in zip(xs2, groups, shapes)]
        if l % 2 == 0:
            wg, wu, wd = ffn[l // 2]
            xs2 = [_swiglu(x2, lw['g_ffn'], wg, wu, wd, min(FFN_TILE, x2.shape[0])) for x2 in xs2]
        else:
            wg, wu, wd = moe[l // 2]
            normed = l == depth - 1
            xs2 = _routed_swiglu(xs2, lw['g_ffn'], routers[l // 2], wg, wu, wd, final_g, final_norm=normed)
    assert normed, "the final rmsnorm is fused into the last layer's routed SwiGLU"
    return tuple(x2.reshape(shp) for x2, shp in zip(xs2, shapes))


def kernel(x_prompt, x_sample, mem_prompt, mem_sample, norm_mix_g, w_in, sink_a, mla_q_norm_g, mla_kv_norm_g, mla_w_uq, mla_w_ukv, mix_out_norm_g, w_out, norm_x_g, norm_mem_g, w_xq, w_xkv, w_xo, norm_ffn_g, ffn_w_gu, ffn_w_down, moe_router, moe_w_gu, moe_w_down, final_norm_g):
    p = dict(norm_mix_g=norm_mix_g, w_in=w_in, sink_a=sink_a, mla_q_norm_g=mla_q_norm_g, mla_kv_norm_g=mla_kv_norm_g,
             mla_w_uq=mla_w_uq, mla_w_ukv=mla_w_ukv, mix_out_norm_g=mix_out_norm_g, w_out=w_out, norm_x_g=norm_x_g,
             norm_mem_g=norm_mem_g, w_xq=w_xq, w_xkv=w_xkv, w_xo=w_xo, norm_ffn_g=norm_ffn_g)
    depth = w_in.shape[0]
    layers = [_prep_layer(l, p) for l in range(depth)]
    ffn = [_prep_ffn(ffn_w_gu[i], ffn_w_down[i]) for i in range(ffn_w_gu.shape[0])]
    moe = [_prep_ffn(moe_w_gu[i], moe_w_down[i]) for i in range(moe_w_gu.shape[0])]
    routers = [jnp.pad(moe_router[i].astype(F32), ((0, 0), (0, LANES - N_EXPERTS))) for i in range(moe_router.shape[0])]
    final_g = final_norm_g.reshape(1, -1).astype(F32)
    return _encoder([(x_prompt, mem_prompt), (x_sample, mem_sample)], layers, ffn, moe, routers, final_g)
```
